```python
import math
import jax, jax.numpy as jnp
from jax import lax
import numpy as np


D_MODEL = 2048
BATCH = 1
SEQ = 8192
DEPTH = 4

MEM_LEN = 256
GLA_HEADS = 4
GLA_DK = 128
GLA_DV = 256
GLA_GATE_RANK = 16
GLA_GATE_NORMALIZER = 16.0
GLA_CHUNK = 64
DIL_HEADS = 4
DIL_HEAD_DIM = 128
DIL_CONFIGS = ((128, 1), (512, 4), (2048, 16))
MEM_HEADS = 4
MEM_HEAD_DIM = 128
REL_BUCKETS = 32
REL_MAX_DISTANCE = 1024
D_FF = 4 * D_MODEL
EPS = 1e-6
NEG_INF = -1e30

GLA_QK_WIDTH = GLA_HEADS * GLA_DK
GLA_V_WIDTH = GLA_HEADS * GLA_DV
DIL_WIDTH = DIL_HEADS * DIL_HEAD_DIM
MEM_WIDTH = MEM_HEADS * MEM_HEAD_DIM
MIX_WIDTH = GLA_V_WIDTH + DIL_WIDTH + MEM_WIDTH
IN_SPLITS = (GLA_QK_WIDTH, GLA_QK_WIDTH, GLA_V_WIDTH, GLA_V_WIDTH, GLA_GATE_RANK, GLA_GATE_RANK,
             DIL_WIDTH, DIL_WIDTH, DIL_WIDTH, MEM_WIDTH)
IN_WIDTH = 2 * GLA_QK_WIDTH + 2 * GLA_V_WIDTH + 2 * GLA_GATE_RANK + 3 * DIL_WIDTH + MEM_WIDTH

kernel_name = 'hymba_gla_dilated_memxattn_encoder'


def rms_norm(x, gain):
    xf = x.astype(jnp.float32)
    y = xf * lax.rsqrt(jnp.mean(xf * xf, axis=-1, keepdims=True) + EPS)
    return (y * gain.astype(jnp.float32)).astype(x.dtype)


def head_rms_norm(x, gain):
    h, e = x.shape[-2:]
    xf = x.astype(jnp.float32)
    y = xf * lax.rsqrt(jnp.mean(xf * xf, axis=-1, keepdims=True) + EPS)
    return y * gain.reshape(h, e).astype(jnp.float32)


def split_columns(t, sizes):
    outs, start = [], 0
    for s in sizes:
        outs.append(t[..., start:start + s])
        start += s
    return outs


def t5_bucket(rel):
    half = REL_BUCKETS // 2
    max_exact = half // 2
    ret = jnp.where(rel > 0, half, 0)
    n = jnp.abs(rel)
    nf = jnp.maximum(n, 1).astype(jnp.float32)
    large = max_exact + (jnp.log(nf / max_exact) / math.log(REL_MAX_DISTANCE / max_exact)
                         * (half - max_exact)).astype(jnp.int32)
    large = jnp.minimum(large, half - 1)
    return ret + jnp.where(n < max_exact, n, large)


def gla_direction(q, k, v, g):
    b_, h_, s_, dk = q.shape
    dv = v.shape[-1]
    c = GLA_CHUNK
    n = s_ // c
    q, k, g = [t.reshape(b_, h_, n, c, dk) for t in (q, k, g)]
    v = v.reshape(b_, h_, n, c, dv)
    b = jnp.cumsum(g, axis=3)
    b_last = b[:, :, :, -1:, :]
    q_dec = q * jnp.exp(b)
    k_inv = k * jnp.exp(-b)
    k_end = k * jnp.exp(b_last - b)
    causal = jnp.tril(jnp.ones((c, c), dtype=bool))
    a = jnp.where(causal, jnp.einsum('bhncd,bhnsd->bhncs', q_dec, k_inv), 0.0)
    o_intra = jnp.einsum('bhncs,bhnsv->bhncv', a, v)
    d_state = jnp.einsum('bhncd,bhncv->nbhdv', k_end, v)
    decay = jnp.moveaxis(jnp.exp(b_last[:, :, :, 0, :]), 2, 0)
    q_chunks = jnp.moveaxis(q_dec, 2, 0)

    def step(state, inp):
        ds, dec, qc = inp
        out = jnp.einsum('bhcd,bhdv->bhcv', qc, state)
        return dec[..., None] * state + ds, out

    state0 = jnp.zeros((b_, h_, dk, dv), dtype=q.dtype)
    _, o_inter = lax.scan(step, state0, (d_state, decay, q_chunks))
    o = o_intra + jnp.moveaxis(o_inter, 0, 2)
    return o.reshape(b_, h_, s_, dv)


def gla_mixer(q, k, v, r, lr_f, lr_b, up_f, bias_f, up_b, bias_b, norm_gain):
    b_, s_ = q.shape[:2]

    def heads(t, e):
        return t.reshape(b_, s_, GLA_HEADS, e).transpose(0, 2, 1, 3).astype(jnp.float32)

    def log_gate(lr, up, bias):
        logits = (jnp.einsum('bsr,rk->bsk', lr, up) + bias).astype(jnp.float32)
        return heads(jax.nn.log_sigmoid(logits) / GLA_GATE_NORMALIZER, GLA_DK)

    qh = heads(q, GLA_DK) * (GLA_DK ** -0.5)
    kh = heads(k, GLA_DK)
    vh = heads(v, GLA_DV)
    g_f = log_gate(lr_f, up_f, bias_f)
    g_b = log_gate(lr_b, up_b, bias_b)
    o_f = gla_direction(qh, kh, vh, g_f)
    flip = lambda t: jnp.flip(t, axis=2)
    o_b = flip(gla_direction(flip(qh), flip(kh), flip(vh), flip(g_b)))
    o = (o_f + o_b).transpose(0, 2, 1, 3)
    gate = jax.nn.silu(r.reshape(b_, s_, GLA_HEADS, GLA_DV).astype(jnp.float32))
    o = head_rms_norm(o, norm_gain) * gate
    return o.reshape(b_, s_, GLA_V_WIDTH)


def dilated_branch(q, k, v, rel_table, window, dilation):
    b_, s_, h_, e = q.shape
    w = window // (2 * dilation)
    l = s_ // dilation
    nb = -(-l // w)
    lp = nb * w

    def stride(t):
        return t.reshape(b_, l, dilation, h_, e).transpose(0, 2, 3, 1, 4)

    qs = jnp.pad(stride(q), ((0, 0), (0, 0), (0, 0), (0, lp - l), (0, 0))).reshape(b_, dilation, h_, nb, w, e)

    def windows(t):
        tp = jnp.pad(stride(t), ((0, 0), (0, 0), (0, 0), (w, lp - l + w), (0, 0)))
        tp = tp.reshape(b_, dilation, h_, nb + 2, w, e)
        return jnp.concatenate([tp[:, :, :, :-2], tp[:, :, :, 1:-1], tp[:, :, :, 2:]], axis=4)

    kw = windows(k)
    vw = windows(v)
    rel_sub = jnp.arange(3 * w)[None, :] - w - jnp.arange(w)[:, None]
    bias = jnp.transpose(rel_table[t5_bucket(rel_sub * dilation)], (2, 0, 1)).astype(jnp.float32)
    key_pos = jnp.arange(nb)[:, None] * w - w + jnp.arange(3 * w)[None, :]
    mask = (jnp.abs(rel_sub) <= w)[None] & ((key_pos >= 0) & (key_pos < l))[:, None, :]
    s = jnp.einsum('bdhnqe,bdhnke->bdhnqk', qs, kw, preferred_element_type=jnp.float32) * (e ** -0.5)
    s = jnp.where(mask, s + bias[None, None, :, None], NEG_INF)
    m = jnp.max(s, axis=-1, keepdims=True)
    p = jnp.exp(s - m)
    den = jnp.sum(p, axis=-1, keepdims=True)
    o = jnp.einsum('bdhnqk,bdhnke->bdhnqe', p, vw.astype(jnp.float32)) / den
    lse = m + jnp.log(den)

    def unstride(t):
        t = t.reshape(b_, dilation, h_, lp, t.shape[-1])[:, :, :, :l]
        return t.transpose(0, 3, 1, 2, 4).reshape(b_, s_, h_, t.shape[-1])

    return unstride(o), unstride(lse)[..., 0]


def dilated_mixer(q, k, v, rel_table, norm_gain):
    b_, s_ = q.shape[:2]
    qh, kh, vh = [t.reshape(b_, s_, DIL_HEADS, DIL_HEAD_DIM) for t in (q, k, v)]
    outs, lses = [], []
    for window, dilation in DIL_CONFIGS:
        o, lse = dilated_branch(qh, kh, vh, rel_table, window, dilation)
        outs.append(o)
        lses.append(lse)
    weights = jax.nn.softmax(jnp.stack(lses, axis=0), axis=0)
    o = jnp.einsum('rbsh,rbshe->bshe', weights, jnp.stack(outs, axis=0))
    return head_rms_norm(o, norm_gain).reshape(b_, s_, DIL_WIDTH)


def memory_mixer(q, mem, mem_gain, w_mem_kv, norm_gain):
    b_, s_ = q.shape[:2]
    qh = q.reshape(b_, s_, MEM_HEADS, MEM_HEAD_DIM)
    kv = jnp.einsum('bmd,dk->bmk', rms_norm(mem, mem_gain), w_mem_kv)
    km, vm = [t.reshape(b_, mem.shape[1], MEM_HEADS, MEM_HEAD_DIM) for t in split_columns(kv, (MEM_WIDTH, MEM_WIDTH))]
    s = jnp.einsum('bshe,bmhe->bhsm', qh, km, preferred_element_type=jnp.float32) * (MEM_HEAD_DIM ** -0.5)
    p = jax.nn.softmax(s, axis=-1)
    o = jnp.einsum('bhsm,bmhe->bshe', p, vm.astype(jnp.float32))
    return head_rms_norm(o, norm_gain).reshape(b_, s_, MEM_WIDTH)


def setup_inputs(seed: int = 0) -> dict:
    key = jax.random.key(seed)
    ks = jax.random.split(key, 24)
    f32 = jnp.float32
    nrm = lambda k, shape, scale: jax.random.normal(k, shape, f32) * scale
    gain = lambda k, shape: 1.0 + 0.02 * jax.random.normal(k, shape, f32)
    return {
        'x': nrm(ks[0], (BATCH, SEQ, D_MODEL), 1.0),
        'mem': nrm(ks[1], (BATCH, MEM_LEN, D_MODEL), 1.0),
        'norm_mix': gain(ks[2], (DEPTH, D_MODEL)),
        'w_in': nrm(ks[3], (DEPTH, D_MODEL, IN_WIDTH), D_MODEL ** -0.5),
        'gla_gate_up_fwd': nrm(ks[4], (DEPTH, GLA_GATE_RANK, GLA_QK_WIDTH), GLA_GATE_RANK ** -0.5),
        'gla_gate_bias_fwd': nrm(ks[5], (DEPTH, GLA_QK_WIDTH), 0.1),
        'gla_gate_up_bwd': nrm(ks[6], (DEPTH, GLA_GATE_RANK, GLA_QK_WIDTH), GLA_GATE_RANK ** -0.5),
        'gla_gate_bias_bwd': nrm(ks[7], (DEPTH, GLA_QK_WIDTH), 0.1),
        'gla_norm': gain(ks[8], (DEPTH, GLA_V_WIDTH)),
        'rel_bias': nrm(ks[9], (REL_BUCKETS, DIL_HEADS), 0.5),
        'dil_norm': gain(ks[10], (DEPTH, DIL_WIDTH)),
        'mem_norm': gain(ks[11], (DEPTH, D_MODEL)),
        'w_mem_kv': nrm(ks[12], (DEPTH, D_MODEL, 2 * MEM_WIDTH), D_MODEL ** -0.5),
        'mem_out_norm': gain(ks[13], (DEPTH, MEM_WIDTH)),
        'w_out': nrm(ks[14], (DEPTH, MIX_WIDTH, D_MODEL), MIX_WIDTH ** -0.5),
        'norm_mlp': gain(ks[15], (DEPTH, D_MODEL)),
        'w_up': nrm(ks[16], (DEPTH, D_MODEL, D_FF), D_MODEL ** -0.5),
        'w_down': nrm(ks[17], (DEPTH, D_FF, D_MODEL), D_FF ** -0.5),
        'norm_final': gain(ks[18], (D_MODEL,)),
    }


def reference(x, mem, norm_mix, w_in, gla_gate_up_fwd, gla_gate_bias_fwd, gla_gate_up_bwd, gla_gate_bias_bwd,
              gla_norm, rel_bias, dil_norm, mem_norm, w_mem_kv, mem_out_norm, w_out, norm_mlp, w_up, w_down,
              norm_final):
    for l in range(DEPTH):
        h = rms_norm(x, norm_mix[l])
        proj = jnp.einsum('bsd,dk->bsk', h, w_in[l])
        (g_q, g_k, g_v, g_r, lr_f, lr_b, d_q, d_k, d_v, m_q) = split_columns(proj, IN_SPLITS)
        gla_out = gla_mixer(g_q, g_k, g_v, g_r, lr_f, lr_b, gla_gate_up_fwd[l], gla_gate_bias_fwd[l],
                            gla_gate_up_bwd[l], gla_gate_bias_bwd[l], gla_norm[l])
        dil_out = dilated_mixer(d_q, d_k, d_v, rel_bias, dil_norm[l])
        mem_out = memory_mixer(m_q, mem, mem_norm[l], w_mem_kv[l], mem_out_norm[l])
        mixed = jnp.concatenate([gla_out, dil_out, mem_out], axis=-1).astype(x.dtype)
        x = x + jnp.einsum('bsk,kd->bsd', mixed, w_out[l])
        h = rms_norm(x, norm_mlp[l])
        u = jnp.square(jax.nn.relu(jnp.einsum('bsd,df->bsf', h, w_up[l])))
        x = x + jnp.einsum('bsf,fd->bsd', u, w_down[l])
    return rms_norm(x, norm_final)
```

```python
import functools
import math

import jax
import jax.numpy as jnp
from jax import lax
from jax.experimental import pallas as pl
from jax.experimental.pallas import tpu as pltpu

F32 = jnp.float32
BF16 = jnp.bfloat16

D_MODEL = 2048
SEQ = 8192
DEPTH = 4
MEM_LEN = 256
GLA_HEADS = 4
GLA_DK = 128
GLA_DV = 256
GLA_GATE_RANK = 16
GLA_GATE_NORMALIZER = 16.0
GLA_CHUNK = 64
DIL_HEADS = 4
DIL_HEAD_DIM = 128
DIL_CONFIGS = ((128, 1), (512, 4), (2048, 16))
MEM_HEADS = 4
MEM_HEAD_DIM = 128
REL_BUCKETS = 32
REL_MAX_DISTANCE = 1024
D_FF = 4 * D_MODEL
EPS = 1e-6
NEG_INF = -1e30

GLA_QK_WIDTH = GLA_HEADS * GLA_DK
GLA_V_WIDTH = GLA_HEADS * GLA_DV
DIL_WIDTH = DIL_HEADS * DIL_HEAD_DIM
MEM_WIDTH = MEM_HEADS * MEM_HEAD_DIM
MIX_WIDTH = GLA_V_WIDTH + DIL_WIDTH + MEM_WIDTH

LANES = 128

COL_GQ = 0
COL_GK = COL_GQ + GLA_QK_WIDTH // LANES
COL_GV = COL_GK + GLA_QK_WIDTH // LANES
COL_GR = COL_GV + GLA_V_WIDTH // LANES
COL_DQ = COL_GR + GLA_V_WIDTH // LANES
COL_DK = COL_DQ + DIL_WIDTH // LANES
COL_DV = COL_DK + DIL_WIDTH // LANES
COL_MQ = COL_DV + DIL_WIDTH // LANES
COL_LR = COL_MQ + MEM_WIDTH // LANES
PROJ_COLS = COL_LR + 2
PROJ_WIDTH = PROJ_COLS * LANES

DIL_BAND = 64
DIL_TQ = 128
DIL_TK = DIL_TQ + 2 * DIL_BAND

VMEM_LIMIT = 48 * 1024 * 1024


def _params(n_axes, vmem=VMEM_LIMIT):
    return pltpu.CompilerParams(dimension_semantics=("arbitrary",) * n_axes, vmem_limit_bytes=vmem)


def _dot(a, b):
    return jnp.dot(a, b, preferred_element_type=F32)


def _dot_nt(a, b):
    return lax.dot_general(a, b, (((1,), (1,)), ((), ())), preferred_element_type=F32)


def _dot_tn(a, b):
    return lax.dot_general(a, b, (((0,), (0,)), ((), ())), preferred_element_type=F32)


def _rms(x, gain):
    return x * lax.rsqrt(jnp.mean(x * x, axis=-1, keepdims=True) + EPS) * gain


def _in_proj_kernel(x_ref, g_ref, w_ref, o_ref, h_ref):
    @pl.when(pl.program_id(1) == 0)
    def _():
        h_ref[...] = _rms(x_ref[...], g_ref[...]).astype(h_ref.dtype)

    o_ref[...] = _dot(h_ref[...], w_ref[...]).astype(o_ref.dtype)


def _in_proj(x, gain, w, layer, tm=1024, tn=768):
    s, d = x.shape
    n = w.shape[-1]
    return pl.pallas_call(
        _in_proj_kernel,
        grid=(s // tm, n // tn),
        in_specs=[
            pl.BlockSpec((tm, d), lambda i, j: (i, 0)),
            pl.BlockSpec((None, 1, d), lambda i, j: (layer, 0, 0)),
            pl.BlockSpec((None, d, tn), lambda i, j: (layer, 0, j)),
        ],
        out_specs=pl.BlockSpec((tm, tn), lambda i, j: (i, j)),
        out_shape=jax.ShapeDtypeStruct((s, n), BF16),
        scratch_shapes=[pltpu.VMEM((tm, d), BF16)],
        compiler_params=_params(2),
        name="in_proj",
    )(x, gain, w)


def _log_sigmoid(x):
    return jnp.minimum(x, 0.0) - jnp.log1p(jnp.exp(-jnp.abs(x)))


def _gla_direction(q_ref, k_ref, v_ref, lr_ref, up_ref, bias_ref, o_ref, s_ref, reverse):
    t = q_ref.shape[0]
    c = GLA_CHUNK
    logits = _dot(lr_ref[...], up_ref[...]) + bias_ref[...]
    g = _log_sigmoid(logits) * (1.0 / GLA_GATE_NORMALIZER)

    row = lax.broadcasted_iota(jnp.int32, (t, t), 0)
    col = lax.broadcasted_iota(jnp.int32, (t, t), 1)
    chunk_shift = c.bit_length() - 1
    same_chunk = jnp.right_shift(row, chunk_shift) == jnp.right_shift(col, chunk_shift)
    ordered = (col >= row) if reverse else (col <= row)
    tri = jnp.where(same_chunk, jnp.where(ordered, 1.0, 0.0), 0.0).astype(BF16)
    g_hi = g.astype(BF16)
    rem = g - g_hi.astype(F32)
    g_mid = rem.astype(BF16)
    g_lo = (rem - g_mid.astype(F32)).astype(BF16)
    b = _dot(tri, g_hi) + _dot(tri, g_mid) + _dot(tri, g_lo)

    crow = lax.broadcasted_iota(jnp.int32, (c, c), 0)
    ccol = lax.broadcasted_iota(jnp.int32, (c, c), 1)
    causal = (ccol >= crow) if reverse else (ccol <= crow)
    scale = GLA_DK ** -0.5

    chunk_ids = range(t // c)
    for ci in (reversed(chunk_ids) if reverse else chunk_ids):
        sl = slice(ci * c, (ci + 1) * c)
        bc = b[sl]
        b_last = bc[0:1] if reverse else bc[c - 1:c]
        qc = q_ref[sl, :].astype(F32) * scale
        kc = k_ref[sl, :].astype(F32)
        vc = v_ref[sl, :]
        q_dec = (qc * jnp.exp(bc)).astype(BF16)
        k_inv = (kc * jnp.exp(-bc)).astype(BF16)
        k_end = (kc * jnp.exp(b_last - bc)).astype(BF16)
        a = jnp.where(causal, _dot_nt(q_dec, k_inv), 0.0).astype(BF16)
        state = s_ref[...]
        o = _dot(a, vc) + _dot_nt(q_dec, state.astype(BF16))
        o_ref[sl, :] = o.astype(o_ref.dtype)
        s_ref[...] = state * jnp.exp(b_last) + _dot_tn(vc, k_end)


def _gla_kernel(qf, kf, vf, lrf, qb, kb, vb, lrb, upf, upb, bsf, bsb, of, ob, sf, sb):
    @pl.when(pl.program_id(1) == 0)
    def _():
        sf[...] = jnp.zeros_like(sf)
        sb[...] = jnp.zeros_like(sb)

    _gla_direction(qf, kf, vf, lrf, upf, bsf, of, sf, reverse=False)
    _gla_direction(qb, kb, vb, lrb, upb, bsb, ob, sb, reverse=True)


def _gla(proj, up_f, up_b, bias_f, bias_b, layer, t=256):
    s = proj.shape[0]
    nb = s // t
    vcol = COL_GV * LANES // GLA_DV

    def row_specs(rmap):
        return [
            pl.BlockSpec((t, GLA_DK), lambda h, i: (rmap(i), COL_GQ + h)),
            pl.BlockSpec((t, GLA_DK), lambda h, i: (rmap(i), COL_GK + h)),
            pl.BlockSpec((t, GLA_DV), lambda h, i: (rmap(i), vcol + h)),
            pl.BlockSpec((t, LANES), lambda h, i: (rmap(i), COL_LR)),
        ]

    fwd = lambda i: i
    bwd = lambda i: nb - 1 - i
    up_spec = pl.BlockSpec((None, LANES, GLA_DK), lambda h, i: (layer, 0, h))
    bias_spec = pl.BlockSpec((None, 1, GLA_DK), lambda h, i: (layer, 0, h))
    out_sds = jax.ShapeDtypeStruct((s, GLA_V_WIDTH), BF16)
    return pl.pallas_call(
        _gla_kernel,
        grid=(GLA_HEADS, nb),
        in_specs=row_specs(fwd) + row_specs(bwd) + [up_spec, up_spec, bias_spec, bias_spec],
        out_specs=[
            pl.BlockSpec((t, GLA_DV), lambda h, i: (fwd(i), h)),
            pl.BlockSpec((t, GLA_DV), lambda h, i: (bwd(i), h)),
        ],
        out_shape=[out_sds, out_sds],
        scratch_shapes=[pltpu.VMEM((GLA_DV, GLA_DK), F32), pltpu.VMEM((GLA_DV, GLA_DK), F32)],
        compiler_params=_params(2),
        name="gla_scan",
    )(proj, proj, proj, proj, proj, proj, proj, proj, up_f, up_b, bias_f, bias_b)


def _dil_kernel(q_ref, kp_ref, km_ref, kn_ref, vp_ref, vm_ref, vn_ref, bias_ref,
                o_ref, lse_ref, kbuf, vbuf, *, sub_len):
    tq = q_ref.shape[0]
    i = pl.program_id(1)
    kbuf[0:DIL_BAND, :] = kp_ref[...]
    kbuf[DIL_BAND:DIL_BAND + tq, :] = km_ref[...]
    kbuf[DIL_BAND + tq:, :] = kn_ref[...]
    vbuf[0:DIL_BAND, :] = vp_ref[...]
    vbuf[DIL_BAND:DIL_BAND + tq, :] = vm_ref[...]
    vbuf[DIL_BAND + tq:, :] = vn_ref[...]
    scale = DIL_HEAD_DIM ** -0.5
    lane = lax.broadcasted_iota(jnp.int32, (1, DIL_TK), 1)
    for ti in range(tq // DIL_TQ):
        r0 = ti * DIL_TQ
        q = q_ref[r0:r0 + DIL_TQ, :]
        k = kbuf[r0:r0 + DIL_TK, :]
        v = vbuf[r0:r0 + DIL_TK, :]
        key_pos = i * tq + (r0 - DIL_BAND) + lane
        edge = jnp.where((key_pos >= 0) & (key_pos < sub_len), 0.0, NEG_INF)
        sc = _dot_nt(q, k) * scale + bias_ref[...] + edge
        m = jnp.max(sc, axis=-1, keepdims=True)
        p = jnp.exp(sc - m)
        den = jnp.sum(p, axis=-1, keepdims=True)
        o = _dot(p.astype(BF16), v) / den
        o_ref[r0:r0 + DIL_TQ, :] = o.astype(o_ref.dtype)
        lse_ref[r0:r0 + DIL_TQ, :] = jnp.broadcast_to(m + jnp.log(den), (DIL_TQ, DIL_HEAD_DIM))


def _dil_branch(proj, bias, dilation, tq=512):
    s = proj.shape[0]
    sub_len = s // dilation
    tq = min(tq, sub_len)
    view = proj.reshape(sub_len, dilation * PROJ_WIDTH)
    hb = tq // DIL_BAND
    n_halo = sub_len // DIL_BAND

    def col(base):
        return lambda rh, i: (rh // DIL_HEADS) * PROJ_COLS + base + rh % DIL_HEADS

    def main(base):
        c = col(base)
        return pl.BlockSpec((tq, DIL_HEAD_DIM), lambda rh, i: (i, c(rh, i)))

    def prev(base):
        c = col(base)
        return pl.BlockSpec((DIL_BAND, DIL_HEAD_DIM), lambda rh, i: (jnp.maximum(i * hb - 1, 0), c(rh, i)))

    def nxt(base):
        c = col(base)
        return pl.BlockSpec((DIL_BAND, DIL_HEAD_DIM),
                            lambda rh, i: (jnp.minimum((i + 1) * hb, n_halo - 1), c(rh, i)))

    out_spec = pl.BlockSpec((tq, DIL_HEAD_DIM), lambda rh, i: (i, rh))
    o, lse = pl.pallas_call(
        functools.partial(_dil_kernel, sub_len=sub_len),
        grid=(dilation * DIL_HEADS, sub_len // tq),
        in_specs=[main(COL_DQ), prev(COL_DK), main(COL_DK), nxt(COL_DK),
                  prev(COL_DV), main(COL_DV), nxt(COL_DV),
                  pl.BlockSpec((None, DIL_TQ, DIL_TK), lambda rh, i: (rh % DIL_HEADS, 0, 0))],
        out_specs=[out_spec, out_spec],
        out_shape=[jax.ShapeDtypeStruct((sub_len, dilation * DIL_WIDTH), BF16),
                   jax.ShapeDtypeStruct((sub_len, dilation * DIL_WIDTH), F32)],
        scratch_shapes=[pltpu.VMEM((tq + 2 * DIL_BAND, DIL_HEAD_DIM), BF16),
                        pltpu.VMEM((tq + 2 * DIL_BAND, DIL_HEAD_DIM), BF16)],
        compiler_params=_params(2),
        name=f"dil_attn_d{dilation}",
    )(view, view, view, view, view, view, view, bias)
    return o.reshape(s, DIL_WIDTH), lse.reshape(s, DIL_WIDTH)


def _t5_bucket(rel):
    half = REL_BUCKETS // 2
    max_exact = half // 2
    ret = jnp.where(rel > 0, half, 0)
    n = jnp.abs(rel)
    nf = jnp.maximum(n, 1).astype(F32)
    large = max_exact + (jnp.log(nf / max_exact) / math.log(REL_MAX_DISTANCE / max_exact)
                         * (half - max_exact)).astype(jnp.int32)
    large = jnp.minimum(large, half - 1)
    return ret + jnp.where(n < max_exact, n, large)


def _dil_bias_tiles(rel_bias):
    rel = jnp.arange(DIL_TK)[None, :] - DIL_BAND - jnp.arange(DIL_TQ)[:, None]
    band = jnp.abs(rel) <= DIL_BAND
    tiles = []
    for _, dilation in DIL_CONFIGS:
        bias = jnp.transpose(rel_bias[_t5_bucket(rel * dilation)], (2, 0, 1)).astype(F32)
        tiles.append(jnp.where(band[None], bias, NEG_INF))
    return tiles


def _mem_kv_kernel(mem_ref, g_ref, w_ref, o_ref):
    h = _rms(mem_ref[...], g_ref[...]).astype(BF16)
    o_ref[...] = _dot(h, w_ref[...]).astype(o_ref.dtype)


def _mem_kv(mem, gain, w):
    m, d = mem.shape
    n = w.shape[-1]
    return pl.pallas_call(
        _mem_kv_kernel,
        grid=(DEPTH,),
        in_specs=[
            pl.BlockSpec((m, d), lambda l: (0, 0)),
            pl.BlockSpec((None, 1, d), lambda l: (l, 0, 0)),
            pl.BlockSpec((None, d, n), lambda l: (l, 0, 0)),
        ],
        out_specs=pl.BlockSpec((None, m, n), lambda l: (l, 0, 0)),
        out_shape=jax.ShapeDtypeStruct((DEPTH, m, n), BF16),
        compiler_params=_params(1),
        name="mem_kv",
    )(mem, gain, w)


def _head_rms(o, gain):
    return o * lax.rsqrt(jnp.mean(o * o, axis=-1, keepdims=True) + EPS) * gain


def _mix_kernel(of_ref, ob_ref, gr_ref, o1_ref, o2_ref, o3_ref, l1_ref, l2_ref, l3_ref,
                mq_ref, km_ref, vm_ref, ggla_ref, gdil_ref, gmem_ref, w_ref, x_ref,
                out_ref, mixed_ref):
    for h in range(GLA_HEADS):
        sl = slice(h * GLA_DV, (h + 1) * GLA_DV)
        o = of_ref[:, sl].astype(F32) + ob_ref[:, sl].astype(F32)
        r = gr_ref[:, sl].astype(F32)
        gate = r * (1.0 / (1.0 + jnp.exp(-r)))
        mixed_ref[:, sl] = (_head_rms(o, ggla_ref[:, sl]) * gate).astype(mixed_ref.dtype)

    for h in range(DIL_HEADS):
        sl = slice(h * DIL_HEAD_DIM, (h + 1) * DIL_HEAD_DIM)
        a1, a2, a3 = l1_ref[:, sl], l2_ref[:, sl], l3_ref[:, sl]
        m = jnp.maximum(jnp.maximum(a1, a2), a3)
        e1, e2, e3 = jnp.exp(a1 - m), jnp.exp(a2 - m), jnp.exp(a3 - m)
        o = (e1 * o1_ref[:, sl].astype(F32) + e2 * o2_ref[:, sl].astype(F32)
             + e3 * o3_ref[:, sl].astype(F32)) / (e1 + e2 + e3)
        c0 = GLA_V_WIDTH + h * DIL_HEAD_DIM
        mixed_ref[:, c0:c0 + DIL_HEAD_DIM] = _head_rms(o, gdil_ref[:, sl]).astype(mixed_ref.dtype)

    scale = MEM_HEAD_DIM ** -0.5
    for h in range(MEM_HEADS):
        sl = slice(h * MEM_HEAD_DIM, (h + 1) * MEM_HEAD_DIM)
        sc = _dot_nt(mq_ref[:, sl], km_ref[:, sl]) * scale
        m = jnp.max(sc, axis=-1, keepdims=True)
        p = jnp.exp(sc - m)
        den = jnp.sum(p, axis=-1, keepdims=True)
        o = _dot(p.astype(BF16), vm_ref[:, sl]) / den
        c0 = GLA_V_WIDTH + DIL_WIDTH + h * MEM_HEAD_DIM
        mixed_ref[:, c0:c0 + MEM_HEAD_DIM] = _head_rms(o, gmem_ref[:, sl]).astype(mixed_ref.dtype)

    out_ref[...] = x_ref[...] + _dot(mixed_ref[...], w_ref[...])


def _mix_out(x, proj, o_f, o_b, dil, kv, g_gla, g_dil, g_mem, w_out, layer, t=256):
    s, d = x.shape
    (o1, l1), (o2, l2), (o3, l3) = dil
    rows = lambda w, c=0: pl.BlockSpec((t, w), lambda i: (i, c))
    per_layer = lambda shape, c=0: pl.BlockSpec((None,) + shape, lambda i: (layer, 0, c))
    return pl.pallas_call(
        _mix_kernel,
        grid=(s // t,),
        in_specs=[
            rows(GLA_V_WIDTH), rows(GLA_V_WIDTH),
            rows(GLA_V_WIDTH, COL_GR * LANES // GLA_V_WIDTH),
            rows(DIL_WIDTH), rows(DIL_WIDTH), rows(DIL_WIDTH),
            rows(DIL_WIDTH), rows(DIL_WIDTH), rows(DIL_WIDTH),
            rows(MEM_WIDTH, COL_MQ * LANES // MEM_WIDTH),
            per_layer((MEM_LEN, MEM_WIDTH), 0), per_layer((MEM_LEN, MEM_WIDTH), 1),
            per_layer((1, GLA_V_WIDTH)), per_layer((1, DIL_WIDTH)), per_layer((1, MEM_WIDTH)),
            per_layer((MIX_WIDTH, d)),
            rows(d),
        ],
        out_specs=rows(d),
        out_shape=jax.ShapeDtypeStruct((s, d), F32),
        scratch_shapes=[pltpu.VMEM((t, MIX_WIDTH), BF16)],
        compiler_params=_params(1),
        name="mix_out_proj",
    )(o_f, o_b, proj, o1, o2, o3, l1, l2, l3, proj, kv, kv, g_gla, g_dil, g_mem, w_out, x)


def _mlp_kernel(x_ref, g_ref, wu_ref, wd_ref, gf_ref, o_ref, h_ref, *, final_norm):
    f = pl.program_id(1)

    @pl.when(f == 0)
    def _():
        x = x_ref[...]
        h_ref[...] = _rms(x, g_ref[...]).astype(h_ref.dtype)
        o_ref[...] = x

    u = jnp.square(jnp.maximum(_dot(h_ref[...], wu_ref[...]), 0.0)).astype(BF16)
    o_ref[...] += _dot(u, wd_ref[...])

    if final_norm:
        @pl.when(f == pl.num_programs(1) - 1)
        def _():
            o_ref[...] = _rms(o_ref[...], gf_ref[...])


def _mlp(x, gain, w_up, w_down, gain_final, layer, final_norm, tm=512, tf=1024):
    s, d = x.shape
    ff = w_up.shape[-1]
    return pl.pallas_call(
        functools.partial(_mlp_kernel, final_norm=final_norm),
        grid=(s // tm, ff // tf),
        in_specs=[
            pl.BlockSpec((tm, d), lambda i, f: (i, 0)),
            pl.BlockSpec((None, 1, d), lambda i, f: (layer, 0, 0)),
            pl.BlockSpec((None, d, tf), lambda i, f: (layer, 0, f)),
            pl.BlockSpec((None, tf, d), lambda i, f: (layer, f, 0)),
            pl.BlockSpec((1, d), lambda i, f: (0, 0)),
        ],
        out_specs=pl.BlockSpec((tm, d), lambda i, f: (i, 0)),
        out_shape=jax.ShapeDtypeStruct((s, d), F32),
        scratch_shapes=[pltpu.VMEM((tm, d), BF16)],
        compiler_params=_params(2),
        name="mlp",
    )(x, gain, w_up, w_down, gain_final)


def _reorder_w_in(w_in):
    sizes = (GLA_QK_WIDTH, GLA_QK_WIDTH, GLA_V_WIDTH, GLA_V_WIDTH, GLA_GATE_RANK, GLA_GATE_RANK,
             DIL_WIDTH, DIL_WIDTH, DIL_WIDTH, MEM_WIDTH)
    parts, start = [], 0
    for size in sizes:
        parts.append(w_in[..., start:start + size])
        start += size
    gq, gk, gv, gr, lr_f, lr_b, dq, dk, dv, mq = parts
    used = sum(sizes)
    pad = jnp.zeros(w_in.shape[:-1] + (PROJ_WIDTH - used,), w_in.dtype)
    return jnp.concatenate([gq, gk, gv, gr, dq, dk, dv, mq, lr_f, lr_b, pad], axis=-1).astype(BF16)


def _pad_gate_up(up, row0):
    out = jnp.zeros((up.shape[0], LANES, up.shape[-1]), BF16)
    return out.at[:, row0:row0 + GLA_GATE_RANK, :].set(up.astype(BF16))


def kernel(x, mem, norm_mix, w_in, gla_gate_up_fwd, gla_gate_bias_fwd, gla_gate_up_bwd, gla_gate_bias_bwd, gla_norm, rel_bias, dil_norm, mem_norm, w_mem_kv, mem_out_norm, w_out, norm_mlp, w_up, w_down, norm_final):
    batch, seq, d = x.shape
    assert batch == 1 and seq == SEQ and d == D_MODEL
    xs = x.reshape(seq, d)
    row = lambda g: g.reshape(g.shape[0], 1, g.shape[-1])

    w_in_r = _reorder_w_in(w_in)
    up_f = _pad_gate_up(gla_gate_up_fwd, 0)
    up_b = _pad_gate_up(gla_gate_up_bwd, GLA_GATE_RANK)
    w_out_b = w_out.astype(BF16)
    w_up_b = w_up.astype(BF16)
    w_down_b = w_down.astype(BF16)
    bias_tiles = _dil_bias_tiles(rel_bias)
    kv = _mem_kv(mem.reshape(MEM_LEN, d), row(mem_norm), w_mem_kv.astype(BF16))

    for layer in range(DEPTH):
        proj = _in_proj(xs, row(norm_mix), w_in_r, layer)
        o_f, o_b = _gla(proj, up_f, up_b, row(gla_gate_bias_fwd), row(gla_gate_bias_bwd), layer)
        dil = [_dil_branch(proj, bias_tiles[b], dilation) for b, (_, dilation) in enumerate(DIL_CONFIGS)]
        xs = _mix_out(xs, proj, o_f, o_b, dil, kv, row(gla_norm), row(dil_norm), row(mem_out_norm),
                      w_out_b, layer)
        xs = _mlp(xs, row(norm_mlp), w_up_b, w_down_b, norm_final.reshape(1, d), layer,
                  final_norm=(layer == DEPTH - 1))
    return xs.reshape(batch, seq, d)
```

```python
import functools
import math

import jax
import jax.numpy as jnp
from jax import lax
from jax.experimental import pallas as pl
from jax.experimental.pallas import tpu as pltpu

F32 = jnp.float32
BF16 = jnp.bfloat16

D_MODEL = 2048
SEQ = 8192
DEPTH = 4
MEM_LEN = 256
GLA_HEADS = 4
GLA_DK = 128
GLA_DV = 256
GLA_GATE_RANK = 16
GLA_GATE_NORMALIZER = 16.0
GLA_CHUNK = 64
DIL_HEADS = 4
DIL_HEAD_DIM = 128
DIL_CONFIGS = ((128, 1), (512, 4), (2048, 16))
MEM_HEADS = 4
MEM_HEAD_DIM = 128
REL_BUCKETS = 32
REL_MAX_DISTANCE = 1024
D_FF = 4 * D_MODEL
EPS = 1e-6
NEG_INF = -1e30

GLA_QK_WIDTH = GLA_HEADS * GLA_DK
GLA_V_WIDTH = GLA_HEADS * GLA_DV
DIL_WIDTH = DIL_HEADS * DIL_HEAD_DIM
MEM_WIDTH = MEM_HEADS * MEM_HEAD_DIM
MIX_WIDTH = GLA_V_WIDTH + DIL_WIDTH + MEM_WIDTH

LANES = 128

COL_GQ = 0
COL_GK = COL_GQ + GLA_QK_WIDTH // LANES
COL_GV = COL_GK + GLA_QK_WIDTH // LANES
COL_GR = COL_GV + GLA_V_WIDTH // LANES
COL_DQ = COL_GR + GLA_V_WIDTH // LANES
COL_DK = COL_DQ + DIL_WIDTH // LANES
COL_DV = COL_DK + DIL_WIDTH // LANES
COL_MQ = COL_DV + DIL_WIDTH // LANES
COL_LR = COL_MQ + MEM_WIDTH // LANES
PROJ_COLS = COL_LR + 2
PROJ_WIDTH = PROJ_COLS * LANES

GLA_BLOCK = 256

DIL_BAND = 64
DIL_TQ = 128
DIL_TK = DIL_TQ + 2 * DIL_BAND
DIL_DILATIONS = tuple(d for _, d in DIL_CONFIGS)
DIL_TB = DIL_TQ * max(DIL_DILATIONS)

VMEM_LIMIT = 48 * 1024 * 1024


def _params(n_axes, vmem=VMEM_LIMIT):
    return pltpu.CompilerParams(dimension_semantics=("arbitrary",) * n_axes, vmem_limit_bytes=vmem)


def _dot(a, b):
    return jnp.dot(a, b, preferred_element_type=F32)


def _dot_nt(a, b):
    return lax.dot_general(a, b, (((1,), (1,)), ((), ())), preferred_element_type=F32)


def _dot_tn(a, b):
    return lax.dot_general(a, b, (((0,), (0,)), ((), ())), preferred_element_type=F32)


def _rms(x, gain):
    return x * lax.rsqrt(jnp.mean(x * x, axis=-1, keepdims=True) + EPS) * gain


def _in_proj_kernel(x_ref, g_ref, w_ref, o_ref, h_ref):
    @pl.when(pl.program_id(1) == 0)
    def _():
        h_ref[...] = _rms(x_ref[...], g_ref[...]).astype(h_ref.dtype)

    o_ref[...] = _dot(h_ref[...], w_ref[...]).astype(o_ref.dtype)


def _in_proj(x, gain, w, layer, tm=1024, tn=768):
    s, d = x.shape
    n = w.shape[-1]
    return pl.pallas_call(
        _in_proj_kernel,
        grid=(s // tm, n // tn),
        in_specs=[
            pl.BlockSpec((tm, d), lambda i, j: (i, 0)),
            pl.BlockSpec((None, 1, d), lambda i, j: (layer, 0, 0)),
            pl.BlockSpec((None, d, tn), lambda i, j: (layer, 0, j)),
        ],
        out_specs=pl.BlockSpec((tm, tn), lambda i, j: (i, j)),
        out_shape=jax.ShapeDtypeStruct((s, n), BF16),
        scratch_shapes=[pltpu.VMEM((tm, d), BF16)],
        compiler_params=_params(2),
        name="in_proj",
    )(x, gain, w)


def _log_sigmoid(x):
    return jnp.minimum(x, 0.0) - jnp.log1p(jnp.exp(-jnp.abs(x)))


GLA_LEVELS = (GLA_BLOCK // GLA_CHUNK).bit_length() - 1


def _gla_tables(tri_ref, lvl_ref, reverse):
    t = GLA_BLOCK
    row = lax.broadcasted_iota(jnp.int32, (t, t), 0)
    col = lax.broadcasted_iota(jnp.int32, (t, t), 1)
    hi, lo = (col, row) if reverse else (row, col)
    valid = lo <= hi
    tri_ref[...] = jnp.where(valid, 1.0, 0.0).astype(tri_ref.dtype)
    shift = GLA_CHUNK.bit_length() - 1
    lvl = jnp.where(jnp.right_shift(hi, shift) == jnp.right_shift(lo, shift),
                    jnp.where(valid, 0, -1), -1)
    for level in range(1, GLA_LEVELS + 1):
        half = jnp.right_shift(hi, shift + level - 1)
        pair = jnp.where(half == jnp.right_shift(lo, shift + level - 1) + 1,
                         jnp.bitwise_and(half, 1), 0)
        lvl = jnp.where(pair == 1, level, lvl)
    lvl_ref[...] = lvl


def _gla_boundary(cum, group, reverse, chunk_level):
    t = cum.shape[0]
    pieces = []
    for a in range(0, t, group):
        if chunk_level:
            idx = a + group if reverse else a - 1
        else:
            idx = a + group // 2 if reverse else a + group // 2 - 1
        if idx < 0 or idx >= t:
            ref_row = jnp.zeros((1, cum.shape[1]), cum.dtype)
        else:
            ref_row = cum[idx:idx + 1]
        pieces.append(jnp.broadcast_to(ref_row, (group, cum.shape[1])))
    return jnp.concatenate(pieces, axis=0)


def _gla_block(q_ref, k_ref, v_ref, lr_ref, up_ref, bias_ref, o_ref, s_ref, tri_ref, lvl_ref, r0, reverse):
    t = GLA_BLOCK
    rows = slice(r0, r0 + t)
    logits = _dot(lr_ref[rows, :], up_ref[...]) + bias_ref[...]
    g = _log_sigmoid(logits) * (1.0 / GLA_GATE_NORMALIZER)
    g_hi = g.astype(BF16)
    g_lo = (g - g_hi.astype(F32)).astype(BF16)
    tri = tri_ref[...]
    cum = _dot(tri, g_hi) + _dot(tri, g_lo)

    q = q_ref[rows, :].astype(F32) * (GLA_DK ** -0.5)
    k = k_ref[rows, :].astype(F32)
    v = v_ref[rows, :]
    lvl = lvl_ref[...]

    ref0 = _gla_boundary(cum, GLA_CHUNK, reverse, chunk_level=True)
    q_dec = (q * jnp.exp(cum - ref0)).astype(BF16)
    k_inv = (k * jnp.exp(ref0 - cum)).astype(BF16)
    a = jnp.where(lvl == 0, _dot_nt(q_dec, k_inv), 0.0)
    for level in range(1, GLA_LEVELS + 1):
        mid = _gla_boundary(cum, GLA_CHUNK << level, reverse, chunk_level=False)
        q_l = (q * jnp.exp(jnp.minimum(cum - mid, 0.0))).astype(BF16)
        k_l = (k * jnp.exp(jnp.minimum(mid - cum, 0.0))).astype(BF16)
        a = jnp.where(lvl == level, _dot_nt(q_l, k_l), a)

    total = cum[0:1] if reverse else cum[t - 1:t]
    state = s_ref[...]
    q_in = (q * jnp.exp(cum)).astype(BF16)
    o = _dot(a.astype(BF16), v) + _dot_nt(q_in, state.astype(BF16))
    o_ref[rows, :] = o.astype(o_ref.dtype)
    k_out = (k * jnp.exp(total - cum)).astype(BF16)
    s_ref[...] = state * jnp.exp(total) + _dot_tn(v, k_out)


def _gla_kernel(qf, kf, vf, lrf, qb, kb, vb, lrb, upf, upb, bsf, bsb, of, ob,
                sf, sb, tri_f, tri_b, lvl_f, lvl_b):
    @pl.when(pl.program_id(1) == 0)
    def _():
        sf[...] = jnp.zeros_like(sf)
        sb[...] = jnp.zeros_like(sb)
        _gla_tables(tri_f, lvl_f, reverse=False)
        _gla_tables(tri_b, lvl_b, reverse=True)

    starts = range(0, qf.shape[0], GLA_BLOCK)
    for r_fwd, r_bwd in zip(starts, reversed(starts)):
        _gla_block(qf, kf, vf, lrf, upf, bsf, of, sf, tri_f, lvl_f, r_fwd, reverse=False)
        _gla_block(qb, kb, vb, lrb, upb, bsb, ob, sb, tri_b, lvl_b, r_bwd, reverse=True)


def _gla(proj, up_f, up_b, bias_f, bias_b, layer, t=512):
    s = proj.shape[0]
    nb = s // t
    vcol = COL_GV * LANES // GLA_DV

    def row_specs(rmap):
        return [
            pl.BlockSpec((t, GLA_DK), lambda h, i: (rmap(i), COL_GQ + h)),
            pl.BlockSpec((t, GLA_DK), lambda h, i: (rmap(i), COL_GK + h)),
            pl.BlockSpec((t, GLA_DV), lambda h, i: (rmap(i), vcol + h)),
            pl.BlockSpec((t, LANES), lambda h, i: (rmap(i), COL_LR)),
        ]

    fwd = lambda i: i
    bwd = lambda i: nb - 1 - i
    up_spec = pl.BlockSpec((None, LANES, GLA_DK), lambda h, i: (layer, 0, h))
    bias_spec = pl.BlockSpec((None, 1, GLA_DK), lambda h, i: (layer, 0, h))
    out_sds = jax.ShapeDtypeStruct((s, GLA_V_WIDTH), BF16)
    state = pltpu.VMEM((GLA_DV, GLA_DK), F32)
    tri = pltpu.VMEM((GLA_BLOCK, GLA_BLOCK), BF16)
    lvl = pltpu.VMEM((GLA_BLOCK, GLA_BLOCK), jnp.int32)
    return pl.pallas_call(
        _gla_kernel,
        grid=(GLA_HEADS, nb),
        in_specs=row_specs(fwd) + row_specs(bwd) + [up_spec, up_spec, bias_spec, bias_spec],
        out_specs=[
            pl.BlockSpec((t, GLA_DV), lambda h, i: (fwd(i), h)),
            pl.BlockSpec((t, GLA_DV), lambda h, i: (bwd(i), h)),
        ],
        out_shape=[out_sds, out_sds],
        scratch_shapes=[state, state, tri, tri, lvl, lvl],
        compiler_params=_params(2),
        name="gla_scan",
    )(proj, proj, proj, proj, proj, proj, proj, proj, up_f, up_b, bias_f, bias_b)


def _dil_tile(q, k, v, bias, edge):
    sc = _dot_nt(q, k) * (DIL_HEAD_DIM ** -0.5) + bias
    if edge is not None:
        sc = sc + edge
    m = jnp.max(sc, axis=-1, keepdims=True)
    p = jnp.exp(sc - m)
    den = jnp.sum(p, axis=-1, keepdims=True)
    o = _dot(p.astype(BF16), v) / den
    return o, jnp.broadcast_to(m + jnp.log(den), o.shape)


def _dil_kernel(q_ref, k_ref, v_ref, bias_ref, gain_ref, out_ref,
                stage, kg1, vg1, kg4, vg4, kg16, vg16, qp4, qp16, o1, o4, o16, l1, l4, l16):
    i = pl.program_id(1)
    s = k_ref.shape[0]
    tb = q_ref.shape[0]
    k_bufs = dict(zip(DIL_DILATIONS, (kg1, kg4, kg16)))
    v_bufs = dict(zip(DIL_DILATIONS, (vg1, vg4, vg16)))
    q_perm = dict(zip(DIL_DILATIONS, (None, qp4, qp16)))
    o_bufs = dict(zip(DIL_DILATIONS, (o1, o4, o16)))
    l_bufs = dict(zip(DIL_DILATIONS, (l1, l4, l16)))

    def scatter_residues(dst, d, residue_stride, offset):
        per = tb // d
        for r in range(d):
            start = offset + r * residue_stride
            if not isinstance(start, int):
                start = pl.multiple_of(start, DIL_BAND)
            dst[pl.ds(start, per), :] = stage[pl.ds(r, per, stride=d), :].astype(dst.dtype)

    @pl.when(i == 0)
    def _():
        zeros = jnp.zeros((DIL_BAND, DIL_HEAD_DIM), BF16)
        for buf in (kg1, vg1, kg4, vg4, kg16, vg16):
            buf[0:DIL_BAND, :] = zeros
            buf[DIL_BAND + s:, :] = zeros
        kg1[DIL_BAND:DIL_BAND + s, :] = k_ref[...]
        vg1[DIL_BAND:DIL_BAND + s, :] = v_ref[...]

        def body(c, carry):
            base = pl.multiple_of(c * tb, tb)
            for src, bufs in ((k_ref, k_bufs), (v_ref, v_bufs)):
                stage[...] = src[pl.ds(base, tb), :].astype(F32)
                for d in DIL_DILATIONS[1:]:
                    scatter_residues(bufs[d], d, s // d, DIL_BAND + c * (tb // d))
            return carry

        lax.fori_loop(0, s // tb, body, 0)

    stage[...] = q_ref[...].astype(F32)
    for d in DIL_DILATIONS[1:]:
        scatter_residues(q_perm[d], d, tb // d, 0)

    lane = lax.broadcasted_iota(jnp.int32, (1, DIL_TK), 1)
    for b, d in enumerate(DIL_DILATIONS):
        sub_len = s // d
        per = tb // d
        n_tiles = per // DIL_TQ
        bias = bias_ref[b]
        for r in range(d):
            for tj in range(n_tiles):
                u0 = i * per + tj * DIL_TQ
                q_src = q_ref if d == 1 else q_perm[d]
                q = q_src[r * per + tj * DIL_TQ:r * per + (tj + 1) * DIL_TQ, :]
                k_start = pl.multiple_of(r * sub_len + u0, DIL_TQ)
                k = k_bufs[d][pl.ds(k_start, DIL_TK), :]
                v = v_bufs[d][pl.ds(k_start, DIL_TK), :]
                edge = None
                if tj == 0 or tj == n_tiles - 1:
                    key_pos = u0 - DIL_BAND + lane
                    edge = jnp.where((key_pos >= 0) & (key_pos < sub_len), 0.0, NEG_INF)
                o, lse = _dil_tile(q, k, v, bias, edge)
                if d == 1:
                    rows = pl.ds(tj * DIL_TQ, DIL_TQ)
                else:
                    rows = pl.ds(r + d * tj * DIL_TQ, DIL_TQ, stride=d)
                o_bufs[d][rows, :] = o
                l_bufs[d][rows, :] = lse

    lses = [l_bufs[d][...] for d in DIL_DILATIONS]
    m = functools.reduce(jnp.maximum, lses)
    es = [jnp.exp(l - m) for l in lses]
    num = sum(e * o_bufs[d][...] for e, d in zip(es, DIL_DILATIONS))
    mixed = num / sum(es)
    out_ref[...] = _head_rms(mixed, gain_ref[...]).astype(out_ref.dtype)


def _dil_attention(proj, bias, gain, layer):
    s = proj.shape[0]
    tb = DIL_TB
    guarded = pltpu.VMEM((s + 2 * DIL_BAND, DIL_HEAD_DIM), BF16)
    tile_f32 = pltpu.VMEM((tb, DIL_HEAD_DIM), F32)
    tile_bf16 = pltpu.VMEM((tb, DIL_HEAD_DIM), BF16)
    return pl.pallas_call(
        _dil_kernel,
        grid=(DIL_HEADS, s // tb),
        in_specs=[
            pl.BlockSpec((tb, DIL_HEAD_DIM), lambda h, i: (i, COL_DQ + h)),
            pl.BlockSpec((s, DIL_HEAD_DIM), lambda h, i: (0, COL_DK + h)),
            pl.BlockSpec((s, DIL_HEAD_DIM), lambda h, i: (0, COL_DV + h)),
            pl.BlockSpec((len(DIL_DILATIONS), None, DIL_TQ, DIL_TK), lambda h, i: (0, h, 0, 0)),
            pl.BlockSpec((None, 1, DIL_HEAD_DIM), lambda h, i: (layer, 0, h)),
        ],
        out_specs=pl.BlockSpec((tb, DIL_HEAD_DIM), lambda h, i: (i, h)),
        out_shape=jax.ShapeDtypeStruct((s, DIL_WIDTH), BF16),
        scratch_shapes=[tile_f32] + [guarded] * 6 + [tile_bf16] * 2 + [tile_f32] * 6,
        compiler_params=_params(2),
        name="dil_attn",
    )(proj, proj, proj, bias, gain)


def _t5_bucket(rel):
    half = REL_BUCKETS // 2
    max_exact = half // 2
    ret = jnp.where(rel > 0, half, 0)
    n = jnp.abs(rel)
    nf = jnp.maximum(n, 1).astype(F32)
    large = max_exact + (jnp.log(nf / max_exact) / math.log(REL_MAX_DISTANCE / max_exact)
                         * (half - max_exact)).astype(jnp.int32)
    large = jnp.minimum(large, half - 1)
    return ret + jnp.where(n < max_exact, n, large)


def _dil_bias_tiles(rel_bias):
    rel = jnp.arange(DIL_TK)[None, :] - DIL_BAND - jnp.arange(DIL_TQ)[:, None]
    band = jnp.abs(rel) <= DIL_BAND
    tiles = []
    for d in DIL_DILATIONS:
        onehot = (_t5_bucket(rel * d)[..., None] == jnp.arange(REL_BUCKETS)).astype(F32)
        bias = jnp.einsum("qkb,bh->hqk", onehot, rel_bias.astype(F32), precision=lax.Precision.HIGHEST)
        tiles.append(jnp.where(band[None], bias, NEG_INF))
    return jnp.stack(tiles, axis=0)


def _mem_kv_kernel(mem_ref, g_ref, w_ref, o_ref):
    h = _rms(mem_ref[...], g_ref[...]).astype(BF16)
    o_ref[...] = _dot(h, w_ref[...]).astype(o_ref.dtype)


def _mem_kv(mem, gain, w):
    m, d = mem.shape
    n = w.shape[-1]
    return pl.pallas_call(
        _mem_kv_kernel,
        grid=(DEPTH,),
        in_specs=[
            pl.BlockSpec((m, d), lambda l: (0, 0)),
            pl.BlockSpec((None, 1, d), lambda l: (l, 0, 0)),
            pl.BlockSpec((None, d, n), lambda l: (l, 0, 0)),
        ],
        out_specs=pl.BlockSpec((None, m, n), lambda l: (l, 0, 0)),
        out_shape=jax.ShapeDtypeStruct((DEPTH, m, n), BF16),
        compiler_params=_params(1),
        name="mem_kv",
    )(mem, gain, w)


def _head_rms(o, gain):
    return o * lax.rsqrt(jnp.mean(o * o, axis=-1, keepdims=True) + EPS) * gain


def _mix_kernel(of_ref, ob_ref, gr_ref, dil_ref, mq_ref, km_ref, vm_ref, ggla_ref, gmem_ref, w_ref, x_ref,
                out_ref, mixed_ref):
    for h in range(GLA_HEADS):
        sl = slice(h * GLA_DV, (h + 1) * GLA_DV)
        o = of_ref[:, sl].astype(F32) + ob_ref[:, sl].astype(F32)
        r = gr_ref[:, sl].astype(F32)
        gate = r * (1.0 / (1.0 + jnp.exp(-r)))
        mixed_ref[:, sl] = (_head_rms(o, ggla_ref[:, sl]) * gate).astype(mixed_ref.dtype)

    mixed_ref[:, GLA_V_WIDTH:GLA_V_WIDTH + DIL_WIDTH] = dil_ref[...]

    scale = MEM_HEAD_DIM ** -0.5
    for h in range(MEM_HEADS):
        sl = slice(h * MEM_HEAD_DIM, (h + 1) * MEM_HEAD_DIM)
        sc = _dot_nt(mq_ref[:, sl], km_ref[:, sl]) * scale
        m = jnp.max(sc, axis=-1, keepdims=True)
        p = jnp.exp(sc - m)
        den = jnp.sum(p, axis=-1, keepdims=True)
        o = _dot(p.astype(BF16), vm_ref[:, sl]) / den
        c0 = GLA_V_WIDTH + DIL_WIDTH + h * MEM_HEAD_DIM
        mixed_ref[:, c0:c0 + MEM_HEAD_DIM] = _head_rms(o, gmem_ref[:, sl]).astype(mixed_ref.dtype)

    out_ref[...] = x_ref[...] + _dot(mixed_ref[...], w_ref[...])


def _mix_out(x, proj, o_f, o_b, dil, kv, g_gla, g_mem, w_out, layer, t=256):
    s, d = x.shape
    rows = lambda w, c=0: pl.BlockSpec((t, w), lambda i: (i, c))
    per_layer = lambda shape, c=0: pl.BlockSpec((None,) + shape, lambda i: (layer, 0, c))
    return pl.pallas_call(
        _mix_kernel,
        grid=(s // t,),
        in_specs=[
            rows(GLA_V_WIDTH), rows(GLA_V_WIDTH),
            rows(GLA_V_WIDTH, COL_GR * LANES // GLA_V_WIDTH),
            rows(DIL_WIDTH),
            rows(MEM_WIDTH, COL_MQ * LANES // MEM_WIDTH),
            per_layer((MEM_LEN, MEM_WIDTH), 0), per_layer((MEM_LEN, MEM_WIDTH), 1),
            per_layer((1, GLA_V_WIDTH)), per_layer((1, MEM_WIDTH)),
            per_layer((MIX_WIDTH, d)),
            rows(d),
        ],
        out_specs=rows(d),
        out_shape=jax.ShapeDtypeStruct((s, d), F32),
        scratch_shapes=[pltpu.VMEM((t, MIX_WIDTH), BF16)],
        compiler_params=_params(1),
        name="mix_out_proj",
    )(o_f, o_b, proj, dil, proj, kv, kv, g_gla, g_mem, w_out, x)


def _mlp_kernel(x_ref, g_ref, wu_ref, wd_ref, gf_ref, o_ref, h_ref, *, final_norm):
    f = pl.program_id(1)

    @pl.when(f == 0)
    def _():
        x = x_ref[...]
        h_ref[...] = _rms(x, g_ref[...]).astype(h_ref.dtype)
        o_ref[...] = x

    u = jnp.square(jnp.maximum(_dot(h_ref[...], wu_ref[...]), 0.0)).astype(BF16)
    o_ref[...] += _dot(u, wd_ref[...])

    if final_norm:
        @pl.when(f == pl.num_programs(1) - 1)
        def _():
            o_ref[...] = _rms(o_ref[...], gf_ref[...])


def _mlp(x, gain, w_up, w_down, gain_final, layer, final_norm, tm=512, tf=1024):
    s, d = x.shape
    ff = w_up.shape[-1]
    return pl.pallas_call(
        functools.partial(_mlp_kernel, final_norm=final_norm),
        grid=(s // tm, ff // tf),
        in_specs=[
            pl.BlockSpec((tm, d), lambda i, f: (i, 0)),
            pl.BlockSpec((None, 1, d), lambda i, f: (layer, 0, 0)),
            pl.BlockSpec((None, d, tf), lambda i, f: (layer, 0, f)),
            pl.BlockSpec((None, tf, d), lambda i, f: (layer, f, 0)),
            pl.BlockSpec((1, d), lambda i, f: (0, 0)),
        ],
        out_specs=pl.BlockSpec((tm, d), lambda i, f: (i, 0)),
        out_shape=jax.ShapeDtypeStruct((s, d), F32),
        scratch_shapes=[pltpu.VMEM((tm, d), BF16)],
        compiler_params=_params(2),
        name="mlp",
    )(x, gain, w_up, w_down, gain_final)


def _reorder_w_in(w_in):
    sizes = (GLA_QK_WIDTH, GLA_QK_WIDTH, GLA_V_WIDTH, GLA_V_WIDTH, GLA_GATE_RANK, GLA_GATE_RANK,
             DIL_WIDTH, DIL_WIDTH, DIL_WIDTH, MEM_WIDTH)
    parts, start = [], 0
    for size in sizes:
        parts.append(w_in[..., start:start + size])
        start += size
    gq, gk, gv, gr, lr_f, lr_b, dq, dk, dv, mq = parts
    used = sum(sizes)
    pad = jnp.zeros(w_in.shape[:-1] + (PROJ_WIDTH - used,), w_in.dtype)
    return jnp.concatenate([gq, gk, gv, gr, dq, dk, dv, mq, lr_f, lr_b, pad], axis=-1).astype(BF16)


def _pad_gate_up(up, row0):
    out = jnp.zeros((up.shape[0], LANES, up.shape[-1]), BF16)
    return out.at[:, row0:row0 + GLA_GATE_RANK, :].set(up.astype(BF16))


def kernel(x, mem, norm_mix, w_in, gla_gate_up_fwd, gla_gate_bias_fwd, gla_gate_up_bwd, gla_gate_bias_bwd, gla_norm, rel_bias, dil_norm, mem_norm, w_mem_kv, mem_out_norm, w_out, norm_mlp, w_up, w_down, norm_final):
    batch, seq, d = x.shape
    assert batch == 1 and seq == SEQ and d == D_MODEL
    xs = x.reshape(seq, d)
    row = lambda g: g.reshape(g.shape[0], 1, g.shape[-1])

    w_in_r = _reorder_w_in(w_in)
    up_f = _pad_gate_up(gla_gate_up_fwd, 0)
    up_b = _pad_gate_up(gla_gate_up_bwd, GLA_GATE_RANK)
    w_out_b = w_out.astype(BF16)
    w_up_b = w_up.astype(BF16)
    w_down_b = w_down.astype(BF16)
    dil_bias = _dil_bias_tiles(rel_bias)
    kv = _mem_kv(mem.reshape(MEM_LEN, d), row(mem_norm), w_mem_kv.astype(BF16))

    for layer in range(DEPTH):
        proj = _in_proj(xs, row(norm_mix), w_in_r, layer)
        o_f, o_b = _gla(proj, up_f, up_b, row(gla_gate_bias_fwd), row(gla_gate_bias_bwd), layer)
        dil = _dil_attention(proj, dil_bias, row(dil_norm), layer)
        xs = _mix_out(xs, proj, o_f, o_b, dil, kv, row(gla_norm), row(mem_out_norm), w_out_b, layer)
        xs = _mlp(xs, row(norm_mlp), w_up_b, w_down_b, norm_final.reshape(1, d), layer,
                  final_norm=(layer == DEPTH - 1))
    return xs.reshape(batch, seq, d)
```

```python
import functools
import math

import jax
import jax.numpy as jnp
from jax import lax
from jax.experimental import pallas as pl
from jax.experimental.pallas import tpu as pltpu

F32 = jnp.float32
BF16 = jnp.bfloat16

D_MODEL = 2048
SEQ = 8192
DEPTH = 4
MEM_LEN = 256
GLA_HEADS = 4
GLA_DK = 128
GLA_DV = 256
GLA_GATE_RANK = 16
GLA_GATE_NORMALIZER = 16.0
GLA_CHUNK = 64
DIL_HEADS = 4
DIL_HEAD_DIM = 128
DIL_CONFIGS = ((128, 1), (512, 4), (2048, 16))
MEM_HEADS = 4
MEM_HEAD_DIM = 128
REL_BUCKETS = 32
REL_MAX_DISTANCE = 1024
D_FF = 4 * D_MODEL
EPS = 1e-6
NEG_INF = -1e30

GLA_QK_WIDTH = GLA_HEADS * GLA_DK
GLA_V_WIDTH = GLA_HEADS * GLA_DV
DIL_WIDTH = DIL_HEADS * DIL_HEAD_DIM
MEM_WIDTH = MEM_HEADS * MEM_HEAD_DIM
MIX_WIDTH = GLA_V_WIDTH + DIL_WIDTH + MEM_WIDTH

LANES = 128

COL_GQ = 0
COL_GK = COL_GQ + GLA_QK_WIDTH // LANES
COL_GV = COL_GK + GLA_QK_WIDTH // LANES
COL_GR = COL_GV + GLA_V_WIDTH // LANES
COL_DQ = COL_GR + GLA_V_WIDTH // LANES
COL_DK = COL_DQ + DIL_WIDTH // LANES
COL_DV = COL_DK + DIL_WIDTH // LANES
COL_MQ = COL_DV + DIL_WIDTH // LANES
COL_LR = COL_MQ + MEM_WIDTH // LANES
PROJ_COLS = COL_LR + 2
PROJ_WIDTH = PROJ_COLS * LANES

GLA_BLOCK = 256

DIL_BAND = 64
DIL_TQ = 128
DIL_TK = DIL_TQ + 2 * DIL_BAND
DIL_DILATIONS = tuple(d for _, d in DIL_CONFIGS)
DIL_TB = DIL_TQ * max(DIL_DILATIONS)

VMEM_LIMIT = 48 * 1024 * 1024


def _params(n_axes, vmem=VMEM_LIMIT):
    return pltpu.CompilerParams(dimension_semantics=("arbitrary",) * n_axes, vmem_limit_bytes=vmem)


def _dot(a, b):
    return jnp.dot(a, b, preferred_element_type=F32)


def _dot_nt(a, b):
    return lax.dot_general(a, b, (((1,), (1,)), ((), ())), preferred_element_type=F32)


def _dot_tn(a, b):
    return lax.dot_general(a, b, (((0,), (0,)), ((), ())), preferred_element_type=F32)


def _rms(x, gain):
    return x * lax.rsqrt(jnp.mean(x * x, axis=-1, keepdims=True) + EPS) * gain


def _in_proj_kernel(x_ref, g_ref, w_ref, o_ref, h_ref, *, n_sub):
    j = pl.program_id(1)
    sub = x_ref.shape[0] // n_sub

    @pl.when(j == 0)
    def _():
        for c in range(n_sub):
            rows = slice(c * sub, (c + 1) * sub)
            h = _rms(x_ref[rows, :], g_ref[...]).astype(h_ref.dtype)
            h_ref[rows, :] = h
            o_ref[rows, :] = _dot(h, w_ref[...]).astype(o_ref.dtype)

    @pl.when(j != 0)
    def _():
        o_ref[...] = _dot(h_ref[...], w_ref[...]).astype(o_ref.dtype)


def _in_proj(x, gain, w, layer, tm=1024, tn=768, n_sub=4):
    s, d = x.shape
    n = w.shape[-1]
    return pl.pallas_call(
        functools.partial(_in_proj_kernel, n_sub=n_sub),
        grid=(s // tm, n // tn),
        in_specs=[
            pl.BlockSpec((tm, d), lambda i, j: (i, 0)),
            pl.BlockSpec((None, 1, d), lambda i, j: (layer, 0, 0)),
            pl.BlockSpec((None, d, tn), lambda i, j: (layer, 0, j)),
        ],
        out_specs=pl.BlockSpec((tm, tn), lambda i, j: (i, j)),
        out_shape=jax.ShapeDtypeStruct((s, n), BF16),
        scratch_shapes=[pltpu.VMEM((tm, d), BF16)],
        compiler_params=_params(2),
        name="in_proj",
    )(x, gain, w)


def _log_sigmoid(x):
    return jnp.minimum(x, 0.0) - jnp.log(1.0 + jnp.exp(-jnp.abs(x)))


GLA_LEVELS = (GLA_BLOCK // GLA_CHUNK).bit_length() - 1


def _gla_tables(tri_ref, lvl_ref, reverse):
    t = GLA_BLOCK
    row = lax.broadcasted_iota(jnp.int32, (t, t), 0)
    col = lax.broadcasted_iota(jnp.int32, (t, t), 1)
    valid = (col >= row) if reverse else (col <= row)
    tri_ref[...] = jnp.where(valid, 1.0, 0.0).astype(tri_ref.dtype)
    shift = GLA_CHUNK.bit_length() - 1
    lvl_ref[...] = jnp.where(jnp.right_shift(row, shift) == jnp.right_shift(col, shift),
                             jnp.where(valid, 0, -1), -1)


def _gla_boundary(cum, group, reverse):
    t = cum.shape[0]
    pieces = []
    for a in range(0, t, group):
        idx = a + group if reverse else a - 1
        if idx < 0 or idx >= t:
            ref_row = jnp.zeros((1, cum.shape[1]), cum.dtype)
        else:
            ref_row = cum[idx:idx + 1]
        pieces.append(jnp.broadcast_to(ref_row, (group, cum.shape[1])))
    return jnp.concatenate(pieces, axis=0)


def _place_rows(part, row0, total_rows):
    pieces = []
    if row0 > 0:
        pieces.append(jnp.zeros((row0, part.shape[1]), part.dtype))
    pieces.append(part)
    rest = total_rows - row0 - part.shape[0]
    if rest > 0:
        pieces.append(jnp.zeros((rest, part.shape[1]), part.dtype))
    return jnp.concatenate(pieces, axis=0)


def _gla_block(q_ref, k_ref, v_ref, lr_ref, up_ref, bias_ref, o_ref, s_ref, tri_ref, lvl_ref, r0, reverse):
    t = GLA_BLOCK
    rows = slice(r0, r0 + t)
    logits = _dot(lr_ref[rows, :], up_ref[...]) + bias_ref[...]
    g = _log_sigmoid(logits) * (1.0 / GLA_GATE_NORMALIZER)
    g_hi = g.astype(BF16)
    g_lo = (g - g_hi.astype(F32)).astype(BF16)
    tri = tri_ref[...]
    cum = _dot(tri, g_hi) + _dot(tri, g_lo)

    q = q_ref[rows, :].astype(F32) * (GLA_DK ** -0.5)
    k = k_ref[rows, :].astype(F32)
    v = v_ref[rows, :]
    lvl = lvl_ref[...]

    ref0 = _gla_boundary(cum, GLA_CHUNK, reverse)
    q_dec = (q * jnp.exp(cum - ref0)).astype(BF16)
    k_inv = (k * jnp.exp(ref0 - cum)).astype(BF16)
    q_segs, k_segs = [], []
    for level in range(1, GLA_LEVELS + 1):
        group = GLA_CHUNK << level
        for a0 in range(0, t, group):
            mid = a0 + group // 2
            att = slice(a0, mid) if reverse else slice(mid, a0 + group)
            src = slice(mid, a0 + group) if reverse else slice(a0, mid)
            ref_row = cum[mid:mid + 1] if reverse else cum[mid - 1:mid]
            q_part = (q[att] * jnp.exp(cum[att] - ref_row)).astype(BF16)
            k_part = (k[src] * jnp.exp(ref_row - cum[src])).astype(BF16)
            q_segs.append(_place_rows(q_part, att.start, t))
            k_segs.append(_place_rows(k_part, src.start, t))
    cross = _dot_nt(jnp.concatenate(q_segs, axis=1), jnp.concatenate(k_segs, axis=1))
    a = jnp.where(lvl == 0, _dot_nt(q_dec, k_inv), cross)

    total = cum[0:1] if reverse else cum[t - 1:t]
    state = s_ref[...]
    q_in = (q * jnp.exp(cum)).astype(BF16)
    o = _dot(a.astype(BF16), v) + _dot_nt(q_in, state.astype(BF16))
    o_ref[rows, :] = o.astype(o_ref.dtype)
    k_out = (k * jnp.exp(total - cum)).astype(BF16)
    s_ref[...] = state * jnp.exp(total) + _dot_tn(v, k_out)


def _gla_kernel(qf, kf, vf, lrf, qb, kb, vb, lrb, upf, upb, bsf, bsb, of, ob,
                sf, sb, tri_f, tri_b, lvl_f, lvl_b):
    @pl.when(pl.program_id(1) == 0)
    def _():
        sf[...] = jnp.zeros_like(sf)
        sb[...] = jnp.zeros_like(sb)
        _gla_tables(tri_f, lvl_f, reverse=False)
        _gla_tables(tri_b, lvl_b, reverse=True)

    starts = range(0, qf.shape[0], GLA_BLOCK)
    for r_fwd, r_bwd in zip(starts, reversed(starts)):
        _gla_block(qf, kf, vf, lrf, upf, bsf, of, sf, tri_f, lvl_f, r_fwd, reverse=False)
        _gla_block(qb, kb, vb, lrb, upb, bsb, ob, sb, tri_b, lvl_b, r_bwd, reverse=True)


def _gla(proj, up_f, up_b, bias_f, bias_b, layer, t=1024):
    s = proj.shape[0]
    nb = s // t
    vcol = COL_GV * LANES // GLA_DV

    def row_specs(rmap):
        return [
            pl.BlockSpec((t, GLA_DK), lambda h, i: (rmap(i), COL_GQ + h)),
            pl.BlockSpec((t, GLA_DK), lambda h, i: (rmap(i), COL_GK + h)),
            pl.BlockSpec((t, GLA_DV), lambda h, i: (rmap(i), vcol + h)),
            pl.BlockSpec((t, LANES), lambda h, i: (rmap(i), COL_LR)),
        ]

    fwd = lambda i: i
    bwd = lambda i: nb - 1 - i
    up_spec = pl.BlockSpec((None, LANES, GLA_DK), lambda h, i: (layer, 0, h))
    bias_spec = pl.BlockSpec((None, 1, GLA_DK), lambda h, i: (layer, 0, h))
    out_sds = jax.ShapeDtypeStruct((s, GLA_V_WIDTH), BF16)
    state = pltpu.VMEM((GLA_DV, GLA_DK), F32)
    tri = pltpu.VMEM((GLA_BLOCK, GLA_BLOCK), BF16)
    lvl = pltpu.VMEM((GLA_BLOCK, GLA_BLOCK), jnp.int32)
    return pl.pallas_call(
        _gla_kernel,
        grid=(GLA_HEADS, nb),
        in_specs=row_specs(fwd) + row_specs(bwd) + [up_spec, up_spec, bias_spec, bias_spec],
        out_specs=[
            pl.BlockSpec((t, GLA_DV), lambda h, i: (fwd(i), h)),
            pl.BlockSpec((t, GLA_DV), lambda h, i: (bwd(i), h)),
        ],
        out_shape=[out_sds, out_sds],
        scratch_shapes=[state, state, tri, tri, lvl, lvl],
        compiler_params=_params(2),
        name="gla_scan",
    )(proj, proj, proj, proj, proj, proj, proj, proj, up_f, up_b, bias_f, bias_b)


def _dil_tile(q, k, v, bias, edge):
    sc = _dot_nt(q, k) * (DIL_HEAD_DIM ** -0.5) + bias
    if edge is not None:
        sc = sc + edge
    m = jnp.max(sc, axis=-1, keepdims=True)
    p = jnp.exp(sc - m)
    den = jnp.sum(p, axis=-1, keepdims=True)
    o = _dot(p.astype(BF16), v) / den
    return o, jnp.broadcast_to(m + jnp.log(den), o.shape)


def _dil_kernel(q_ref, k_ref, v_ref, bias_ref, gain_ref, out_ref,
                stage, kg1, vg1, kg4, vg4, kg16, vg16, qp4, qp16, o1, o4, o16, l1, l4, l16):
    i = pl.program_id(1)
    s = k_ref.shape[0]
    tb = q_ref.shape[0]
    k_bufs = dict(zip(DIL_DILATIONS, (kg1, kg4, kg16)))
    v_bufs = dict(zip(DIL_DILATIONS, (vg1, vg4, vg16)))
    q_perm = dict(zip(DIL_DILATIONS, (None, qp4, qp16)))
    o_bufs = dict(zip(DIL_DILATIONS, (o1, o4, o16)))
    l_bufs = dict(zip(DIL_DILATIONS, (l1, l4, l16)))

    def scatter_residues(dst, d, residue_stride, offset):
        per = tb // d
        for r in range(d):
            start = offset + r * residue_stride
            if not isinstance(start, int):
                start = pl.multiple_of(start, DIL_BAND)
            dst[pl.ds(start, per), :] = stage[pl.ds(r, per, stride=d), :].astype(dst.dtype)

    @pl.when(i == 0)
    def _():
        zeros = jnp.zeros((DIL_BAND, DIL_HEAD_DIM), BF16)
        for buf in (kg1, vg1, kg4, vg4, kg16, vg16):
            buf[0:DIL_BAND, :] = zeros
            buf[DIL_BAND + s:, :] = zeros
        kg1[DIL_BAND:DIL_BAND + s, :] = k_ref[...]
        vg1[DIL_BAND:DIL_BAND + s, :] = v_ref[...]

        def body(c, carry):
            base = pl.multiple_of(c * tb, tb)
            for src, bufs in ((k_ref, k_bufs), (v_ref, v_bufs)):
                stage[...] = src[pl.ds(base, tb), :].astype(F32)
                for d in DIL_DILATIONS[1:]:
                    scatter_residues(bufs[d], d, s // d, DIL_BAND + c * (tb // d))
            return carry

        lax.fori_loop(0, s // tb, body, 0)

    stage[...] = q_ref[...].astype(F32)
    for d in DIL_DILATIONS[1:]:
        scatter_residues(q_perm[d], d, tb // d, 0)

    lane = lax.broadcasted_iota(jnp.int32, (1, DIL_TK), 1)
    for b, d in enumerate(DIL_DILATIONS):
        sub_len = s // d
        per = tb // d
        n_tiles = per // DIL_TQ
        bias = bias_ref[b]
        for r in range(d):
            for tj in range(n_tiles):
                u0 = i * per + tj * DIL_TQ
                q_src = q_ref if d == 1 else q_perm[d]
                q = q_src[r * per + tj * DIL_TQ:r * per + (tj + 1) * DIL_TQ, :]
                k_start = pl.multiple_of(r * sub_len + u0, DIL_TQ)
                k = k_bufs[d][pl.ds(k_start, DIL_TK), :]
                v = v_bufs[d][pl.ds(k_start, DIL_TK), :]
                edge = None
                if tj == 0 or tj == n_tiles - 1:
                    key_pos = u0 - DIL_BAND + lane
                    edge = jnp.where((key_pos >= 0) & (key_pos < sub_len), 0.0, NEG_INF)
                o, lse = _dil_tile(q, k, v, bias, edge)
                if d == 1:
                    rows = pl.ds(tj * DIL_TQ, DIL_TQ)
                else:
                    rows = pl.ds(r + d * tj * DIL_TQ, DIL_TQ, stride=d)
                o_bufs[d][rows, :] = o
                l_bufs[d][rows, :] = lse

    lses = [l_bufs[d][...] for d in DIL_DILATIONS]
    m = functools.reduce(jnp.maximum, lses)
    es = [jnp.exp(l - m) for l in lses]
    num = sum(e * o_bufs[d][...] for e, d in zip(es, DIL_DILATIONS))
    mixed = num / sum(es)
    out_ref[...] = _head_rms(mixed, gain_ref[...]).astype(out_ref.dtype)


def _dil_attention(proj, bias, gain, layer):
    s = proj.shape[0]
    tb = DIL_TB
    guarded = pltpu.VMEM((s + 2 * DIL_BAND, DIL_HEAD_DIM), BF16)
    tile_f32 = pltpu.VMEM((tb, DIL_HEAD_DIM), F32)
    tile_bf16 = pltpu.VMEM((tb, DIL_HEAD_DIM), BF16)
    return pl.pallas_call(
        _dil_kernel,
        grid=(DIL_HEADS, s // tb),
        in_specs=[
            pl.BlockSpec((tb, DIL_HEAD_DIM), lambda h, i: (i, COL_DQ + h)),
            pl.BlockSpec((s, DIL_HEAD_DIM), lambda h, i: (0, COL_DK + h)),
            pl.BlockSpec((s, DIL_HEAD_DIM), lambda h, i: (0, COL_DV + h)),
            pl.BlockSpec((len(DIL_DILATIONS), None, DIL_TQ, DIL_TK), lambda h, i: (0, h, 0, 0)),
            pl.BlockSpec((None, 1, DIL_HEAD_DIM), lambda h, i: (layer, 0, h)),
        ],
        out_specs=pl.BlockSpec((tb, DIL_HEAD_DIM), lambda h, i: (i, h)),
        out_shape=jax.ShapeDtypeStruct((s, DIL_WIDTH), BF16),
        scratch_shapes=[tile_f32] + [guarded] * 6 + [tile_bf16] * 2 + [tile_f32] * 6,
        compiler_params=_params(2),
        name="dil_attn",
    )(proj, proj, proj, bias, gain)


def _t5_bucket(rel):
    half = REL_BUCKETS // 2
    max_exact = half // 2
    ret = jnp.where(rel > 0, half, 0)
    n = jnp.abs(rel)
    nf = jnp.maximum(n, 1).astype(F32)
    large = max_exact + (jnp.log(nf / max_exact) / math.log(REL_MAX_DISTANCE / max_exact)
                         * (half - max_exact)).astype(jnp.int32)
    large = jnp.minimum(large, half - 1)
    return ret + jnp.where(n < max_exact, n, large)


def _dil_bias_tiles(rel_bias):
    rel = jnp.arange(DIL_TK)[None, :] - DIL_BAND - jnp.arange(DIL_TQ)[:, None]
    band = jnp.abs(rel) <= DIL_BAND
    tiles = []
    for d in DIL_DILATIONS:
        onehot = (_t5_bucket(rel * d)[..., None] == jnp.arange(REL_BUCKETS)).astype(F32)
        bias = jnp.einsum("qkb,bh->hqk", onehot, rel_bias.astype(F32), precision=lax.Precision.HIGHEST)
        tiles.append(jnp.where(band[None], bias, NEG_INF))
    return jnp.stack(tiles, axis=0)


def _mem_kv_kernel(mem_ref, g_ref, w_ref, o_ref):
    h = _rms(mem_ref[...], g_ref[...]).astype(BF16)
    o_ref[...] = _dot(h, w_ref[...]).astype(o_ref.dtype)


def _mem_kv(mem, gain, w):
    m, d = mem.shape
    n = w.shape[-1]
    return pl.pallas_call(
        _mem_kv_kernel,
        grid=(DEPTH,),
        in_specs=[
            pl.BlockSpec((m, d), lambda l: (0, 0)),
            pl.BlockSpec((None, 1, d), lambda l: (l, 0, 0)),
            pl.BlockSpec((None, d, n), lambda l: (l, 0, 0)),
        ],
        out_specs=pl.BlockSpec((None, m, n), lambda l: (l, 0, 0)),
        out_shape=jax.ShapeDtypeStruct((DEPTH, m, n), BF16),
        compiler_params=_params(1),
        name="mem_kv",
    )(mem, gain, w)


def _head_rms(o, gain):
    return o * lax.rsqrt(jnp.mean(o * o, axis=-1, keepdims=True) + EPS) * gain


def _mix_kernel(of_ref, ob_ref, gr_ref, dil_ref, mq_ref, km_ref, vm_ref, ggla_ref, gmem_ref, w_ref, x_ref,
                out_ref, gla_ref, mem_ref):
    dil0 = GLA_V_WIDTH
    mem0 = GLA_V_WIDTH + DIL_WIDTH
    out_ref[...] = x_ref[...] + _dot(dil_ref[...], w_ref[dil0:mem0, :])

    for h in range(GLA_HEADS):
        sl = slice(h * GLA_DV, (h + 1) * GLA_DV)
        o = of_ref[:, sl].astype(F32) + ob_ref[:, sl].astype(F32)
        r = gr_ref[:, sl].astype(F32)
        gate = r * (1.0 / (1.0 + jnp.exp(-r)))
        gla_ref[:, sl] = (_head_rms(o, ggla_ref[:, sl]) * gate).astype(gla_ref.dtype)
    out_ref[...] += _dot(gla_ref[...], w_ref[0:dil0, :])

    scale = MEM_HEAD_DIM ** -0.5
    for h in range(MEM_HEADS):
        sl = slice(h * MEM_HEAD_DIM, (h + 1) * MEM_HEAD_DIM)
        sc = _dot_nt(mq_ref[:, sl], km_ref[:, sl]) * scale
        m = jnp.max(sc, axis=-1, keepdims=True)
        p = jnp.exp(sc - m)
        den = jnp.sum(p, axis=-1, keepdims=True)
        o = _dot(p.astype(BF16), vm_ref[:, sl]) / den
        mem_ref[:, sl] = _head_rms(o, gmem_ref[:, sl]).astype(mem_ref.dtype)
    out_ref[...] += _dot(mem_ref[...], w_ref[mem0:, :])


def _mix_out(x, proj, o_f, o_b, dil, kv, g_gla, g_mem, w_out, layer, t=512):
    s, d = x.shape
    rows = lambda w, c=0: pl.BlockSpec((t, w), lambda i: (i, c))
    per_layer = lambda shape, c=0: pl.BlockSpec((None,) + shape, lambda i: (layer, 0, c),
                                                pipeline_mode=pl.Buffered(1))
    return pl.pallas_call(
        _mix_kernel,
        grid=(s // t,),
        in_specs=[
            rows(GLA_V_WIDTH), rows(GLA_V_WIDTH),
            rows(GLA_V_WIDTH, COL_GR * LANES // GLA_V_WIDTH),
            rows(DIL_WIDTH),
            rows(MEM_WIDTH, COL_MQ * LANES // MEM_WIDTH),
            per_layer((MEM_LEN, MEM_WIDTH), 0), per_layer((MEM_LEN, MEM_WIDTH), 1),
            per_layer((1, GLA_V_WIDTH)), per_layer((1, MEM_WIDTH)),
            per_layer((MIX_WIDTH, d)),
            rows(d),
        ],
        out_specs=rows(d),
        out_shape=jax.ShapeDtypeStruct((s, d), F32),
        scratch_shapes=[pltpu.VMEM((t, GLA_V_WIDTH), BF16), pltpu.VMEM((t, MEM_WIDTH), BF16)],
        compiler_params=_params(1),
        name="mix_out_proj",
    )(o_f, o_b, proj, dil, proj, kv, kv, g_gla, g_mem, w_out, x)


def _mlp_kernel(x_ref, g_ref, wu_ref, wd_ref, gf_ref, o_ref, h_ref, *, final_norm, n_sub):
    f = pl.program_id(1)
    sub = x_ref.shape[0] // n_sub

    def hidden(h):
        return jnp.square(jnp.maximum(_dot(h, wu_ref[...]), 0.0)).astype(BF16)

    @pl.when(f == 0)
    def _():
        for c in range(n_sub):
            rows = slice(c * sub, (c + 1) * sub)
            x = x_ref[rows, :]
            h = _rms(x, g_ref[...]).astype(h_ref.dtype)
            h_ref[rows, :] = h
            o_ref[rows, :] = x + _dot(hidden(h), wd_ref[...])

    @pl.when(f != 0)
    def _():
        o_ref[...] += _dot(hidden(h_ref[...]), wd_ref[...])

    if final_norm:
        @pl.when(f == pl.num_programs(1) - 1)
        def _():
            o_ref[...] = _rms(o_ref[...], gf_ref[...])


def _mlp(x, gain, w_up, w_down, gain_final, layer, final_norm, tm=512, tf=1024, n_sub=2):
    s, d = x.shape
    ff = w_up.shape[-1]
    return pl.pallas_call(
        functools.partial(_mlp_kernel, final_norm=final_norm, n_sub=n_sub),
        grid=(s // tm, ff // tf),
        in_specs=[
            pl.BlockSpec((tm, d), lambda i, f: (i, 0)),
            pl.BlockSpec((None, 1, d), lambda i, f: (layer, 0, 0)),
            pl.BlockSpec((None, d, tf), lambda i, f: (layer, 0, f)),
            pl.BlockSpec((None, tf, d), lambda i, f: (layer, f, 0)),
            pl.BlockSpec((1, d), lambda i, f: (0, 0)),
        ],
        out_specs=pl.BlockSpec((tm, d), lambda i, f: (i, 0)),
        out_shape=jax.ShapeDtypeStruct((s, d), F32),
        scratch_shapes=[pltpu.VMEM((tm, d), BF16)],
        compiler_params=_params(2),
        name="mlp",
    )(x, gain, w_up, w_down, gain_final)


def _w_in_prep_kernel(w_ref, o_ref):
    lr0 = 2 * GLA_QK_WIDTH + 2 * GLA_V_WIDTH
    lr_w = 2 * GLA_GATE_RANK
    tail = w_ref.shape[1] - lr0 - lr_w
    o_ref[:, 0:lr0] = w_ref[:, 0:lr0].astype(o_ref.dtype)
    rest = w_ref[:, lr0:].astype(o_ref.dtype)
    o_ref[:, lr0:lr0 + tail] = rest[:, lr_w:]
    pad = jnp.zeros((o_ref.shape[0], o_ref.shape[1] - lr0 - tail - lr_w), o_ref.dtype)
    o_ref[:, lr0 + tail:] = jnp.concatenate([rest[:, 0:lr_w], pad], axis=1)


def _reorder_w_in(w_in, rows=256):
    depth, d, n = w_in.shape
    assert COL_LR * LANES == n - 2 * GLA_GATE_RANK
    return pl.pallas_call(
        _w_in_prep_kernel,
        grid=(depth, d // rows),
        in_specs=[pl.BlockSpec((None, rows, n), lambda l, i: (l, i, 0))],
        out_specs=pl.BlockSpec((None, rows, PROJ_WIDTH), lambda l, i: (l, i, 0)),
        out_shape=jax.ShapeDtypeStruct((depth, d, PROJ_WIDTH), BF16),
        compiler_params=_params(2),
        name="w_in_prep",
    )(w_in)


def _pad_gate_up(up, row0):
    out = jnp.zeros((up.shape[0], LANES, up.shape[-1]), BF16)
    return out.at[:, row0:row0 + GLA_GATE_RANK, :].set(up.astype(BF16))


def kernel(x, mem, norm_mix, w_in, gla_gate_up_fwd, gla_gate_bias_fwd, gla_gate_up_bwd, gla_gate_bias_bwd, gla_norm, rel_bias, dil_norm, mem_norm, w_mem_kv, mem_out_norm, w_out, norm_mlp, w_up, w_down, norm_final):
    batch, seq, d = x.shape
    assert batch == 1 and seq == SEQ and d == D_MODEL
    xs = x.reshape(seq, d)
    row = lambda g: g.reshape(g.shape[0], 1, g.shape[-1])

    w_in_r = _reorder_w_in(w_in)
    up_f = _pad_gate_up(gla_gate_up_fwd, 0)
    up_b = _pad_gate_up(gla_gate_up_bwd, GLA_GATE_RANK)
    w_out_b = w_out.astype(BF16)
    w_up_b = w_up.astype(BF16)
    w_down_b = w_down.astype(BF16)
    dil_bias = _dil_bias_tiles(rel_bias)
    kv = _mem_kv(mem.reshape(MEM_LEN, d), row(mem_norm), w_mem_kv.astype(BF16))

    for layer in range(DEPTH):
        proj = _in_proj(xs, row(norm_mix), w_in_r, layer)
        o_f, o_b = _gla(proj, up_f, up_b, row(gla_gate_bias_fwd), row(gla_gate_bias_bwd), layer)
        dil = _dil_attention(proj, dil_bias, row(dil_norm), layer)
        xs = _mix_out(xs, proj, o_f, o_b, dil, kv, row(gla_norm), row(mem_out_norm), w_out_b, layer)
        xs = _mlp(xs, row(norm_mlp), w_up_b, w_down_b, norm_final.reshape(1, d), layer,
                  final_norm=(layer == DEPTH - 1))
    return xs.reshape(batch, seq, d)
```

```python
import functools
import math

import jax
import jax.numpy as jnp
from jax import lax
from jax.experimental import pallas as pl
from jax.experimental.pallas import tpu as pltpu

F32 = jnp.float32
BF16 = jnp.bfloat16

D_MODEL = 2048
SEQ = 8192
DEPTH = 4
MEM_LEN = 256
GLA_HEADS = 4
GLA_DK = 128
GLA_DV = 256
GLA_GATE_RANK = 16
GLA_GATE_NORMALIZER = 16.0
GLA_CHUNK = 64
DIL_HEADS = 4
DIL_HEAD_DIM = 128
DIL_CONFIGS = ((128, 1), (512, 4), (2048, 16))
MEM_HEADS = 4
MEM_HEAD_DIM = 128
REL_BUCKETS = 32
REL_MAX_DISTANCE = 1024
D_FF = 4 * D_MODEL
EPS = 1e-6
NEG_INF = -1e30

GLA_QK_WIDTH = GLA_HEADS * GLA_DK
GLA_V_WIDTH = GLA_HEADS * GLA_DV
DIL_WIDTH = DIL_HEADS * DIL_HEAD_DIM
MEM_WIDTH = MEM_HEADS * MEM_HEAD_DIM
MIX_WIDTH = GLA_V_WIDTH + DIL_WIDTH + MEM_WIDTH

LANES = 128

COL_GQ = 0
COL_GK = COL_GQ + GLA_QK_WIDTH // LANES
COL_GV = COL_GK + GLA_QK_WIDTH // LANES
COL_GR = COL_GV + GLA_V_WIDTH // LANES
COL_DQ = COL_GR + GLA_V_WIDTH // LANES
COL_DK = COL_DQ + DIL_WIDTH // LANES
COL_DV = COL_DK + DIL_WIDTH // LANES
COL_MQ = COL_DV + DIL_WIDTH // LANES
COL_LR = COL_MQ + MEM_WIDTH // LANES
PROJ_COLS = COL_LR + 2
PROJ_WIDTH = PROJ_COLS * LANES

GLA_BLOCK = 256

DIL_BAND = 64
DIL_TQ = 128
DIL_TK = DIL_TQ + 2 * DIL_BAND
DIL_DILATIONS = tuple(d for _, d in DIL_CONFIGS)
DIL_TB = DIL_TQ * max(DIL_DILATIONS)

VMEM_LIMIT = 48 * 1024 * 1024


def _params(n_axes, vmem=VMEM_LIMIT):
    return pltpu.CompilerParams(dimension_semantics=("arbitrary",) * n_axes, vmem_limit_bytes=vmem)


def _dot(a, b):
    return jnp.dot(a, b, preferred_element_type=F32)


def _dot_nt(a, b):
    return lax.dot_general(a, b, (((1,), (1,)), ((), ())), preferred_element_type=F32)


def _dot_tn(a, b):
    return lax.dot_general(a, b, (((0,), (0,)), ((), ())), preferred_element_type=F32)


def _rms(x, gain):
    return x * lax.rsqrt(jnp.mean(x * x, axis=-1, keepdims=True) + EPS) * gain


def _in_proj_kernel(x_ref, g_ref, w_ref, o_ref, h_ref, *, n_sub):
    j = pl.program_id(1)
    sub = x_ref.shape[0] // n_sub

    @pl.when(j == 0)
    def _():
        for c in range(n_sub):
            rows = slice(c * sub, (c + 1) * sub)
            h = _rms(x_ref[rows, :], g_ref[...]).astype(h_ref.dtype)
            h_ref[rows, :] = h
            o_ref[rows, :] = _dot_nt(h, w_ref[...]).astype(o_ref.dtype)

    @pl.when(j != 0)
    def _():
        o_ref[...] = _dot_nt(h_ref[...], w_ref[...]).astype(o_ref.dtype)


def _in_proj(x, gain, w, layer, tm=1024, tn=768, n_sub=4):
    s, d = x.shape
    n = w.shape[1]
    return pl.pallas_call(
        functools.partial(_in_proj_kernel, n_sub=n_sub),
        grid=(s // tm, n // tn),
        in_specs=[
            pl.BlockSpec((tm, d), lambda i, j: (i, 0)),
            pl.BlockSpec((None, 1, d), lambda i, j: (layer, 0, 0)),
            pl.BlockSpec((None, tn, d), lambda i, j: (layer, j, 0)),
        ],
        out_specs=pl.BlockSpec((tm, tn), lambda i, j: (i, j)),
        out_shape=jax.ShapeDtypeStruct((s, n), BF16),
        scratch_shapes=[pltpu.VMEM((tm, d), BF16)],
        compiler_params=_params(2),
        name="in_proj",
    )(x, gain, w)


def _log_sigmoid(x):
    return jnp.minimum(x, 0.0) - jnp.log(1.0 + jnp.exp(-jnp.abs(x)))


GLA_LEVELS = (GLA_BLOCK // GLA_CHUNK).bit_length() - 1


def _gla_tables(tri_ref, lvl_ref, reverse):
    t = GLA_BLOCK
    row = lax.broadcasted_iota(jnp.int32, (t, t), 0)
    col = lax.broadcasted_iota(jnp.int32, (t, t), 1)
    valid = (col >= row) if reverse else (col <= row)
    tri_ref[...] = jnp.where(valid, 1.0, 0.0).astype(tri_ref.dtype)
    shift = GLA_CHUNK.bit_length() - 1
    lvl_ref[...] = jnp.where(jnp.right_shift(row, shift) == jnp.right_shift(col, shift),
                             jnp.where(valid, 0, -1), -1)


def _gla_boundary(cum, group, reverse):
    t = cum.shape[0]
    pieces = []
    for a in range(0, t, group):
        idx = a + group if reverse else a - 1
        if idx < 0 or idx >= t:
            ref_row = jnp.zeros((1, cum.shape[1]), cum.dtype)
        else:
            ref_row = cum[idx:idx + 1]
        pieces.append(jnp.broadcast_to(ref_row, (group, cum.shape[1])))
    return jnp.concatenate(pieces, axis=0)


def _place_rows(part, row0, total_rows):
    pieces = []
    if row0 > 0:
        pieces.append(jnp.zeros((row0, part.shape[1]), part.dtype))
    pieces.append(part)
    rest = total_rows - row0 - part.shape[0]
    if rest > 0:
        pieces.append(jnp.zeros((rest, part.shape[1]), part.dtype))
    return jnp.concatenate(pieces, axis=0)


def _gla_block(q_ref, k_ref, v_ref, lr_ref, up_ref, bias_ref, o_ref, s_ref, tri_ref, lvl_ref, r0, reverse):
    t = GLA_BLOCK
    rows = slice(r0, r0 + t)
    logits = _dot(lr_ref[rows, :], up_ref[...]) + bias_ref[...]
    g = _log_sigmoid(logits) * (1.0 / GLA_GATE_NORMALIZER)
    g_hi = g.astype(BF16)
    g_lo = (g - g_hi.astype(F32)).astype(BF16)
    tri = tri_ref[...]
    cum = _dot(tri, g_hi) + _dot(tri, g_lo)

    q = q_ref[rows, :].astype(F32) * (GLA_DK ** -0.5)
    k = k_ref[rows, :].astype(F32)
    v = v_ref[rows, :]
    lvl = lvl_ref[...]

    ref0 = _gla_boundary(cum, GLA_CHUNK, reverse)
    q_dec = (q * jnp.exp(cum - ref0)).astype(BF16)
    k_inv = (k * jnp.exp(ref0 - cum)).astype(BF16)
    q_segs, k_segs = [], []
    for level in range(1, GLA_LEVELS + 1):
        group = GLA_CHUNK << level
        for a0 in range(0, t, group):
            mid = a0 + group // 2
            att = slice(a0, mid) if reverse else slice(mid, a0 + group)
            src = slice(mid, a0 + group) if reverse else slice(a0, mid)
            ref_row = cum[mid:mid + 1] if reverse else cum[mid - 1:mid]
            q_part = (q[att] * jnp.exp(cum[att] - ref_row)).astype(BF16)
            k_part = (k[src] * jnp.exp(ref_row - cum[src])).astype(BF16)
            q_segs.append(_place_rows(q_part, att.start, t))
            k_segs.append(_place_rows(k_part, src.start, t))
    cross = _dot_nt(jnp.concatenate(q_segs, axis=1), jnp.concatenate(k_segs, axis=1))
    a = jnp.where(lvl == 0, _dot_nt(q_dec, k_inv), cross)

    total = cum[0:1] if reverse else cum[t - 1:t]
    state = s_ref[...]
    q_in = (q * jnp.exp(cum)).astype(BF16)
    o = _dot(a.astype(BF16), v) + _dot_nt(q_in, state.astype(BF16))
    o_ref[rows, :] = o.astype(o_ref.dtype)
    k_out = (k * jnp.exp(total - cum)).astype(BF16)
    s_ref[...] = state * jnp.exp(total) + _dot_tn(v, k_out)


def _gla_kernel(qf, kf, vf, lrf, qb, kb, vb, lrb, upf, upb, bsf, bsb, wu_ref, wd_ref, of, ob, wu_out, wd_out,
                sf, sb, tri_f, tri_b, lvl_f, lvl_b):
    wu_out[...] = wu_ref[...].astype(wu_out.dtype)
    wd_out[...] = wd_ref[...].astype(wd_out.dtype)

    @pl.when(pl.program_id(1) == 0)
    def _():
        sf[...] = jnp.zeros_like(sf)
        sb[...] = jnp.zeros_like(sb)
        _gla_tables(tri_f, lvl_f, reverse=False)
        _gla_tables(tri_b, lvl_b, reverse=True)

    starts = range(0, qf.shape[0], GLA_BLOCK)
    for r_fwd, r_bwd in zip(starts, reversed(starts)):
        _gla_block(qf, kf, vf, lrf, upf, bsf, of, sf, tri_f, lvl_f, r_fwd, reverse=False)
        _gla_block(qb, kb, vb, lrb, upb, bsb, ob, sb, tri_b, lvl_b, r_bwd, reverse=True)


def _gla(proj, up_f, up_b, bias_f, bias_b, w_up, w_down, layer, t=1024):
    s = proj.shape[0]
    nb = s // t
    vcol = COL_GV * LANES // GLA_DV
    n_steps = GLA_HEADS * nb
    d, ff = w_up.shape[1:]
    slab = lambda rows, cols: (
        pl.BlockSpec((None, rows // n_steps, cols), lambda h, i: (layer, h * nb + i, 0)),
        pl.BlockSpec((rows // n_steps, cols), lambda h, i: (h * nb + i, 0)),
        jax.ShapeDtypeStruct((rows, cols), BF16))
    wu_in, wu_out, wu_sds = slab(d, ff)
    wd_in, wd_out, wd_sds = slab(ff, d)

    def row_specs(rmap):
        return [
            pl.BlockSpec((t, GLA_DK), lambda h, i: (rmap(i), COL_GQ + h)),
            pl.BlockSpec((t, GLA_DK), lambda h, i: (rmap(i), COL_GK + h)),
            pl.BlockSpec((t, GLA_DV), lambda h, i: (rmap(i), vcol + h)),
            pl.BlockSpec((t, LANES), lambda h, i: (rmap(i), COL_LR)),
        ]

    fwd = lambda i: i
    bwd = lambda i: nb - 1 - i
    up_spec = pl.BlockSpec((None, LANES, GLA_DK), lambda h, i: (layer, 0, h))
    bias_spec = pl.BlockSpec((None, 1, GLA_DK), lambda h, i: (layer, 0, h))
    out_sds = jax.ShapeDtypeStruct((s, GLA_V_WIDTH), BF16)
    state = pltpu.VMEM((GLA_DV, GLA_DK), F32)
    tri = pltpu.VMEM((GLA_BLOCK, GLA_BLOCK), BF16)
    lvl = pltpu.VMEM((GLA_BLOCK, GLA_BLOCK), jnp.int32)
    return pl.pallas_call(
        _gla_kernel,
        grid=(GLA_HEADS, nb),
        in_specs=row_specs(fwd) + row_specs(bwd) + [up_spec, up_spec, bias_spec, bias_spec, wu_in, wd_in],
        out_specs=[
            pl.BlockSpec((t, GLA_DV), lambda h, i: (fwd(i), h)),
            pl.BlockSpec((t, GLA_DV), lambda h, i: (bwd(i), h)),
            wu_out, wd_out,
        ],
        out_shape=[out_sds, out_sds, wu_sds, wd_sds],
        scratch_shapes=[state, state, tri, tri, lvl, lvl],
        compiler_params=_params(2),
        name="gla_scan",
    )(proj, proj, proj, proj, proj, proj, proj, proj, up_f, up_b, bias_f, bias_b, w_up, w_down)


def _dil_tile(q, k, v, bias, edge):
    sc = _dot_nt(q, k) * (DIL_HEAD_DIM ** -0.5) + bias
    if edge is not None:
        sc = sc + edge
    m = jnp.max(sc, axis=-1, keepdims=True)
    p = jnp.exp(sc - m)
    den = jnp.sum(p, axis=-1, keepdims=True)
    o = _dot(p.astype(BF16), v) / den
    return o, jnp.broadcast_to(m + jnp.log(den), o.shape)


def _dil_kernel(q_ref, k_ref, v_ref, bias_ref, gain_ref, out_ref,
                stage, kg1, vg1, kg4, vg4, kg16, vg16, qp4, qp16, o1, o4, o16, l1, l4, l16):
    i = pl.program_id(1)
    s = k_ref.shape[0]
    tb = q_ref.shape[0]
    k_bufs = dict(zip(DIL_DILATIONS, (kg1, kg4, kg16)))
    v_bufs = dict(zip(DIL_DILATIONS, (vg1, vg4, vg16)))
    q_perm = dict(zip(DIL_DILATIONS, (None, qp4, qp16)))
    o_bufs = dict(zip(DIL_DILATIONS, (o1, o4, o16)))
    l_bufs = dict(zip(DIL_DILATIONS, (l1, l4, l16)))

    def scatter_residues(dst, d, residue_stride, offset):
        per = tb // d
        for r in range(d):
            start = offset + r * residue_stride
            if not isinstance(start, int):
                start = pl.multiple_of(start, DIL_BAND)
            dst[pl.ds(start, per), :] = stage[pl.ds(r, per, stride=d), :].astype(dst.dtype)

    @pl.when(i == 0)
    def _():
        zeros = jnp.zeros((DIL_BAND, DIL_HEAD_DIM), BF16)
        for buf in (kg1, vg1, kg4, vg4, kg16, vg16):
            buf[0:DIL_BAND, :] = zeros
            buf[DIL_BAND + s:, :] = zeros
        kg1[DIL_BAND:DIL_BAND + s, :] = k_ref[...]
        vg1[DIL_BAND:DIL_BAND + s, :] = v_ref[...]

        def body(c, carry):
            base = pl.multiple_of(c * tb, tb)
            for src, bufs in ((k_ref, k_bufs), (v_ref, v_bufs)):
                stage[...] = src[pl.ds(base, tb), :].astype(F32)
                for d in DIL_DILATIONS[1:]:
                    scatter_residues(bufs[d], d, s // d, DIL_BAND + c * (tb // d))
            return carry

        lax.fori_loop(0, s // tb, body, 0)

    stage[...] = q_ref[...].astype(F32)
    for d in DIL_DILATIONS[1:]:
        scatter_residues(q_perm[d], d, tb // d, 0)

    lane = lax.broadcasted_iota(jnp.int32, (1, DIL_TK), 1)
    for b, d in enumerate(DIL_DILATIONS):
        sub_len = s // d
        per = tb // d
        n_tiles = per // DIL_TQ
        bias = bias_ref[b]
        for r in range(d):
            for tj in range(n_tiles):
                u0 = i * per + tj * DIL_TQ
                q_src = q_ref if d == 1 else q_perm[d]
                q = q_src[r * per + tj * DIL_TQ:r * per + (tj + 1) * DIL_TQ, :]
                k_start = pl.multiple_of(r * sub_len + u0, DIL_TQ)
                k = k_bufs[d][pl.ds(k_start, DIL_TK), :]
                v = v_bufs[d][pl.ds(k_start, DIL_TK), :]
                edge = None
                if tj == 0 or tj == n_tiles - 1:
                    key_pos = u0 - DIL_BAND + lane
                    edge = jnp.where((key_pos >= 0) & (key_pos < sub_len), 0.0, NEG_INF)
                o, lse = _dil_tile(q, k, v, bias, edge)
                if d == 1:
                    rows = pl.ds(tj * DIL_TQ, DIL_TQ)
                else:
                    rows = pl.ds(r + d * tj * DIL_TQ, DIL_TQ, stride=d)
                o_bufs[d][rows, :] = o
                l_bufs[d][rows, :] = lse

    lses = [l_bufs[d][...] for d in DIL_DILATIONS]
    m = functools.reduce(jnp.maximum, lses)
    es = [jnp.exp(l - m) for l in lses]
    num = sum(e * o_bufs[d][...] for e, d in zip(es, DIL_DILATIONS))
    mixed = num / sum(es)
    out_ref[...] = _head_rms(mixed, gain_ref[...]).astype(out_ref.dtype)


def _dil_attention(proj, bias, gain, layer):
    s = proj.shape[0]
    tb = DIL_TB
    guarded = pltpu.VMEM((s + 2 * DIL_BAND, DIL_HEAD_DIM), BF16)
    tile_f32 = pltpu.VMEM((tb, DIL_HEAD_DIM), F32)
    tile_bf16 = pltpu.VMEM((tb, DIL_HEAD_DIM), BF16)
    return pl.pallas_call(
        _dil_kernel,
        grid=(DIL_HEADS, s // tb),
        in_specs=[
            pl.BlockSpec((tb, DIL_HEAD_DIM), lambda h, i: (i, COL_DQ + h)),
            pl.BlockSpec((s, DIL_HEAD_DIM), lambda h, i: (0, COL_DK + h)),
            pl.BlockSpec((s, DIL_HEAD_DIM), lambda h, i: (0, COL_DV + h)),
            pl.BlockSpec((len(DIL_DILATIONS), None, DIL_TQ, DIL_TK), lambda h, i: (0, h, 0, 0)),
            pl.BlockSpec((None, 1, DIL_HEAD_DIM), lambda h, i: (layer, 0, h)),
        ],
        out_specs=pl.BlockSpec((tb, DIL_HEAD_DIM), lambda h, i: (i, h)),
        out_shape=jax.ShapeDtypeStruct((s, DIL_WIDTH), BF16),
        scratch_shapes=[tile_f32] + [guarded] * 6 + [tile_bf16] * 2 + [tile_f32] * 6,
        compiler_params=_params(2),
        name="dil_attn",
    )(proj, proj, proj, bias, gain)


def _t5_bucket(rel):
    half = REL_BUCKETS // 2
    max_exact = half // 2
    ret = jnp.where(rel > 0, half, 0)
    n = jnp.abs(rel)
    nf = jnp.maximum(n, 1).astype(F32)
    large = max_exact + (jnp.log(nf / max_exact) / math.log(REL_MAX_DISTANCE / max_exact)
                         * (half - max_exact)).astype(jnp.int32)
    large = jnp.minimum(large, half - 1)
    return ret + jnp.where(n < max_exact, n, large)


def _dil_bias_tiles(rel_bias):
    rel = jnp.arange(DIL_TK)[None, :] - DIL_BAND - jnp.arange(DIL_TQ)[:, None]
    band = jnp.abs(rel) <= DIL_BAND
    tiles = []
    for d in DIL_DILATIONS:
        onehot = (_t5_bucket(rel * d)[..., None] == jnp.arange(REL_BUCKETS)).astype(F32)
        bias = jnp.einsum("qkb,bh->hqk", onehot, rel_bias.astype(F32), precision=lax.Precision.HIGHEST)
        tiles.append(jnp.where(band[None], bias, NEG_INF))
    return jnp.stack(tiles, axis=0)


def _mem_kv_kernel(mem_ref, g_ref, w_ref, o_ref):
    h = _rms(mem_ref[...], g_ref[...]).astype(BF16)
    o_ref[...] = _dot(h, w_ref[...]).astype(o_ref.dtype)


def _mem_kv(mem, gain, w):
    m, d = mem.shape
    n = w.shape[-1]
    return pl.pallas_call(
        _mem_kv_kernel,
        grid=(DEPTH,),
        in_specs=[
            pl.BlockSpec((m, d), lambda l: (0, 0)),
            pl.BlockSpec((None, 1, d), lambda l: (l, 0, 0)),
            pl.BlockSpec((None, d, n), lambda l: (l, 0, 0)),
        ],
        out_specs=pl.BlockSpec((None, m, n), lambda l: (l, 0, 0)),
        out_shape=jax.ShapeDtypeStruct((DEPTH, m, n), BF16),
        compiler_params=_params(1),
        name="mem_kv",
    )(mem, gain, w)


def _head_rms(o, gain):
    return o * lax.rsqrt(jnp.mean(o * o, axis=-1, keepdims=True) + EPS) * gain


def _mix_kernel(of_ref, ob_ref, gr_ref, dil_ref, mq_ref, km_ref, vm_ref, ggla_ref, gmem_ref, w_ref, x_ref,
                out_ref, gla_ref, mem_ref):
    dil0 = GLA_V_WIDTH
    mem0 = GLA_V_WIDTH + DIL_WIDTH
    out_ref[...] = x_ref[...] + _dot(dil_ref[...], w_ref[dil0:mem0, :])

    for h in range(GLA_HEADS):
        sl = slice(h * GLA_DV, (h + 1) * GLA_DV)
        o = of_ref[:, sl].astype(F32) + ob_ref[:, sl].astype(F32)
        r = gr_ref[:, sl].astype(F32)
        gate = r * (1.0 / (1.0 + jnp.exp(-r)))
        gla_ref[:, sl] = (_head_rms(o, ggla_ref[:, sl]) * gate).astype(gla_ref.dtype)
    out_ref[...] += _dot(gla_ref[...], w_ref[0:dil0, :])

    scale = MEM_HEAD_DIM ** -0.5
    for h in range(MEM_HEADS):
        sl = slice(h * MEM_HEAD_DIM, (h + 1) * MEM_HEAD_DIM)
        sc = _dot_nt(mq_ref[:, sl], km_ref[:, sl]) * scale
        m = jnp.max(sc, axis=-1, keepdims=True)
        p = jnp.exp(sc - m)
        den = jnp.sum(p, axis=-1, keepdims=True)
        o = _dot(p.astype(BF16), vm_ref[:, sl]) / den
        mem_ref[:, sl] = _head_rms(o, gmem_ref[:, sl]).astype(mem_ref.dtype)
    out_ref[...] += _dot(mem_ref[...], w_ref[mem0:, :])


def _mix_out(x, proj, o_f, o_b, dil, kv, g_gla, g_mem, w_out, layer, t=512):
    s, d = x.shape
    rows = lambda w, c=0: pl.BlockSpec((t, w), lambda i: (i, c))
    per_layer = lambda shape, c=0: pl.BlockSpec((None,) + shape, lambda i: (layer, 0, c),
                                                pipeline_mode=pl.Buffered(1))
    return pl.pallas_call(
        _mix_kernel,
        grid=(s // t,),
        in_specs=[
            rows(GLA_V_WIDTH), rows(GLA_V_WIDTH),
            rows(GLA_V_WIDTH, COL_GR * LANES // GLA_V_WIDTH),
            rows(DIL_WIDTH),
            rows(MEM_WIDTH, COL_MQ * LANES // MEM_WIDTH),
            per_layer((MEM_LEN, MEM_WIDTH), 0), per_layer((MEM_LEN, MEM_WIDTH), 1),
            per_layer((1, GLA_V_WIDTH)), per_layer((1, MEM_WIDTH)),
            per_layer((MIX_WIDTH, d)),
            rows(d),
        ],
        out_specs=rows(d),
        out_shape=jax.ShapeDtypeStruct((s, d), F32),
        scratch_shapes=[pltpu.VMEM((t, GLA_V_WIDTH), BF16), pltpu.VMEM((t, MEM_WIDTH), BF16)],
        compiler_params=_params(1),
        name="mix_out_proj",
    )(o_f, o_b, proj, dil, proj, kv, kv, g_gla, g_mem, w_out, x)


def _mlp_kernel(x_ref, g_ref, wu_ref, wd_ref, gf_ref, o_ref, h_ref, *, final_norm, n_sub):
    f = pl.program_id(1)
    sub = x_ref.shape[0] // n_sub

    def hidden(h):
        return jnp.square(jnp.maximum(_dot(h, wu_ref[...]), 0.0)).astype(BF16)

    @pl.when(f == 0)
    def _():
        for c in range(n_sub):
            rows = slice(c * sub, (c + 1) * sub)
            x = x_ref[rows, :]
            h = _rms(x, g_ref[...]).astype(h_ref.dtype)
            h_ref[rows, :] = h
            o_ref[rows, :] = x + _dot(hidden(h), wd_ref[...])

    @pl.when(f != 0)
    def _():
        o_ref[...] += _dot(hidden(h_ref[...]), wd_ref[...])

    if final_norm:
        @pl.when(f == pl.num_programs(1) - 1)
        def _():
            o_ref[...] = _rms(o_ref[...], gf_ref[...])


def _mlp(x, gain, w_up, w_down, gain_final, layer, final_norm, tm=512, tf=1024, n_sub=2):
    s, d = x.shape
    ff = w_up.shape[-1]
    return pl.pallas_call(
        functools.partial(_mlp_kernel, final_norm=final_norm, n_sub=n_sub),
        grid=(s // tm, ff // tf),
        in_specs=[
            pl.BlockSpec((tm, d), lambda i, f: (i, 0)),
            pl.BlockSpec((None, 1, d), lambda i, f: (layer, 0, 0)),
            pl.BlockSpec((d, tf), lambda i, f: (0, f)),
            pl.BlockSpec((tf, d), lambda i, f: (f, 0)),
            pl.BlockSpec((1, d), lambda i, f: (0, 0)),
        ],
        out_specs=pl.BlockSpec((tm, d), lambda i, f: (i, 0)),
        out_shape=jax.ShapeDtypeStruct((s, d), F32),
        scratch_shapes=[pltpu.VMEM((tm, d), BF16)],
        compiler_params=_params(2),
        name="mlp",
    )(x, gain, w_up, w_down, gain_final)


def _w_in_prep_kernel(w_ref, o_ref):
    lr0 = 2 * GLA_QK_WIDTH + 2 * GLA_V_WIDTH
    lr_w = 2 * GLA_GATE_RANK
    tail = w_ref.shape[0] - lr0 - lr_w
    o_ref[0:lr0, :] = w_ref[0:lr0, :].astype(o_ref.dtype)
    o_ref[lr0:lr0 + tail, :] = w_ref[lr0 + lr_w:, :].astype(o_ref.dtype)
    o_ref[lr0 + tail:lr0 + tail + lr_w, :] = w_ref[lr0:lr0 + lr_w, :].astype(o_ref.dtype)
    o_ref[lr0 + tail + lr_w:, :] = jnp.zeros((o_ref.shape[0] - lr0 - tail - lr_w, o_ref.shape[1]), o_ref.dtype)


def _reorder_w_in(w_in, cols=256):
    depth, d, n = w_in.shape
    assert COL_LR * LANES == n - 2 * GLA_GATE_RANK
    return pl.pallas_call(
        _w_in_prep_kernel,
        grid=(depth, d // cols),
        in_specs=[pl.BlockSpec((None, n, cols), lambda l, i: (l, 0, i))],
        out_specs=pl.BlockSpec((None, PROJ_WIDTH, cols), lambda l, i: (l, 0, i)),
        out_shape=jax.ShapeDtypeStruct((depth, PROJ_WIDTH, d), BF16),
        compiler_params=_params(2),
        name="w_in_prep",
    )(jnp.swapaxes(w_in, 1, 2))


def _pad_gate_up(up, row0):
    out = jnp.zeros((up.shape[0], LANES, up.shape[-1]), BF16)
    return out.at[:, row0:row0 + GLA_GATE_RANK, :].set(up.astype(BF16))


def kernel(x, mem, norm_mix, w_in, gla_gate_up_fwd, gla_gate_bias_fwd, gla_gate_up_bwd, gla_gate_bias_bwd, gla_norm, rel_bias, dil_norm, mem_norm, w_mem_kv, mem_out_norm, w_out, norm_mlp, w_up, w_down, norm_final):
    batch, seq, d = x.shape
    assert batch == 1 and seq == SEQ and d == D_MODEL
    xs = x.reshape(seq, d)
    row = lambda g: g.reshape(g.shape[0], 1, g.shape[-1])

    w_in_r = _reorder_w_in(w_in)
    up_f = _pad_gate_up(gla_gate_up_fwd, 0)
    up_b = _pad_gate_up(gla_gate_up_bwd, GLA_GATE_RANK)
    w_out_b = w_out.astype(BF16)
    dil_bias = _dil_bias_tiles(rel_bias)
    kv = _mem_kv(mem.reshape(MEM_LEN, d), row(mem_norm), w_mem_kv.astype(BF16))

    for layer in range(DEPTH):
        proj = _in_proj(xs, row(norm_mix), w_in_r, layer)
        o_f, o_b, w_up_b, w_down_b = _gla(proj, up_f, up_b, row(gla_gate_bias_fwd), row(gla_gate_bias_bwd),
                                          w_up, w_down, layer)
        dil = _dil_attention(proj, dil_bias, row(dil_norm), layer)
        xs = _mix_out(xs, proj, o_f, o_b, dil, kv, row(gla_norm), row(mem_out_norm), w_out_b, layer)
        xs = _mlp(xs, row(norm_mlp), w_up_b, w_down_b, norm_final.reshape(1, d), layer,
                  final_norm=(layer == DEPTH - 1))
    return xs.reshape(batch, seq, d)
```

```python
import functools
import math

import jax
import jax.numpy as jnp
from jax import lax
from jax.experimental import pallas as pl
from jax.experimental.pallas import tpu as pltpu

F32 = jnp.float32
BF16 = jnp.bfloat16

D_MODEL = 2048
SEQ = 8192
DEPTH = 4
MEM_LEN = 256
GLA_HEADS = 4
GLA_DK = 128
GLA_DV = 256
GLA_GATE_RANK = 16
GLA_GATE_NORMALIZER = 16.0
GLA_CHUNK = 64
DIL_HEADS = 4
DIL_HEAD_DIM = 128
DIL_CONFIGS = ((128, 1), (512, 4), (2048, 16))
MEM_HEADS = 4
MEM_HEAD_DIM = 128
REL_BUCKETS = 32
REL_MAX_DISTANCE = 1024
D_FF = 4 * D_MODEL
EPS = 1e-6
NEG_INF = -1e30

GLA_QK_WIDTH = GLA_HEADS * GLA_DK
GLA_V_WIDTH = GLA_HEADS * GLA_DV
DIL_WIDTH = DIL_HEADS * DIL_HEAD_DIM
MEM_WIDTH = MEM_HEADS * MEM_HEAD_DIM
MIX_WIDTH = GLA_V_WIDTH + DIL_WIDTH + MEM_WIDTH

LANES = 128

COL_GQ = 0
COL_GK = COL_GQ + GLA_QK_WIDTH // LANES
COL_GV = COL_GK + GLA_QK_WIDTH // LANES
COL_GR = COL_GV + GLA_V_WIDTH // LANES
COL_DQ = COL_GR + GLA_V_WIDTH // LANES
COL_DK = COL_DQ + DIL_WIDTH // LANES
COL_DV = COL_DK + DIL_WIDTH // LANES
COL_MQ = COL_DV + DIL_WIDTH // LANES
COL_LR = COL_MQ + MEM_WIDTH // LANES
PROJ_COLS = COL_LR + 2
PROJ_WIDTH = PROJ_COLS * LANES

GLA_BLOCK = 256

DIL_BAND = 64
DIL_TQ = 128
DIL_TK = DIL_TQ + 2 * DIL_BAND
DIL_DILATIONS = tuple(d for _, d in DIL_CONFIGS)
DIL_TB = DIL_TQ * max(DIL_DILATIONS)
DIL_INTERLEAVE = 8

VMEM_LIMIT = 48 * 1024 * 1024


def _params(n_axes, vmem=VMEM_LIMIT):
    return pltpu.CompilerParams(dimension_semantics=("arbitrary",) * n_axes, vmem_limit_bytes=vmem)


def _dot(a, b):
    return jnp.dot(a, b, preferred_element_type=F32)


def _dot_nt(a, b):
    return lax.dot_general(a, b, (((1,), (1,)), ((), ())), preferred_element_type=F32)


def _dot_tn(a, b):
    return lax.dot_general(a, b, (((0,), (0,)), ((), ())), preferred_element_type=F32)


def _rms(x, gain):
    return x * lax.rsqrt(jnp.mean(x * x, axis=-1, keepdims=True) + EPS) * gain


def _run_interleaved(stage_generators):
    live = list(stage_generators)
    while live:
        live = [gen for gen in live if next(gen, StopIteration) is not StopIteration]


def _in_proj_kernel(x_ref, g_ref, w_ref, o_ref, h_ref, *, n_sub):
    j = pl.program_id(1)
    sub = x_ref.shape[0] // n_sub

    @pl.when(j == 0)
    def _():
        for c in range(n_sub):
            rows = slice(c * sub, (c + 1) * sub)
            h = _rms(x_ref[rows, :], g_ref[...]).astype(h_ref.dtype)
            h_ref[rows, :] = h
            o_ref[rows, :] = _dot_nt(h, w_ref[...]).astype(o_ref.dtype)

    @pl.when(j != 0)
    def _():
        o_ref[...] = _dot_nt(h_ref[...], w_ref[...]).astype(o_ref.dtype)


def _in_proj(x, gain, w, layer, tm=1024, tn=768, n_sub=4):
    s, d = x.shape
    n = w.shape[1]
    return pl.pallas_call(
        functools.partial(_in_proj_kernel, n_sub=n_sub),
        grid=(s // tm, n // tn),
        in_specs=[
            pl.BlockSpec((tm, d), lambda i, j: (i, 0)),
            pl.BlockSpec((None, 1, d), lambda i, j: (layer, 0, 0)),
            pl.BlockSpec((None, tn, d), lambda i, j: (layer, j, 0)),
        ],
        out_specs=pl.BlockSpec((tm, tn), lambda i, j: (i, j)),
        out_shape=jax.ShapeDtypeStruct((s, n), BF16),
        scratch_shapes=[pltpu.VMEM((tm, d), BF16)],
        compiler_params=_params(2),
        name="in_proj",
    )(x, gain, w)


def _log_sigmoid(x):
    return jnp.minimum(x, 0.0) - jnp.log(1.0 + jnp.exp(-jnp.abs(x)))


GLA_LEVELS = (GLA_BLOCK // GLA_CHUNK).bit_length() - 1


def _gla_tables(tri_ref, lvl_ref, reverse):
    t = GLA_BLOCK
    row = lax.broadcasted_iota(jnp.int32, (t, t), 0)
    col = lax.broadcasted_iota(jnp.int32, (t, t), 1)
    valid = (col >= row) if reverse else (col <= row)
    tri_ref[...] = jnp.where(valid, 1.0, 0.0).astype(tri_ref.dtype)
    shift = GLA_CHUNK.bit_length() - 1
    lvl_ref[...] = jnp.where(jnp.right_shift(row, shift) == jnp.right_shift(col, shift),
                             jnp.where(valid, 0, -1), -1)


def _gla_boundary(cum, group, reverse):
    t = cum.shape[0]
    pieces = []
    for a in range(0, t, group):
        idx = a + group if reverse else a - 1
        if idx < 0 or idx >= t:
            ref_row = jnp.zeros((1, cum.shape[1]), cum.dtype)
        else:
            ref_row = cum[idx:idx + 1]
        pieces.append(jnp.broadcast_to(ref_row, (group, cum.shape[1])))
    return jnp.concatenate(pieces, axis=0)


def _place_rows(part, row0, total_rows):
    pieces = []
    if row0 > 0:
        pieces.append(jnp.zeros((row0, part.shape[1]), part.dtype))
    pieces.append(part)
    rest = total_rows - row0 - part.shape[0]
    if rest > 0:
        pieces.append(jnp.zeros((rest, part.shape[1]), part.dtype))
    return jnp.concatenate(pieces, axis=0)


def _gla_block(q_ref, k_ref, v_ref, lr_ref, up_ref, bias_ref, o_ref, s_ref, tri_ref, lvl_ref, r0, reverse):
    t = GLA_BLOCK
    rows = slice(r0, r0 + t)
    logits = _dot(lr_ref[rows, :], up_ref[...]) + bias_ref[...]
    yield
    g = _log_sigmoid(logits) * (1.0 / GLA_GATE_NORMALIZER)
    g_hi = g.astype(BF16)
    g_lo = (g - g_hi.astype(F32)).astype(BF16)
    cum2 = _dot(tri_ref[...], jnp.concatenate([g_hi, g_lo], axis=1))
    yield
    cum = cum2[:, :GLA_DK] + cum2[:, GLA_DK:]

    q = q_ref[rows, :].astype(F32) * (GLA_DK ** -0.5)
    k = k_ref[rows, :].astype(F32)
    v = v_ref[rows, :]
    lvl = lvl_ref[...]

    ref0 = _gla_boundary(cum, GLA_CHUNK, reverse)
    q_dec = (q * jnp.exp(cum - ref0)).astype(BF16)
    k_inv = (k * jnp.exp(ref0 - cum)).astype(BF16)
    q_segs, k_segs = [], []
    for level in range(1, GLA_LEVELS + 1):
        group = GLA_CHUNK << level
        for a0 in range(0, t, group):
            mid = a0 + group // 2
            att = slice(a0, mid) if reverse else slice(mid, a0 + group)
            src = slice(mid, a0 + group) if reverse else slice(a0, mid)
            ref_row = cum[mid:mid + 1] if reverse else cum[mid - 1:mid]
            q_part = (q[att] * jnp.exp(cum[att] - ref_row)).astype(BF16)
            k_part = (k[src] * jnp.exp(ref_row - cum[src])).astype(BF16)
            q_segs.append(_place_rows(q_part, att.start, t))
            k_segs.append(_place_rows(k_part, src.start, t))
    cross = _dot_nt(jnp.concatenate(q_segs, axis=1), jnp.concatenate(k_segs, axis=1))
    diag = _dot_nt(q_dec, k_inv)
    total = cum[0:1] if reverse else cum[t - 1:t]
    q_in = (q * jnp.exp(cum)).astype(BF16)
    k_out = (k * jnp.exp(total - cum)).astype(BF16)
    carry_in = _dot_tn(v, k_out)
    yield
    a = jnp.where(lvl == 0, diag, cross)
    o_local = _dot(a.astype(BF16), v)
    yield
    state = s_ref[...]
    o = o_local + _dot_nt(q_in, state.astype(BF16))
    o_ref[rows, :] = o.astype(o_ref.dtype)
    s_ref[...] = state * jnp.exp(total) + carry_in


def _gla_kernel(qf, kf, vf, lrf, qb, kb, vb, lrb, upf, upb, bsf, bsb, wu_ref, wd_ref, of, ob, wu_out, wd_out,
                sf, sb, tri_f, tri_b, lvl_f, lvl_b, *, interleave):
    @pl.when(pl.program_id(1) == 0)
    def _():
        sf[...] = jnp.zeros_like(sf)
        sb[...] = jnp.zeros_like(sb)
        _gla_tables(tri_f, lvl_f, reverse=False)
        _gla_tables(tri_b, lvl_b, reverse=True)

    starts = list(range(0, qf.shape[0], GLA_BLOCK))
    n_groups = len(starts) // interleave
    cast_jobs = [(wu_ref, wu_out), (wd_ref, wd_out)]

    def cast_piece(piece):
        for src, dst in cast_jobs:
            n = src.shape[0] // n_groups
            dst[piece * n:(piece + 1) * n, :] = src[piece * n:(piece + 1) * n, :].astype(dst.dtype)

    for gi in range(n_groups):
        blocks = []
        for r_fwd, r_bwd in list(zip(starts, reversed(starts)))[gi * interleave:(gi + 1) * interleave]:
            blocks.append(_gla_block(qf, kf, vf, lrf, upf, bsf, of, sf, tri_f, lvl_f, r_fwd, reverse=False))
            blocks.append(_gla_block(qb, kb, vb, lrb, upb, bsb, ob, sb, tri_b, lvl_b, r_bwd, reverse=True))
        _run_interleaved(blocks)
        cast_piece(gi)


def _gla(proj, up_f, up_b, bias_f, bias_b, w_up, w_down, layer, t=1024, interleave=4):
    s = proj.shape[0]
    nb = s // t
    vcol = COL_GV * LANES // GLA_DV
    n_steps = GLA_HEADS * nb
    d, ff = w_up.shape[1:]
    slab = lambda rows, cols: (
        pl.BlockSpec((None, rows // n_steps, cols), lambda h, i: (layer, h * nb + i, 0)),
        pl.BlockSpec((rows // n_steps, cols), lambda h, i: (h * nb + i, 0)),
        jax.ShapeDtypeStruct((rows, cols), BF16))
    wu_in, wu_out, wu_sds = slab(d, ff)
    wd_in, wd_out, wd_sds = slab(ff, d)

    def row_specs(rmap):
        return [
            pl.BlockSpec((t, GLA_DK), lambda h, i: (rmap(i), COL_GQ + h)),
            pl.BlockSpec((t, GLA_DK), lambda h, i: (rmap(i), COL_GK + h)),
            pl.BlockSpec((t, GLA_DV), lambda h, i: (rmap(i), vcol + h)),
            pl.BlockSpec((t, LANES), lambda h, i: (rmap(i), COL_LR)),
        ]

    fwd = lambda i: i
    bwd = lambda i: nb - 1 - i
    up_spec = pl.BlockSpec((None, LANES, GLA_DK), lambda h, i: (layer, 0, h))
    bias_spec = pl.BlockSpec((None, 1, GLA_DK), lambda h, i: (layer, 0, h))
    out_sds = jax.ShapeDtypeStruct((s, GLA_V_WIDTH), BF16)
    state = pltpu.VMEM((GLA_DV, GLA_DK), F32)
    tri = pltpu.VMEM((GLA_BLOCK, GLA_BLOCK), BF16)
    lvl = pltpu.VMEM((GLA_BLOCK, GLA_BLOCK), jnp.int32)
    return pl.pallas_call(
        functools.partial(_gla_kernel, interleave=interleave),
        grid=(GLA_HEADS, nb),
        in_specs=row_specs(fwd) + row_specs(bwd) + [up_spec, up_spec, bias_spec, bias_spec, wu_in, wd_in],
        out_specs=[
            pl.BlockSpec((t, GLA_DV), lambda h, i: (fwd(i), h)),
            pl.BlockSpec((t, GLA_DV), lambda h, i: (bwd(i), h)),
            wu_out, wd_out,
        ],
        out_shape=[out_sds, out_sds, wu_sds, wd_sds],
        scratch_shapes=[state, state, tri, tri, lvl, lvl],
        compiler_params=_params(2),
        name="gla_scan",
    )(proj, proj, proj, proj, proj, proj, proj, proj, up_f, up_b, bias_f, bias_b, w_up, w_down)


def _dil_tile(q, k, v, bias, edge, o_dst, l_dst, rows):
    raw = _dot_nt(q, k)
    yield
    sc = raw * (DIL_HEAD_DIM ** -0.5) + bias
    if edge is not None:
        sc = sc + edge
    m = jnp.max(sc, axis=-1, keepdims=True)
    p = jnp.exp(sc - m)
    den = jnp.sum(p, axis=-1, keepdims=True)
    pv = _dot(p.astype(BF16), v)
    yield
    o = pv / den
    o_dst[rows, :] = o
    l_dst[rows, :] = jnp.broadcast_to(m + jnp.log(den), o.shape)


def _dil_kernel(q_ref, k_ref, v_ref, bias_ref, gain_ref, wo_ref, out_ref, wo_out,
                stage, stage_mid, kg1, vg1, kg4, vg4, kg16, vg16, qp4, qp16, o1, o4, o16, l1, l4, l16):
    i = pl.program_id(1)
    s = k_ref.shape[0]
    tb = q_ref.shape[0]
    k_bufs = dict(zip(DIL_DILATIONS, (kg1, kg4, kg16)))
    v_bufs = dict(zip(DIL_DILATIONS, (vg1, vg4, vg16)))
    q_perm = dict(zip(DIL_DILATIONS, (None, qp4, qp16)))
    o_bufs = dict(zip(DIL_DILATIONS, (o1, o4, o16)))
    l_bufs = dict(zip(DIL_DILATIONS, (l1, l4, l16)))
    _, d_mid, d_max = DIL_DILATIONS
    assert d_max == d_mid * d_mid

    def deinterleave(dst_mid, dst_max, base_mid, base_max, residue_rows):
        per_mid, per_max = tb // d_mid, tb // d_max

        def at(base, offset):
            start = base + offset
            return start if isinstance(start, int) else pl.multiple_of(start, DIL_BAND)

        for r in range(d_mid):
            part = stage[pl.ds(r, per_mid, stride=d_mid), :]
            stage_mid[r * per_mid:(r + 1) * per_mid, :] = part
            dst_mid[pl.ds(at(base_mid, r * (residue_rows // d_mid)), per_mid), :] = part.astype(dst_mid.dtype)
        for r in range(d_max):
            r_mid, m = r % d_mid, r // d_mid
            part = stage_mid[pl.ds(r_mid * per_mid + m, per_max, stride=d_mid), :]
            dst_max[pl.ds(at(base_max, r * (residue_rows // d_max)), per_max), :] = part.astype(dst_max.dtype)

    @pl.when(i == 0)
    def _():
        zeros = jnp.zeros((DIL_BAND, DIL_HEAD_DIM), BF16)
        for buf in (kg1, vg1, kg4, vg4, kg16, vg16):
            buf[0:DIL_BAND, :] = zeros
            buf[DIL_BAND + s:, :] = zeros
        kg1[DIL_BAND:DIL_BAND + s, :] = k_ref[...]
        vg1[DIL_BAND:DIL_BAND + s, :] = v_ref[...]

        def body(c, carry):
            base = pl.multiple_of(c * tb, tb)
            for src, bufs in ((k_ref, k_bufs), (v_ref, v_bufs)):
                stage[...] = src[pl.ds(base, tb), :].astype(F32)
                deinterleave(bufs[d_mid], bufs[d_max], DIL_BAND + c * (tb // d_mid),
                             DIL_BAND + c * (tb // d_max), s)
            return carry

        lax.fori_loop(0, s // tb, body, 0)

    stage[...] = q_ref[...].astype(F32)
    deinterleave(q_perm[d_mid], q_perm[d_max], 0, 0, tb)

    lane = lax.broadcasted_iota(jnp.int32, (1, DIL_TK), 1)
    tiles = []
    for b, d in enumerate(DIL_DILATIONS):
        sub_len = s // d
        per = tb // d
        n_tiles = per // DIL_TQ
        for r in range(d):
            for tj in range(n_tiles):
                tiles.append((b, d, sub_len, per, n_tiles, r, tj))

    def tile_stages(b, d, sub_len, per, n_tiles, r, tj):
        u0 = i * per + tj * DIL_TQ
        q_src = q_ref if d == 1 else q_perm[d]
        q = q_src[r * per + tj * DIL_TQ:r * per + (tj + 1) * DIL_TQ, :]
        k_start = pl.multiple_of(r * sub_len + u0, DIL_TQ)
        k = k_bufs[d][pl.ds(k_start, DIL_TK), :]
        v = v_bufs[d][pl.ds(k_start, DIL_TK), :]
        edge = None
        if tj == 0 or tj == n_tiles - 1:
            key_pos = u0 - DIL_BAND + lane
            edge = jnp.where((key_pos >= 0) & (key_pos < sub_len), 0.0, NEG_INF)
        if d == 1:
            rows = pl.ds(tj * DIL_TQ, DIL_TQ)
        else:
            rows = pl.ds(r + d * tj * DIL_TQ, DIL_TQ, stride=d)
        return _dil_tile(q, k, v, bias_ref[b], edge, o_bufs[d], l_bufs[d], rows)

    for g0 in range(0, len(tiles), DIL_INTERLEAVE):
        _run_interleaved([tile_stages(*tile) for tile in tiles[g0:g0 + DIL_INTERLEAVE]])

    lses = [l_bufs[d][...] for d in DIL_DILATIONS]
    m = functools.reduce(jnp.maximum, lses)
    es = [jnp.exp(l - m) for l in lses]
    num = sum(e * o_bufs[d][...] for e, d in zip(es, DIL_DILATIONS))
    mixed = num / sum(es)
    out_ref[...] = _head_rms(mixed, gain_ref[...]).astype(out_ref.dtype)

    wo_out[...] = wo_ref[...].astype(wo_out.dtype)


def _dil_attention(proj, bias, gain, w_out, layer):
    s = proj.shape[0]
    tb = DIL_TB
    nq = s // tb
    n_steps = DIL_HEADS * nq
    mix_w, d = w_out.shape[1:]
    guarded = pltpu.VMEM((s + 2 * DIL_BAND, DIL_HEAD_DIM), BF16)
    tile_f32 = pltpu.VMEM((tb, DIL_HEAD_DIM), F32)
    tile_bf16 = pltpu.VMEM((tb, DIL_HEAD_DIM), BF16)
    return pl.pallas_call(
        _dil_kernel,
        grid=(DIL_HEADS, nq),
        in_specs=[
            pl.BlockSpec((tb, DIL_HEAD_DIM), lambda h, i: (i, COL_DQ + h)),
            pl.BlockSpec((s, DIL_HEAD_DIM), lambda h, i: (0, COL_DK + h)),
            pl.BlockSpec((s, DIL_HEAD_DIM), lambda h, i: (0, COL_DV + h)),
            pl.BlockSpec((len(DIL_DILATIONS), None, DIL_TQ, DIL_TK), lambda h, i: (0, h, 0, 0)),
            pl.BlockSpec((None, 1, DIL_HEAD_DIM), lambda h, i: (layer, 0, h)),
            pl.BlockSpec((None, mix_w // n_steps, d), lambda h, i: (layer, h * nq + i, 0)),
        ],
        out_specs=[pl.BlockSpec((tb, DIL_HEAD_DIM), lambda h, i: (i, h)),
                   pl.BlockSpec((mix_w // n_steps, d), lambda h, i: (h * nq + i, 0))],
        out_shape=[jax.ShapeDtypeStruct((s, DIL_WIDTH), BF16), jax.ShapeDtypeStruct((mix_w, d), BF16)],
        scratch_shapes=[tile_f32] * 2 + [guarded] * 6 + [tile_bf16] * 2 + [tile_f32] * 6,
        compiler_params=_params(2),
        name="dil_attn",
    )(proj, proj, proj, bias, gain, w_out)


def _t5_bucket(rel):
    half = REL_BUCKETS // 2
    max_exact = half // 2
    ret = jnp.where(rel > 0, half, 0)
    n = jnp.abs(rel)
    nf = jnp.maximum(n, 1).astype(F32)
    large = max_exact + (jnp.log(nf / max_exact) / math.log(REL_MAX_DISTANCE / max_exact)
                         * (half - max_exact)).astype(jnp.int32)
    large = jnp.minimum(large, half - 1)
    return ret + jnp.where(n < max_exact, n, large)


def _dil_bias_tiles(rel_bias):
    rel = jnp.arange(DIL_TK)[None, :] - DIL_BAND - jnp.arange(DIL_TQ)[:, None]
    band = jnp.abs(rel) <= DIL_BAND
    tiles = []
    for d in DIL_DILATIONS:
        onehot = (_t5_bucket(rel * d)[..., None] == jnp.arange(REL_BUCKETS)).astype(F32)
        bias = jnp.einsum("qkb,bh->hqk", onehot, rel_bias.astype(F32), precision=lax.Precision.HIGHEST)
        tiles.append(jnp.where(band[None], bias, NEG_INF))
    return jnp.stack(tiles, axis=0)


def _mem_kv_kernel(mem_ref, g_ref, w_ref, o_ref):
    h = _rms(mem_ref[...], g_ref[...]).astype(BF16)
    o_ref[...] = _dot(h, w_ref[...].astype(BF16)).astype(o_ref.dtype)


def _mem_kv(mem, gain, w):
    m, d = mem.shape
    n = w.shape[-1]
    return pl.pallas_call(
        _mem_kv_kernel,
        grid=(DEPTH,),
        in_specs=[
            pl.BlockSpec((m, d), lambda l: (0, 0)),
            pl.BlockSpec((None, 1, d), lambda l: (l, 0, 0)),
            pl.BlockSpec((None, d, n), lambda l: (l, 0, 0)),
        ],
        out_specs=pl.BlockSpec((None, m, n), lambda l: (l, 0, 0)),
        out_shape=jax.ShapeDtypeStruct((DEPTH, m, n), BF16),
        compiler_params=_params(1),
        name="mem_kv",
    )(mem, gain, w)


def _head_rms(o, gain):
    return o * lax.rsqrt(jnp.mean(o * o, axis=-1, keepdims=True) + EPS) * gain


def _mix_kernel(of_ref, ob_ref, gr_ref, dil_ref, mq_ref, km_ref, vm_ref, ggla_ref, gmem_ref, w_ref, x_ref,
                out_ref, gla_ref, mem_ref):
    dil0 = GLA_V_WIDTH
    mem0 = GLA_V_WIDTH + DIL_WIDTH
    out_ref[...] = x_ref[...] + _dot(dil_ref[...], w_ref[dil0:mem0, :])

    for h in range(GLA_HEADS):
        sl = slice(h * GLA_DV, (h + 1) * GLA_DV)
        o = of_ref[:, sl].astype(F32) + ob_ref[:, sl].astype(F32)
        r = gr_ref[:, sl].astype(F32)
        gate = r * (1.0 / (1.0 + jnp.exp(-r)))
        gla_ref[:, sl] = (_head_rms(o, ggla_ref[:, sl]) * gate).astype(gla_ref.dtype)
    out_ref[...] += _dot(gla_ref[...], w_ref[0:dil0, :])

    scale = MEM_HEAD_DIM ** -0.5
    for h in range(MEM_HEADS):
        sl = slice(h * MEM_HEAD_DIM, (h + 1) * MEM_HEAD_DIM)
        sc = _dot_nt(mq_ref[:, sl], km_ref[:, sl]) * scale
        m = jnp.max(sc, axis=-1, keepdims=True)
        p = jnp.exp(sc - m)
        den = jnp.sum(p, axis=-1, keepdims=True)
        o = _dot(p.astype(BF16), vm_ref[:, sl]) / den
        mem_ref[:, sl] = _head_rms(o, gmem_ref[:, sl]).astype(mem_ref.dtype)
    out_ref[...] += _dot(mem_ref[...], w_ref[mem0:, :])


def _mix_out(x, proj, o_f, o_b, dil, kv, g_gla, g_mem, w_out, layer, t=512):
    s, d = x.shape
    rows = lambda w, c=0: pl.BlockSpec((t, w), lambda i: (i, c))
    per_layer = lambda shape, c=0: pl.BlockSpec((None,) + shape, lambda i: (layer, 0, c),
                                                pipeline_mode=pl.Buffered(1))
    return pl.pallas_call(
        _mix_kernel,
        grid=(s // t,),
        in_specs=[
            rows(GLA_V_WIDTH), rows(GLA_V_WIDTH),
            rows(GLA_V_WIDTH, COL_GR * LANES // GLA_V_WIDTH),
            rows(DIL_WIDTH),
            rows(MEM_WIDTH, COL_MQ * LANES // MEM_WIDTH),
            per_layer((MEM_LEN, MEM_WIDTH), 0), per_layer((MEM_LEN, MEM_WIDTH), 1),
            per_layer((1, GLA_V_WIDTH)), per_layer((1, MEM_WIDTH)),
            pl.BlockSpec((MIX_WIDTH, d), lambda i: (0, 0), pipeline_mode=pl.Buffered(1)),
            rows(d),
        ],
        out_specs=rows(d),
        out_shape=jax.ShapeDtypeStruct((s, d), F32),
        scratch_shapes=[pltpu.VMEM((t, GLA_V_WIDTH), BF16), pltpu.VMEM((t, MEM_WIDTH), BF16)],
        compiler_params=_params(1),
        name="mix_out_proj",
    )(o_f, o_b, proj, dil, proj, kv, kv, g_gla, g_mem, w_out, x)


def _mlp_kernel(x_ref, g_ref, wu_ref, wd_ref, gf_ref, o_ref, h_ref, *, final_norm, n_sub):
    f = pl.program_id(1)
    sub = x_ref.shape[0] // n_sub

    def hidden(h):
        return jnp.square(jnp.maximum(_dot(h, wu_ref[...]), 0.0)).astype(BF16)

    @pl.when(f == 0)
    def _():
        for c in range(n_sub):
            rows = slice(c * sub, (c + 1) * sub)
            x = x_ref[rows, :]
            h = _rms(x, g_ref[...]).astype(h_ref.dtype)
            h_ref[rows, :] = h
            o_ref[rows, :] = x + _dot(hidden(h), wd_ref[...])

    @pl.when(f != 0)
    def _():
        o_ref[...] += _dot(hidden(h_ref[...]), wd_ref[...])

    if final_norm:
        @pl.when(f == pl.num_programs(1) - 1)
        def _():
            o_ref[...] = _rms(o_ref[...], gf_ref[...])


def _mlp(x, gain, w_up, w_down, gain_final, layer, final_norm, tm=512, tf=1024, n_sub=2):
    s, d = x.shape
    ff = w_up.shape[-1]
    return pl.pallas_call(
        functools.partial(_mlp_kernel, final_norm=final_norm, n_sub=n_sub),
        grid=(s // tm, ff // tf),
        in_specs=[
            pl.BlockSpec((tm, d), lambda i, f: (i, 0)),
            pl.BlockSpec((None, 1, d), lambda i, f: (layer, 0, 0)),
            pl.BlockSpec((d, tf), lambda i, f: (0, f)),
            pl.BlockSpec((tf, d), lambda i, f: (f, 0)),
            pl.BlockSpec((1, d), lambda i, f: (0, 0)),
        ],
        out_specs=pl.BlockSpec((tm, d), lambda i, f: (i, 0)),
        out_shape=jax.ShapeDtypeStruct((s, d), F32),
        scratch_shapes=[pltpu.VMEM((tm, d), BF16)],
        compiler_params=_params(2),
        name="mlp",
    )(x, gain, w_up, w_down, gain_final)


def _w_in_prep_kernel(w_ref, o_ref):
    lr0 = 2 * GLA_QK_WIDTH + 2 * GLA_V_WIDTH
    lr_w = 2 * GLA_GATE_RANK
    tail = w_ref.shape[0] - lr0 - lr_w
    o_ref[0:lr0, :] = w_ref[0:lr0, :].astype(o_ref.dtype)
    o_ref[lr0:lr0 + tail, :] = w_ref[lr0 + lr_w:, :].astype(o_ref.dtype)
    o_ref[lr0 + tail:lr0 + tail + lr_w, :] = w_ref[lr0:lr0 + lr_w, :].astype(o_ref.dtype)
    o_ref[lr0 + tail + lr_w:, :] = jnp.zeros((o_ref.shape[0] - lr0 - tail - lr_w, o_ref.shape[1]), o_ref.dtype)


def _reorder_w_in(w_in, cols=256):
    depth, d, n = w_in.shape
    assert COL_LR * LANES == n - 2 * GLA_GATE_RANK
    return pl.pallas_call(
        _w_in_prep_kernel,
        grid=(depth, d // cols),
        in_specs=[pl.BlockSpec((None, n, cols), lambda l, i: (l, 0, i))],
        out_specs=pl.BlockSpec((None, PROJ_WIDTH, cols), lambda l, i: (l, 0, i)),
        out_shape=jax.ShapeDtypeStruct((depth, PROJ_WIDTH, d), BF16),
        compiler_params=_params(2),
        name="w_in_prep",
    )(jnp.swapaxes(w_in, 1, 2))


def _pad_gate_up(up, row0):
    out = jnp.zeros((up.shape[0], LANES, up.shape[-1]), BF16)
    return out.at[:, row0:row0 + GLA_GATE_RANK, :].set(up.astype(BF16))


def kernel(x, mem, norm_mix, w_in, gla_gate_up_fwd, gla_gate_bias_fwd, gla_gate_up_bwd, gla_gate_bias_bwd, gla_norm, rel_bias, dil_norm, mem_norm, w_mem_kv, mem_out_norm, w_out, norm_mlp, w_up, w_down, norm_final):
    batch, seq, d = x.shape
    assert batch == 1 and seq == SEQ and d == D_MODEL
    xs = x.reshape(seq, d)
    row = lambda g: g.reshape(g.shape[0], 1, g.shape[-1])

    w_in_r = _reorder_w_in(w_in)
    up_f = _pad_gate_up(gla_gate_up_fwd, 0)
    up_b = _pad_gate_up(gla_gate_up_bwd, GLA_GATE_RANK)
    dil_bias = _dil_bias_tiles(rel_bias)
    kv = _mem_kv(mem.reshape(MEM_LEN, d), row(mem_norm), w_mem_kv)

    for layer in range(DEPTH):
        proj = _in_proj(xs, row(norm_mix), w_in_r, layer)
        o_f, o_b, w_up_b, w_down_b = _gla(proj, up_f, up_b, row(gla_gate_bias_fwd), row(gla_gate_bias_bwd),
                                          w_up, w_down, layer)
        dil, w_out_b = _dil_attention(proj, dil_bias, row(dil_norm), w_out, layer)
        xs = _mix_out(xs, proj, o_f, o_b, dil, kv, row(gla_norm), row(mem_out_norm), w_out_b, layer)
        xs = _mlp(xs, row(norm_mlp), w_up_b, w_down_b, norm_final.reshape(1, d), layer,
                  final_norm=(layer == DEPTH - 1))
    return xs.reshape(batch, seq, d)
```

```python
import functools
import math

import jax
import jax.numpy as jnp
from jax import lax
from jax.experimental import pallas as pl
from jax.experimental.pallas import tpu as pltpu

F32 = jnp.float32
BF16 = jnp.bfloat16

D_MODEL = 2048
SEQ = 8192
DEPTH = 4
MEM_LEN = 256
GLA_HEADS = 4
GLA_DK = 128
GLA_DV = 256
GLA_GATE_RANK = 16
GLA_GATE_NORMALIZER = 16.0
GLA_CHUNK = 64
DIL_HEADS = 4
DIL_HEAD_DIM = 128
DIL_CONFIGS = ((128, 1), (512, 4), (2048, 16))
MEM_HEADS = 4
MEM_HEAD_DIM = 128
REL_BUCKETS = 32
REL_MAX_DISTANCE = 1024
D_FF = 4 * D_MODEL
EPS = 1e-6
NEG_INF = -1e30

GLA_QK_WIDTH = GLA_HEADS * GLA_DK
GLA_V_WIDTH = GLA_HEADS * GLA_DV
DIL_WIDTH = DIL_HEADS * DIL_HEAD_DIM
MEM_WIDTH = MEM_HEADS * MEM_HEAD_DIM
MIX_WIDTH = GLA_V_WIDTH + DIL_WIDTH + MEM_WIDTH

LANES = 128

COL_GQ = 0
COL_GK = COL_GQ + GLA_QK_WIDTH // LANES
COL_GV = COL_GK + GLA_QK_WIDTH // LANES
COL_GR = COL_GV + GLA_V_WIDTH // LANES
COL_DQ = COL_GR + GLA_V_WIDTH // LANES
COL_DK = COL_DQ + DIL_WIDTH // LANES
COL_DV = COL_DK + DIL_WIDTH // LANES
COL_MQ = COL_DV + DIL_WIDTH // LANES
COL_LR = COL_MQ + MEM_WIDTH // LANES
PROJ_COLS = COL_LR + 2
PROJ_WIDTH = PROJ_COLS * LANES

GLA_BLOCK = 256

DIL_BAND = 64
DIL_TQ = 128
DIL_TK = DIL_TQ + 2 * DIL_BAND
DIL_DILATIONS = tuple(d for _, d in DIL_CONFIGS)
DIL_TB = DIL_TQ * max(DIL_DILATIONS)
DIL_INTERLEAVE = 8

VMEM_LIMIT = 48 * 1024 * 1024


def _params(n_axes, vmem=VMEM_LIMIT):
    return pltpu.CompilerParams(dimension_semantics=("arbitrary",) * n_axes, vmem_limit_bytes=vmem)


def _dot(a, b):
    return jnp.dot(a, b, preferred_element_type=F32)


def _dot_nt(a, b):
    return lax.dot_general(a, b, (((1,), (1,)), ((), ())), preferred_element_type=F32)


def _dot_tn(a, b):
    return lax.dot_general(a, b, (((0,), (0,)), ((), ())), preferred_element_type=F32)


def _rms(x, gain):
    return x * lax.rsqrt(jnp.mean(x * x, axis=-1, keepdims=True) + EPS) * gain


def _run_interleaved(stage_generators):
    live = list(stage_generators)
    while live:
        live = [gen for gen in live if next(gen, StopIteration) is not StopIteration]


def _in_proj_kernel(x_ref, g_ref, w_ref, o_ref, h_ref, *, n_sub):
    j = pl.program_id(1)
    sub = x_ref.shape[0] // n_sub

    @pl.when(j == 0)
    def _():
        for c in range(n_sub):
            rows = slice(c * sub, (c + 1) * sub)
            h = _rms(x_ref[rows, :], g_ref[...]).astype(h_ref.dtype)
            h_ref[rows, :] = h
            o_ref[rows, :] = _dot_nt(h, w_ref[...]).astype(o_ref.dtype)

    @pl.when(j != 0)
    def _():
        for c in range(n_sub):
            rows = slice(c * sub, (c + 1) * sub)
            o_ref[rows, :] = _dot_nt(h_ref[rows, :], w_ref[...]).astype(o_ref.dtype)


def _in_proj(x, gain, w, layer, tm=1024, tn=1792, n_sub=4):
    s, d = x.shape
    n = w.shape[1]
    return pl.pallas_call(
        functools.partial(_in_proj_kernel, n_sub=n_sub),
        grid=(s // tm, n // tn),
        in_specs=[
            pl.BlockSpec((tm, d), lambda i, j: (i, 0)),
            pl.BlockSpec((None, 1, d), lambda i, j: (layer, 0, 0)),
            pl.BlockSpec((None, tn, d), lambda i, j: (layer, j, 0)),
        ],
        out_specs=pl.BlockSpec((tm, tn), lambda i, j: (i, j)),
        out_shape=jax.ShapeDtypeStruct((s, n), BF16),
        scratch_shapes=[pltpu.VMEM((tm, d), BF16)],
        compiler_params=_params(2),
        name="in_proj",
    )(x, gain, w)


def _log_sigmoid(x):
    return jnp.minimum(x, 0.0) - jnp.log(1.0 + jnp.exp(-jnp.abs(x)))


GLA_LEVELS = (GLA_BLOCK // GLA_CHUNK).bit_length() - 1


def _gla_tables(tri_ref, lvl_ref, reverse):
    t = GLA_BLOCK
    row = lax.broadcasted_iota(jnp.int32, (t, t), 0)
    col = lax.broadcasted_iota(jnp.int32, (t, t), 1)
    valid = (col >= row) if reverse else (col <= row)
    tri_ref[...] = jnp.where(valid, 1.0, 0.0).astype(tri_ref.dtype)
    shift = GLA_CHUNK.bit_length() - 1
    lvl_ref[...] = jnp.where(jnp.right_shift(row, shift) == jnp.right_shift(col, shift),
                             jnp.where(valid, 0, -1), -1)


def _gla_boundary(cum, group, reverse):
    t = cum.shape[0]
    pieces = []
    for a in range(0, t, group):
        idx = a + group if reverse else a - 1
        if idx < 0 or idx >= t:
            ref_row = jnp.zeros((1, cum.shape[1]), cum.dtype)
        else:
            ref_row = cum[idx:idx + 1]
        pieces.append(jnp.broadcast_to(ref_row, (group, cum.shape[1])))
    return jnp.concatenate(pieces, axis=0)


def _place_rows(part, row0, total_rows):
    pieces = []
    if row0 > 0:
        pieces.append(jnp.zeros((row0, part.shape[1]), part.dtype))
    pieces.append(part)
    rest = total_rows - row0 - part.shape[0]
    if rest > 0:
        pieces.append(jnp.zeros((rest, part.shape[1]), part.dtype))
    return jnp.concatenate(pieces, axis=0)


def _gla_block(q_ref, k_ref, v_ref, lr_ref, up_ref, bias_ref, o_ref, s_ref, tri_ref, lvl_ref, r0, reverse):
    t = GLA_BLOCK
    rows = slice(r0, r0 + t)
    logits = _dot(lr_ref[rows, :], up_ref[...]) + bias_ref[...]
    yield
    g = _log_sigmoid(logits) * (1.0 / GLA_GATE_NORMALIZER)
    g_hi = g.astype(BF16)
    g_lo = (g - g_hi.astype(F32)).astype(BF16)
    cum2 = _dot(tri_ref[...], jnp.concatenate([g_hi, g_lo], axis=1))
    yield
    cum = cum2[:, :GLA_DK] + cum2[:, GLA_DK:]

    q = q_ref[rows, :].astype(F32) * (GLA_DK ** -0.5)
    k = k_ref[rows, :].astype(F32)
    v = v_ref[rows, :]
    lvl = lvl_ref[...]

    ref0 = _gla_boundary(cum, GLA_CHUNK, reverse)
    q_dec = (q * jnp.exp(cum - ref0)).astype(BF16)
    k_inv = (k * jnp.exp(ref0 - cum)).astype(BF16)
    q_segs, k_segs = [], []
    for level in range(1, GLA_LEVELS + 1):
        group = GLA_CHUNK << level
        for a0 in range(0, t, group):
            mid = a0 + group // 2
            att = slice(a0, mid) if reverse else slice(mid, a0 + group)
            src = slice(mid, a0 + group) if reverse else slice(a0, mid)
            ref_row = cum[mid:mid + 1] if reverse else cum[mid - 1:mid]
            q_part = (q[att] * jnp.exp(cum[att] - ref_row)).astype(BF16)
            k_part = (k[src] * jnp.exp(ref_row - cum[src])).astype(BF16)
            q_segs.append(_place_rows(q_part, att.start, t))
            k_segs.append(_place_rows(k_part, src.start, t))
    cross = _dot_nt(jnp.concatenate(q_segs, axis=1), jnp.concatenate(k_segs, axis=1))
    diag = _dot_nt(q_dec, k_inv)
    total = cum[0:1] if reverse else cum[t - 1:t]
    q_in = (q * jnp.exp(cum)).astype(BF16)
    k_out = (k * jnp.exp(total - cum)).astype(BF16)
    carry_in = _dot_tn(v, k_out)
    yield
    a = jnp.where(lvl == 0, diag, cross)
    o_local = _dot(a.astype(BF16), v)
    yield
    state = s_ref[...]
    o = o_local + _dot_nt(q_in, state.astype(BF16))
    o_ref[rows, :] = o.astype(o_ref.dtype)
    s_ref[...] = state * jnp.exp(total) + carry_in


def _gla_kernel(qf, kf, vf, lrf, qb, kb, vb, lrb, upf, upb, bsf, bsb, wu_ref, wd_ref, of, ob, wu_out, wd_out,
                sf, sb, tri_f, tri_b, lvl_f, lvl_b, *, interleave):
    @pl.when(pl.program_id(1) == 0)
    def _():
        sf[...] = jnp.zeros_like(sf)
        sb[...] = jnp.zeros_like(sb)
        _gla_tables(tri_f, lvl_f, reverse=False)
        _gla_tables(tri_b, lvl_b, reverse=True)

    starts = list(range(0, qf.shape[0], GLA_BLOCK))
    n_groups = len(starts) // interleave
    cast_jobs = [(wu_ref, wu_out), (wd_ref, wd_out)]

    def cast_piece(piece):
        for src, dst in cast_jobs:
            n = src.shape[0] // n_groups
            dst[piece * n:(piece + 1) * n, :] = src[piece * n:(piece + 1) * n, :].astype(dst.dtype)

    for gi in range(n_groups):
        blocks = []
        for r_fwd, r_bwd in list(zip(starts, reversed(starts)))[gi * interleave:(gi + 1) * interleave]:
            blocks.append(_gla_block(qf, kf, vf, lrf, upf, bsf, of, sf, tri_f, lvl_f, r_fwd, reverse=False))
            blocks.append(_gla_block(qb, kb, vb, lrb, upb, bsb, ob, sb, tri_b, lvl_b, r_bwd, reverse=True))
        _run_interleaved(blocks)
        cast_piece(gi)


def _gla(proj, up_f, up_b, bias_f, bias_b, w_up, w_down, layer, t=1024, interleave=4):
    s = proj.shape[0]
    nb = s // t
    vcol = COL_GV * LANES // GLA_DV
    n_steps = GLA_HEADS * nb
    d, ff = w_up.shape[1:]
    slab = lambda rows, cols: (
        pl.BlockSpec((None, rows // n_steps, cols), lambda h, i: (layer, h * nb + i, 0)),
        pl.BlockSpec((rows // n_steps, cols), lambda h, i: (h * nb + i, 0)),
        jax.ShapeDtypeStruct((rows, cols), BF16))
    wu_in, wu_out, wu_sds = slab(d, ff)
    wd_in, wd_out, wd_sds = slab(ff, d)

    def row_specs(rmap):
        return [
            pl.BlockSpec((t, GLA_DK), lambda h, i: (rmap(i), COL_GQ + h)),
            pl.BlockSpec((t, GLA_DK), lambda h, i: (rmap(i), COL_GK + h)),
            pl.BlockSpec((t, GLA_DV), lambda h, i: (rmap(i), vcol + h)),
            pl.BlockSpec((t, LANES), lambda h, i: (rmap(i), COL_LR)),
        ]

    fwd = lambda i: i
    bwd = lambda i: nb - 1 - i
    up_spec = pl.BlockSpec((None, LANES, GLA_DK), lambda h, i: (layer, 0, h))
    bias_spec = pl.BlockSpec((None, 1, GLA_DK), lambda h, i: (layer, 0, h))
    out_sds = jax.ShapeDtypeStruct((s, GLA_V_WIDTH), BF16)
    state = pltpu.VMEM((GLA_DV, GLA_DK), F32)
    tri = pltpu.VMEM((GLA_BLOCK, GLA_BLOCK), BF16)
    lvl = pltpu.VMEM((GLA_BLOCK, GLA_BLOCK), jnp.int32)
    return pl.pallas_call(
        functools.partial(_gla_kernel, interleave=interleave),
        grid=(GLA_HEADS, nb),
        in_specs=row_specs(fwd) + row_specs(bwd) + [up_spec, up_spec, bias_spec, bias_spec, wu_in, wd_in],
        out_specs=[
            pl.BlockSpec((t, GLA_DV), lambda h, i: (fwd(i), h)),
            pl.BlockSpec((t, GLA_DV), lambda h, i: (bwd(i), h)),
            wu_out, wd_out,
        ],
        out_shape=[out_sds, out_sds, wu_sds, wd_sds],
        scratch_shapes=[state, state, tri, tri, lvl, lvl],
        compiler_params=_params(2),
        name="gla_scan",
    )(proj, proj, proj, proj, proj, proj, proj, proj, up_f, up_b, bias_f, bias_b, w_up, w_down)


def _dil_tile(q, k, v, bias, edge, o_dst, l_dst, rows):
    raw = _dot_nt(q, k)
    yield
    sc = raw * (DIL_HEAD_DIM ** -0.5) + bias
    if edge is not None:
        sc = sc + edge
    m = jnp.max(sc, axis=-1, keepdims=True)
    p = jnp.exp(sc - m)
    den = jnp.sum(p, axis=-1, keepdims=True)
    pv = _dot(p.astype(BF16), v)
    yield
    o = pv / den
    o_dst[rows, :] = o
    l_dst[rows, :] = jnp.broadcast_to(m + jnp.log(den), o.shape)


def _dil_kernel(q_ref, k_ref, v_ref, bias_ref, gain_ref, wo_ref, out_ref, wo_out,
                stage, stage_mid, kg1, vg1, kg4, vg4, kg16, vg16, qp4, qp16, o1, o4, o16, l1, l4, l16):
    i = pl.program_id(1)
    s = k_ref.shape[0]
    tb = q_ref.shape[0]
    k_bufs = dict(zip(DIL_DILATIONS, (kg1, kg4, kg16)))
    v_bufs = dict(zip(DIL_DILATIONS, (vg1, vg4, vg16)))
    q_perm = dict(zip(DIL_DILATIONS, (None, qp4, qp16)))
    o_bufs = dict(zip(DIL_DILATIONS, (o1, o4, o16)))
    l_bufs = dict(zip(DIL_DILATIONS, (l1, l4, l16)))
    _, d_mid, d_max = DIL_DILATIONS
    assert d_max == d_mid * d_mid

    def deinterleave(dst_mid, dst_max, base_mid, base_max, residue_rows):
        per_mid, per_max = tb // d_mid, tb // d_max

        def at(base, offset):
            start = base + offset
            return start if isinstance(start, int) else pl.multiple_of(start, DIL_BAND)

        for r in range(d_mid):
            part = stage[pl.ds(r, per_mid, stride=d_mid), :]
            stage_mid[r * per_mid:(r + 1) * per_mid, :] = part
            dst_mid[pl.ds(at(base_mid, r * (residue_rows // d_mid)), per_mid), :] = part.astype(dst_mid.dtype)
        for r in range(d_max):
            r_mid, m = r % d_mid, r // d_mid
            part = stage_mid[pl.ds(r_mid * per_mid + m, per_max, stride=d_mid), :]
            dst_max[pl.ds(at(base_max, r * (residue_rows // d_max)), per_max), :] = part.astype(dst_max.dtype)

    @pl.when(i == 0)
    def _():
        zeros = jnp.zeros((DIL_BAND, DIL_HEAD_DIM), BF16)
        for buf in (kg1, vg1, kg4, vg4, kg16, vg16):
            buf[0:DIL_BAND, :] = zeros
            buf[DIL_BAND + s:, :] = zeros
        kg1[DIL_BAND:DIL_BAND + s, :] = k_ref[...]
        vg1[DIL_BAND:DIL_BAND + s, :] = v_ref[...]

        def body(c, carry):
            base = pl.multiple_of(c * tb, tb)
            for src, bufs in ((k_ref, k_bufs), (v_ref, v_bufs)):
                stage[...] = src[pl.ds(base, tb), :].astype(F32)
                deinterleave(bufs[d_mid], bufs[d_max], DIL_BAND + c * (tb // d_mid),
                             DIL_BAND + c * (tb // d_max), s)
            return carry

        lax.fori_loop(0, s // tb, body, 0)

    stage[...] = q_ref[...].astype(F32)
    deinterleave(q_perm[d_mid], q_perm[d_max], 0, 0, tb)

    lane = lax.broadcasted_iota(jnp.int32, (1, DIL_TK), 1)
    tiles = []
    for b, d in enumerate(DIL_DILATIONS):
        sub_len = s // d
        per = tb // d
        n_tiles = per // DIL_TQ
        for r in range(d):
            for tj in range(n_tiles):
                tiles.append((b, d, sub_len, per, n_tiles, r, tj))

    def tile_stages(b, d, sub_len, per, n_tiles, r, tj):
        u0 = i * per + tj * DIL_TQ
        q_src = q_ref if d == 1 else q_perm[d]
        q = q_src[r * per + tj * DIL_TQ:r * per + (tj + 1) * DIL_TQ, :]
        k_start = pl.multiple_of(r * sub_len + u0, DIL_TQ)
        k = k_bufs[d][pl.ds(k_start, DIL_TK), :]
        v = v_bufs[d][pl.ds(k_start, DIL_TK), :]
        edge = None
        if tj == 0 or tj == n_tiles - 1:
            key_pos = u0 - DIL_BAND + lane
            edge = jnp.where((key_pos >= 0) & (key_pos < sub_len), 0.0, NEG_INF)
        if d == 1:
            rows = pl.ds(tj * DIL_TQ, DIL_TQ)
        else:
            rows = pl.ds(r + d * tj * DIL_TQ, DIL_TQ, stride=d)
        return _dil_tile(q, k, v, bias_ref[b], edge, o_bufs[d], l_bufs[d], rows)

    for g0 in range(0, len(tiles), DIL_INTERLEAVE):
        _run_interleaved([tile_stages(*tile) for tile in tiles[g0:g0 + DIL_INTERLEAVE]])

    lses = [l_bufs[d][...] for d in DIL_DILATIONS]
    m = functools.reduce(jnp.maximum, lses)
    es = [jnp.exp(l - m) for l in lses]
    num = sum(e * o_bufs[d][...] for e, d in zip(es, DIL_DILATIONS))
    mixed = num / sum(es)
    out_ref[...] = _head_rms(mixed, gain_ref[...]).astype(out_ref.dtype)

    wo_out[...] = wo_ref[...].astype(wo_out.dtype)


def _dil_attention(proj, bias, gain, w_out, layer):
    s = proj.shape[0]
    tb = DIL_TB
    nq = s // tb
    n_steps = DIL_HEADS * nq
    mix_w, d = w_out.shape[1:]
    guarded = pltpu.VMEM((s + 2 * DIL_BAND, DIL_HEAD_DIM), BF16)
    tile_f32 = pltpu.VMEM((tb, DIL_HEAD_DIM), F32)
    tile_bf16 = pltpu.VMEM((tb, DIL_HEAD_DIM), BF16)
    return pl.pallas_call(
        _dil_kernel,
        grid=(DIL_HEADS, nq),
        in_specs=[
            pl.BlockSpec((tb, DIL_HEAD_DIM), lambda h, i: (i, COL_DQ + h)),
            pl.BlockSpec((s, DIL_HEAD_DIM), lambda h, i: (0, COL_DK + h)),
            pl.BlockSpec((s, DIL_HEAD_DIM), lambda h, i: (0, COL_DV + h)),
            pl.BlockSpec((len(DIL_DILATIONS), None, DIL_TQ, DIL_TK), lambda h, i: (0, h, 0, 0)),
            pl.BlockSpec((None, 1, DIL_HEAD_DIM), lambda h, i: (layer, 0, h)),
            pl.BlockSpec((None, mix_w // n_steps, d), lambda h, i: (layer, h * nq + i, 0)),
        ],
        out_specs=[pl.BlockSpec((tb, DIL_HEAD_DIM), lambda h, i: (i, h)),
                   pl.BlockSpec((mix_w // n_steps, d), lambda h, i: (h * nq + i, 0))],
        out_shape=[jax.ShapeDtypeStruct((s, DIL_WIDTH), BF16), jax.ShapeDtypeStruct((mix_w, d), BF16)],
        scratch_shapes=[tile_f32] * 2 + [guarded] * 6 + [tile_bf16] * 2 + [tile_f32] * 6,
        compiler_params=_params(2),
        name="dil_attn",
    )(proj, proj, proj, bias, gain, w_out)


def _t5_bucket(rel):
    half = REL_BUCKETS // 2
    max_exact = half // 2
    ret = jnp.where(rel > 0, half, 0)
    n = jnp.abs(rel)
    nf = jnp.maximum(n, 1).astype(F32)
    large = max_exact + (jnp.log(nf / max_exact) / math.log(REL_MAX_DISTANCE / max_exact)
                         * (half - max_exact)).astype(jnp.int32)
    large = jnp.minimum(large, half - 1)
    return ret + jnp.where(n < max_exact, n, large)


def _dil_bias_tiles(rel_bias):
    rel = jnp.arange(DIL_TK)[None, :] - DIL_BAND - jnp.arange(DIL_TQ)[:, None]
    band = jnp.abs(rel) <= DIL_BAND
    tiles = []
    for d in DIL_DILATIONS:
        onehot = (_t5_bucket(rel * d)[..., None] == jnp.arange(REL_BUCKETS)).astype(F32)
        bias = jnp.einsum("qkb,bh->hqk", onehot, rel_bias.astype(F32), precision=lax.Precision.HIGHEST)
        tiles.append(jnp.where(band[None], bias, NEG_INF))
    return jnp.stack(tiles, axis=0)


def _mem_kv_kernel(mem_ref, g_ref, w_ref, o_ref):
    h = _rms(mem_ref[...], g_ref[...]).astype(BF16)
    o_ref[...] = _dot(h, w_ref[...].astype(BF16)).astype(o_ref.dtype)


def _mem_kv(mem, gain, w):
    m, d = mem.shape
    n = w.shape[-1]
    return pl.pallas_call(
        _mem_kv_kernel,
        grid=(DEPTH,),
        in_specs=[
            pl.BlockSpec((m, d), lambda l: (0, 0)),
            pl.BlockSpec((None, 1, d), lambda l: (l, 0, 0)),
            pl.BlockSpec((None, d, n), lambda l: (l, 0, 0)),
        ],
        out_specs=pl.BlockSpec((None, m, n), lambda l: (l, 0, 0)),
        out_shape=jax.ShapeDtypeStruct((DEPTH, m, n), BF16),
        compiler_params=_params(1),
        name="mem_kv",
    )(mem, gain, w)


def _head_rms(o, gain):
    return o * lax.rsqrt(jnp.mean(o * o, axis=-1, keepdims=True) + EPS) * gain


def _mix_kernel(of_ref, ob_ref, gr_ref, dil_ref, mq_ref, km_ref, vm_ref, ggla_ref, gmem_ref, w_ref, x_ref,
                out_ref, gla_ref, mem_ref):
    dil0 = GLA_V_WIDTH
    mem0 = GLA_V_WIDTH + DIL_WIDTH

    def mem_head(h):
        sl = slice(h * MEM_HEAD_DIM, (h + 1) * MEM_HEAD_DIM)
        raw = _dot_nt(mq_ref[:, sl], km_ref[:, sl])
        yield
        sc = raw * (MEM_HEAD_DIM ** -0.5)
        m = jnp.max(sc, axis=-1, keepdims=True)
        p = jnp.exp(sc - m)
        den = jnp.sum(p, axis=-1, keepdims=True)
        pv = _dot(p.astype(BF16), vm_ref[:, sl])
        yield
        mem_ref[:, sl] = _head_rms(pv / den, gmem_ref[:, sl]).astype(mem_ref.dtype)

    def gla_heads():
        for h in range(GLA_HEADS):
            sl = slice(h * GLA_DV, (h + 1) * GLA_DV)
            o = of_ref[:, sl].astype(F32) + ob_ref[:, sl].astype(F32)
            r = gr_ref[:, sl].astype(F32)
            gate = r * (1.0 / (1.0 + jnp.exp(-r)))
            gla_ref[:, sl] = (_head_rms(o, ggla_ref[:, sl]) * gate).astype(gla_ref.dtype)
        yield
        out_ref[...] += _dot(gla_ref[...], w_ref[0:dil0, :])

    def dil_group():
        out_ref[...] = x_ref[...] + _dot(dil_ref[...], w_ref[dil0:mem0, :])
        yield

    _run_interleaved([mem_head(h) for h in range(MEM_HEADS)] + [dil_group(), gla_heads()])
    out_ref[...] += _dot(mem_ref[...], w_ref[mem0:, :])


def _mix_out(x, proj, o_f, o_b, dil, kv, g_gla, g_mem, w_out, layer, t=512):
    s, d = x.shape
    rows = lambda w, c=0: pl.BlockSpec((t, w), lambda i: (i, c))
    per_layer = lambda shape, c=0: pl.BlockSpec((None,) + shape, lambda i: (layer, 0, c),
                                                pipeline_mode=pl.Buffered(1))
    return pl.pallas_call(
        _mix_kernel,
        grid=(s // t,),
        in_specs=[
            rows(GLA_V_WIDTH), rows(GLA_V_WIDTH),
            rows(GLA_V_WIDTH, COL_GR * LANES // GLA_V_WIDTH),
            rows(DIL_WIDTH),
            rows(MEM_WIDTH, COL_MQ * LANES // MEM_WIDTH),
            per_layer((MEM_LEN, MEM_WIDTH), 0), per_layer((MEM_LEN, MEM_WIDTH), 1),
            per_layer((1, GLA_V_WIDTH)), per_layer((1, MEM_WIDTH)),
            pl.BlockSpec((MIX_WIDTH, d), lambda i: (0, 0), pipeline_mode=pl.Buffered(1)),
            rows(d),
        ],
        out_specs=rows(d),
        out_shape=jax.ShapeDtypeStruct((s, d), F32),
        scratch_shapes=[pltpu.VMEM((t, GLA_V_WIDTH), BF16), pltpu.VMEM((t, MEM_WIDTH), BF16)],
        compiler_params=_params(1),
        name="mix_out_proj",
    )(o_f, o_b, proj, dil, proj, kv, kv, g_gla, g_mem, w_out, x)


def _mlp_kernel(x_ref, g_ref, wu_ref, wd_ref, gf_ref, o_ref, h_ref, *, final_norm, n_sub):
    f = pl.program_id(1)
    sub = x_ref.shape[0] // n_sub

    def hidden(h):
        return jnp.square(jnp.maximum(_dot(h, wu_ref[...]), 0.0)).astype(BF16)

    @pl.when(f == 0)
    def _():
        for c in range(n_sub):
            rows = slice(c * sub, (c + 1) * sub)
            x = x_ref[rows, :]
            h = _rms(x, g_ref[...]).astype(h_ref.dtype)
            h_ref[rows, :] = h
            o_ref[rows, :] = x + _dot(hidden(h), wd_ref[...])

    @pl.when(f != 0)
    def _():
        o_ref[...] += _dot(hidden(h_ref[...]), wd_ref[...])

    if final_norm:
        @pl.when(f == pl.num_programs(1) - 1)
        def _():
            o_ref[...] = _rms(o_ref[...], gf_ref[...])


def _mlp(x, gain, w_up, w_down, gain_final, layer, final_norm, tm=512, tf=1024, n_sub=2):
    s, d = x.shape
    ff = w_up.shape[-1]
    return pl.pallas_call(
        functools.partial(_mlp_kernel, final_norm=final_norm, n_sub=n_sub),
        grid=(s // tm, ff // tf),
        in_specs=[
            pl.BlockSpec((tm, d), lambda i, f: (i, 0)),
            pl.BlockSpec((None, 1, d), lambda i, f: (layer, 0, 0)),
            pl.BlockSpec((d, tf), lambda i, f: (0, f)),
            pl.BlockSpec((tf, d), lambda i, f: (f, 0)),
            pl.BlockSpec((1, d), lambda i, f: (0, 0)),
        ],
        out_specs=pl.BlockSpec((tm, d), lambda i, f: (i, 0)),
        out_shape=jax.ShapeDtypeStruct((s, d), F32),
        scratch_shapes=[pltpu.VMEM((tm, d), BF16)],
        compiler_params=_params(2),
        name="mlp",
    )(x, gain, w_up, w_down, gain_final)


def _w_in_prep_kernel(w_ref, o_ref):
    lr0 = 2 * GLA_QK_WIDTH + 2 * GLA_V_WIDTH
    lr_w = 2 * GLA_GATE_RANK
    tail = w_ref.shape[0] - lr0 - lr_w
    o_ref[0:lr0, :] = w_ref[0:lr0, :].astype(o_ref.dtype)
    o_ref[lr0:lr0 + tail, :] = w_ref[lr0 + lr_w:, :].astype(o_ref.dtype)
    o_ref[lr0 + tail:lr0 + tail + lr_w, :] = w_ref[lr0:lr0 + lr_w, :].astype(o_ref.dtype)
    o_ref[lr0 + tail + lr_w:, :] = jnp.zeros((o_ref.shape[0] - lr0 - tail - lr_w, o_ref.shape[1]), o_ref.dtype)


def _reorder_w_in(w_in, cols=256):
    depth, d, n = w_in.shape
    assert COL_LR * LANES == n - 2 * GLA_GATE_RANK
    return pl.pallas_call(
        _w_in_prep_kernel,
        grid=(depth, d // cols),
        in_specs=[pl.BlockSpec((None, n, cols), lambda l, i: (l, 0, i))],
        out_specs=pl.BlockSpec((None, PROJ_WIDTH, cols), lambda l, i: (l, 0, i)),
        out_shape=jax.ShapeDtypeStruct((depth, PROJ_WIDTH, d), BF16),
        compiler_params=_params(2),
        name="w_in_prep",
    )(jnp.swapaxes(w_in, 1, 2))


def _pad_gate_up(up, row0):
    out = jnp.zeros((up.shape[0], LANES, up.shape[-1]), BF16)
    return out.at[:, row0:row0 + GLA_GATE_RANK, :].set(up.astype(BF16))


def kernel(x, mem, norm_mix, w_in, gla_gate_up_fwd, gla_gate_bias_fwd, gla_gate_up_bwd, gla_gate_bias_bwd, gla_norm, rel_bias, dil_norm, mem_norm, w_mem_kv, mem_out_norm, w_out, norm_mlp, w_up, w_down, norm_final):
    batch, seq, d = x.shape
    assert batch == 1 and seq == SEQ and d == D_MODEL
    xs = x.reshape(seq, d)
    row = lambda g: g.reshape(g.shape[0], 1, g.shape[-1])

    w_in_r = _reorder_w_in(w_in)
    up_f = _pad_gate_up(gla_gate_up_fwd, 0)
    up_b = _pad_gate_up(gla_gate_up_bwd, GLA_GATE_RANK)
    dil_bias = _dil_bias_tiles(rel_bias)
    kv = _mem_kv(mem.reshape(MEM_LEN, d), row(mem_norm), w_mem_kv)

    for layer in range(DEPTH):
        proj = _in_proj(xs, row(norm_mix), w_in_r, layer)
        o_f, o_b, w_up_b, w_down_b = _gla(proj, up_f, up_b, row(gla_gate_bias_fwd), row(gla_gate_bias_bwd),
                                          w_up, w_down, layer)
        dil, w_out_b = _dil_attention(proj, dil_bias, row(dil_norm), w_out, layer)
        xs = _mix_out(xs, proj, o_f, o_b, dil, kv, row(gla_norm), row(mem_out_norm), w_out_b, layer)
        xs = _mlp(xs, row(norm_mlp), w_up_b, w_down_b, norm_final.reshape(1, d), layer,
                  final_norm=(layer == DEPTH - 1))
    return xs.reshape(batch, seq, d)
```

```python
import functools
import math

import jax
import jax.numpy as jnp
from jax import lax
from jax.experimental import pallas as pl
from jax.experimental.pallas import tpu as pltpu

F32 = jnp.float32
BF16 = jnp.bfloat16

D_MODEL = 2048
SEQ = 8192
DEPTH = 4
MEM_LEN = 256
GLA_HEADS = 4
GLA_DK = 128
GLA_DV = 256
GLA_GATE_RANK = 16
GLA_GATE_NORMALIZER = 16.0
GLA_CHUNK = 64
DIL_HEADS = 4
DIL_HEAD_DIM = 128
DIL_CONFIGS = ((128, 1), (512, 4), (2048, 16))
MEM_HEADS = 4
MEM_HEAD_DIM = 128
REL_BUCKETS = 32
REL_MAX_DISTANCE = 1024
D_FF = 4 * D_MODEL
EPS = 1e-6
NEG_INF = -1e30

GLA_QK_WIDTH = GLA_HEADS * GLA_DK
GLA_V_WIDTH = GLA_HEADS * GLA_DV
DIL_WIDTH = DIL_HEADS * DIL_HEAD_DIM
MEM_WIDTH = MEM_HEADS * MEM_HEAD_DIM
MIX_WIDTH = GLA_V_WIDTH + DIL_WIDTH + MEM_WIDTH

LANES = 128

COL_GQ = 0
COL_GK = COL_GQ + GLA_QK_WIDTH // LANES
COL_GV = COL_GK + GLA_QK_WIDTH // LANES
COL_GR = COL_GV + GLA_V_WIDTH // LANES
COL_DQ = COL_GR + GLA_V_WIDTH // LANES
COL_DK = COL_DQ + DIL_WIDTH // LANES
COL_DV = COL_DK + DIL_WIDTH // LANES
COL_MQ = COL_DV + DIL_WIDTH // LANES
COL_LR = COL_MQ + MEM_WIDTH // LANES
PROJ_COLS = COL_LR + 2
PROJ_WIDTH = PROJ_COLS * LANES

GLA_BLOCK = 256

DIL_BAND = 64
DIL_TQ = 128
DIL_TK = DIL_TQ + 2 * DIL_BAND
DIL_DILATIONS = tuple(d for _, d in DIL_CONFIGS)
DIL_TB = DIL_TQ * max(DIL_DILATIONS)
DIL_INTERLEAVE = 8

VMEM_LIMIT = 48 * 1024 * 1024
BIG_VMEM_LIMIT = 60 * 1024 * 1024


def _params(n_axes, vmem=VMEM_LIMIT):
    return pltpu.CompilerParams(dimension_semantics=("arbitrary",) * n_axes, vmem_limit_bytes=vmem)


def _dot(a, b):
    return jnp.dot(a, b, preferred_element_type=F32)


def _dot_nt(a, b):
    return lax.dot_general(a, b, (((1,), (1,)), ((), ())), preferred_element_type=F32)


def _dot_tn(a, b):
    return lax.dot_general(a, b, (((0,), (0,)), ((), ())), preferred_element_type=F32)


def _rms(x, gain):
    return x * lax.rsqrt(jnp.mean(x * x, axis=-1, keepdims=True) + EPS) * gain


def _run_interleaved(stage_generators):
    live = list(stage_generators)
    while live:
        live = [gen for gen in live if next(gen, StopIteration) is not StopIteration]


def _in_proj_kernel(x_ref, g_ref, w_ref, o_ref, h_ref, *, n_sub):
    j = pl.program_id(1)
    sub = x_ref.shape[0] // n_sub

    @pl.when(j == 0)
    def _():
        for c in range(n_sub):
            rows = slice(c * sub, (c + 1) * sub)
            h = _rms(x_ref[rows, :], g_ref[...]).astype(h_ref.dtype)
            h_ref[rows, :] = h
            o_ref[rows, :] = _dot_nt(h, w_ref[...]).astype(o_ref.dtype)

    @pl.when(j != 0)
    def _():
        for c in range(n_sub):
            rows = slice(c * sub, (c + 1) * sub)
            o_ref[rows, :] = _dot_nt(h_ref[rows, :], w_ref[...]).astype(o_ref.dtype)


def _in_proj(x, gain, w, layer, tm=1024, tn=1792, n_sub=4):
    s, d = x.shape
    n = w.shape[1]
    return pl.pallas_call(
        functools.partial(_in_proj_kernel, n_sub=n_sub),
        grid=(s // tm, n // tn),
        in_specs=[
            pl.BlockSpec((tm, d), lambda i, j: (i, 0)),
            pl.BlockSpec((None, 1, d), lambda i, j: (layer, 0, 0)),
            pl.BlockSpec((None, tn, d), lambda i, j: (layer, j, 0)),
        ],
        out_specs=pl.BlockSpec((tm, tn), lambda i, j: (i, j)),
        out_shape=jax.ShapeDtypeStruct((s, n), BF16),
        scratch_shapes=[pltpu.VMEM((tm, d), BF16)],
        compiler_params=_params(2),
        name="in_proj",
    )(x, gain, w)


def _log_sigmoid(x):
    return jnp.minimum(x, 0.0) - jnp.log(1.0 + jnp.exp(-jnp.abs(x)))


GLA_LEVELS = (GLA_BLOCK // GLA_CHUNK).bit_length() - 1


def _gla_tables(tri_ref, lvl_ref, reverse):
    t = GLA_BLOCK
    row = lax.broadcasted_iota(jnp.int32, (t, t), 0)
    col = lax.broadcasted_iota(jnp.int32, (t, t), 1)
    valid = (col >= row) if reverse else (col <= row)
    tri_ref[...] = jnp.where(valid, 1.0, 0.0).astype(tri_ref.dtype)
    shift = GLA_CHUNK.bit_length() - 1
    lvl_ref[...] = jnp.where(jnp.right_shift(row, shift) == jnp.right_shift(col, shift),
                             jnp.where(valid, 0, -1), -1)


def _gla_boundary(cum, group, reverse):
    t = cum.shape[0]
    pieces = []
    for a in range(0, t, group):
        idx = a + group if reverse else a - 1
        if idx < 0 or idx >= t:
            ref_row = jnp.zeros((1, cum.shape[1]), cum.dtype)
        else:
            ref_row = cum[idx:idx + 1]
        pieces.append(jnp.broadcast_to(ref_row, (group, cum.shape[1])))
    return jnp.concatenate(pieces, axis=0)


def _place_rows(part, row0, total_rows):
    pieces = []
    if row0 > 0:
        pieces.append(jnp.zeros((row0, part.shape[1]), part.dtype))
    pieces.append(part)
    rest = total_rows - row0 - part.shape[0]
    if rest > 0:
        pieces.append(jnp.zeros((rest, part.shape[1]), part.dtype))
    return jnp.concatenate(pieces, axis=0)


def _gla_block(q_ref, k_ref, v_ref, lr_ref, up_ref, bias_ref, o_ref, s_ref, tri_ref, lvl_ref, r0, reverse):
    t = GLA_BLOCK
    rows = slice(r0, r0 + t)
    logits = _dot(lr_ref[rows, :], up_ref[...]) + bias_ref[...]
    yield
    g = _log_sigmoid(logits) * (1.0 / GLA_GATE_NORMALIZER)
    g_hi = g.astype(BF16)
    g_lo = (g - g_hi.astype(F32)).astype(BF16)
    cum2 = _dot(tri_ref[...], jnp.concatenate([g_hi, g_lo], axis=1))
    yield
    cum = cum2[:, :GLA_DK] + cum2[:, GLA_DK:]

    q = q_ref[rows, :].astype(F32) * (GLA_DK ** -0.5)
    k = k_ref[rows, :].astype(F32)
    v = v_ref[rows, :]
    lvl = lvl_ref[...]

    ref0 = _gla_boundary(cum, GLA_CHUNK, reverse)
    q_dec = (q * jnp.exp(cum - ref0)).astype(BF16)
    k_inv = (k * jnp.exp(ref0 - cum)).astype(BF16)
    q_segs, k_segs = [], []
    for level in range(1, GLA_LEVELS + 1):
        group = GLA_CHUNK << level
        for a0 in range(0, t, group):
            mid = a0 + group // 2
            att = slice(a0, mid) if reverse else slice(mid, a0 + group)
            src = slice(mid, a0 + group) if reverse else slice(a0, mid)
            ref_row = cum[mid:mid + 1] if reverse else cum[mid - 1:mid]
            q_part = (q[att] * jnp.exp(cum[att] - ref_row)).astype(BF16)
            k_part = (k[src] * jnp.exp(ref_row - cum[src])).astype(BF16)
            q_segs.append(_place_rows(q_part, att.start, t))
            k_segs.append(_place_rows(k_part, src.start, t))
    cross = _dot_nt(jnp.concatenate(q_segs, axis=1), jnp.concatenate(k_segs, axis=1))
    diag = _dot_nt(q_dec, k_inv)
    total = cum[0:1] if reverse else cum[t - 1:t]
    q_in = (q * jnp.exp(cum)).astype(BF16)
    k_out = (k * jnp.exp(total - cum)).astype(BF16)
    carry_in = _dot_tn(v, k_out)
    yield
    a = jnp.where(lvl == 0, diag, cross)
    o_local = _dot(a.astype(BF16), v)
    yield
    state = s_ref[...]
    o = o_local + _dot_nt(q_in, state.astype(BF16))
    o_ref[rows, :] = o.astype(o_ref.dtype)
    s_ref[...] = state * jnp.exp(total) + carry_in


def _cast_slab(weights, layer, n_steps, step_of):
    rows, cols = weights.shape[1:]
    slab = rows // n_steps
    return (pl.BlockSpec((None, slab, cols), lambda *g: (layer, step_of(*g), 0)),
            pl.BlockSpec((slab, cols), lambda *g: (step_of(*g), 0)),
            jax.ShapeDtypeStruct((rows, cols), BF16))


def _gla_kernel(qf, kf, vf, lrf, qb, kb, vb, lrb, upf, upb, bsf, bsb, wu_ref, of, ob, wu_out,
                sf, sb, tri_f, tri_b, lvl_f, lvl_b, *, interleave):
    @pl.when(pl.program_id(1) == 0)
    def _():
        sf[...] = jnp.zeros_like(sf)
        sb[...] = jnp.zeros_like(sb)
        _gla_tables(tri_f, lvl_f, reverse=False)
        _gla_tables(tri_b, lvl_b, reverse=True)

    starts = list(range(0, qf.shape[0], GLA_BLOCK))
    n_groups = len(starts) // interleave

    def cast_piece(piece):
        n = wu_ref.shape[0] // n_groups
        wu_out[piece * n:(piece + 1) * n, :] = wu_ref[piece * n:(piece + 1) * n, :].astype(wu_out.dtype)

    for gi in range(n_groups):
        blocks = []
        for r_fwd, r_bwd in list(zip(starts, reversed(starts)))[gi * interleave:(gi + 1) * interleave]:
            blocks.append(_gla_block(qf, kf, vf, lrf, upf, bsf, of, sf, tri_f, lvl_f, r_fwd, reverse=False))
            blocks.append(_gla_block(qb, kb, vb, lrb, upb, bsb, ob, sb, tri_b, lvl_b, r_bwd, reverse=True))
        _run_interleaved(blocks)
        cast_piece(gi)


def _gla(proj, up_f, up_b, bias_f, bias_b, w_up, layer, t=1024, interleave=4):
    s = proj.shape[0]
    nb = s // t
    vcol = COL_GV * LANES // GLA_DV
    wu_in, wu_out, wu_sds = _cast_slab(w_up, layer, GLA_HEADS * nb, lambda h, i: h * nb + i)

    def row_specs(rmap):
        return [
            pl.BlockSpec((t, GLA_DK), lambda h, i: (rmap(i), COL_GQ + h)),
            pl.BlockSpec((t, GLA_DK), lambda h, i: (rmap(i), COL_GK + h)),
            pl.BlockSpec((t, GLA_DV), lambda h, i: (rmap(i), vcol + h)),
            pl.BlockSpec((t, LANES), lambda h, i: (rmap(i), COL_LR)),
        ]

    fwd = lambda i: i
    bwd = lambda i: nb - 1 - i
    up_spec = pl.BlockSpec((None, LANES, GLA_DK), lambda h, i: (layer, 0, h))
    bias_spec = pl.BlockSpec((None, 1, GLA_DK), lambda h, i: (layer, 0, h))
    out_sds = jax.ShapeDtypeStruct((s, GLA_V_WIDTH), BF16)
    state = pltpu.VMEM((GLA_DV, GLA_DK), F32)
    tri = pltpu.VMEM((GLA_BLOCK, GLA_BLOCK), BF16)
    lvl = pltpu.VMEM((GLA_BLOCK, GLA_BLOCK), jnp.int32)
    return pl.pallas_call(
        functools.partial(_gla_kernel, interleave=interleave),
        grid=(GLA_HEADS, nb),
        in_specs=row_specs(fwd) + row_specs(bwd) + [up_spec, up_spec, bias_spec, bias_spec, wu_in],
        out_specs=[
            pl.BlockSpec((t, GLA_DV), lambda h, i: (fwd(i), h)),
            pl.BlockSpec((t, GLA_DV), lambda h, i: (bwd(i), h)),
            wu_out,
        ],
        out_shape=[out_sds, out_sds, wu_sds],
        scratch_shapes=[state, state, tri, tri, lvl, lvl],
        compiler_params=_params(2),
        name="gla_scan",
    )(proj, proj, proj, proj, proj, proj, proj, proj, up_f, up_b, bias_f, bias_b, w_up)


def _dil_tile(q, k, v, bias, edge, o_dst, l_dst, rows):
    raw = _dot_nt(q, k)
    yield
    sc = raw * (DIL_HEAD_DIM ** -0.5) + bias
    if edge is not None:
        sc = sc + edge
    m = jnp.max(sc, axis=-1, keepdims=True)
    p = jnp.exp(sc - m)
    den = jnp.sum(p, axis=-1, keepdims=True)
    pv = _dot(p.astype(BF16), v)
    yield
    o = pv / den
    o_dst[rows, :] = o
    l_dst[rows, :] = jnp.broadcast_to(m + jnp.log(den), o.shape)


def _dil_kernel(q_ref, k_ref, v_ref, bias_ref, gain_ref, wo_ref, wd_ref, out_ref, wo_out, wd_out,
                stage, stage_mid, kg1, vg1, kg4, vg4, kg16, vg16, qp4, qp16, o1, o4, o16, l1, l4, l16):
    i = pl.program_id(1)
    s = k_ref.shape[0]
    tb = q_ref.shape[0]
    k_bufs = dict(zip(DIL_DILATIONS, (kg1, kg4, kg16)))
    v_bufs = dict(zip(DIL_DILATIONS, (vg1, vg4, vg16)))
    q_perm = dict(zip(DIL_DILATIONS, (None, qp4, qp16)))
    o_bufs = dict(zip(DIL_DILATIONS, (o1, o4, o16)))
    l_bufs = dict(zip(DIL_DILATIONS, (l1, l4, l16)))
    _, d_mid, d_max = DIL_DILATIONS
    assert d_max == d_mid * d_mid

    def deinterleave(dst_mid, dst_max, base_mid, base_max, residue_rows):
        per_mid, per_max = tb // d_mid, tb // d_max

        def at(base, offset):
            start = base + offset
            return start if isinstance(start, int) else pl.multiple_of(start, DIL_BAND)

        for r in range(d_mid):
            part = stage[pl.ds(r, per_mid, stride=d_mid), :]
            stage_mid[r * per_mid:(r + 1) * per_mid, :] = part
            dst_mid[pl.ds(at(base_mid, r * (residue_rows // d_mid)), per_mid), :] = part.astype(dst_mid.dtype)
        for r in range(d_max):
            r_mid, m = r % d_mid, r // d_mid
            part = stage_mid[pl.ds(r_mid * per_mid + m, per_max, stride=d_mid), :]
            dst_max[pl.ds(at(base_max, r * (residue_rows // d_max)), per_max), :] = part.astype(dst_max.dtype)

    @pl.when(i == 0)
    def _():
        zeros = jnp.zeros((DIL_BAND, DIL_HEAD_DIM), BF16)
        for buf in (kg1, vg1, kg4, vg4, kg16, vg16):
            buf[0:DIL_BAND, :] = zeros
            buf[DIL_BAND + s:, :] = zeros
        kg1[DIL_BAND:DIL_BAND + s, :] = k_ref[...]
        vg1[DIL_BAND:DIL_BAND + s, :] = v_ref[...]

        def body(c, carry):
            base = pl.multiple_of(c * tb, tb)
            for src, bufs in ((k_ref, k_bufs), (v_ref, v_bufs)):
                stage[...] = src[pl.ds(base, tb), :].astype(F32)
                deinterleave(bufs[d_mid], bufs[d_max], DIL_BAND + c * (tb // d_mid),
                             DIL_BAND + c * (tb // d_max), s)
            return carry

        lax.fori_loop(0, s // tb, body, 0)

    stage[...] = q_ref[...].astype(F32)
    deinterleave(q_perm[d_mid], q_perm[d_max], 0, 0, tb)

    lane = lax.broadcasted_iota(jnp.int32, (1, DIL_TK), 1)
    tiles = []
    for b, d in enumerate(DIL_DILATIONS):
        sub_len = s // d
        per = tb // d
        n_tiles = per // DIL_TQ
        for r in range(d):
            for tj in range(n_tiles):
                tiles.append((b, d, sub_len, per, n_tiles, r, tj))

    def tile_stages(b, d, sub_len, per, n_tiles, r, tj):
        u0 = i * per + tj * DIL_TQ
        q_src = q_ref if d == 1 else q_perm[d]
        q = q_src[r * per + tj * DIL_TQ:r * per + (tj + 1) * DIL_TQ, :]
        k_start = pl.multiple_of(r * sub_len + u0, DIL_TQ)
        k = k_bufs[d][pl.ds(k_start, DIL_TK), :]
        v = v_bufs[d][pl.ds(k_start, DIL_TK), :]
        edge = None
        if tj == 0 or tj == n_tiles - 1:
            key_pos = u0 - DIL_BAND + lane
            edge = jnp.where((key_pos >= 0) & (key_pos < sub_len), 0.0, NEG_INF)
        if d == 1:
            rows = pl.ds(tj * DIL_TQ, DIL_TQ)
        else:
            rows = pl.ds(r + d * tj * DIL_TQ, DIL_TQ, stride=d)
        return _dil_tile(q, k, v, bias_ref[b], edge, o_bufs[d], l_bufs[d], rows)

    for g0 in range(0, len(tiles), DIL_INTERLEAVE):
        _run_interleaved([tile_stages(*tile) for tile in tiles[g0:g0 + DIL_INTERLEAVE]])

    lses = [l_bufs[d][...] for d in DIL_DILATIONS]
    m = functools.reduce(jnp.maximum, lses)
    es = [jnp.exp(l - m) for l in lses]
    num = sum(e * o_bufs[d][...] for e, d in zip(es, DIL_DILATIONS))
    mixed = num / sum(es)
    out_ref[...] = _head_rms(mixed, gain_ref[...]).astype(out_ref.dtype)

    wo_out[...] = wo_ref[...].astype(wo_out.dtype)
    wd_out[...] = wd_ref[...].astype(wd_out.dtype)


def _dil_attention(proj, bias, gain, w_out, w_down, layer):
    s = proj.shape[0]
    tb = DIL_TB
    nq = s // tb
    step_of = lambda h, i: h * nq + i
    wo_in, wo_out, wo_sds = _cast_slab(w_out, layer, DIL_HEADS * nq, step_of)
    wd_in, wd_out, wd_sds = _cast_slab(w_down, layer, DIL_HEADS * nq, step_of)
    guarded = pltpu.VMEM((s + 2 * DIL_BAND, DIL_HEAD_DIM), BF16)
    tile_f32 = pltpu.VMEM((tb, DIL_HEAD_DIM), F32)
    tile_bf16 = pltpu.VMEM((tb, DIL_HEAD_DIM), BF16)
    return pl.pallas_call(
        _dil_kernel,
        grid=(DIL_HEADS, nq),
        in_specs=[
            pl.BlockSpec((tb, DIL_HEAD_DIM), lambda h, i: (i, COL_DQ + h)),
            pl.BlockSpec((s, DIL_HEAD_DIM), lambda h, i: (0, COL_DK + h)),
            pl.BlockSpec((s, DIL_HEAD_DIM), lambda h, i: (0, COL_DV + h)),
            pl.BlockSpec((len(DIL_DILATIONS), None, DIL_TQ, DIL_TK), lambda h, i: (0, h, 0, 0)),
            pl.BlockSpec((None, 1, DIL_HEAD_DIM), lambda h, i: (layer, 0, h)),
            wo_in, wd_in,
        ],
        out_specs=[pl.BlockSpec((tb, DIL_HEAD_DIM), lambda h, i: (i, h)), wo_out, wd_out],
        out_shape=[jax.ShapeDtypeStruct((s, DIL_WIDTH), BF16), wo_sds, wd_sds],
        scratch_shapes=[tile_f32] * 2 + [guarded] * 6 + [tile_bf16] * 2 + [tile_f32] * 6,
        compiler_params=_params(2, BIG_VMEM_LIMIT),
        name="dil_attn",
    )(proj, proj, proj, bias, gain, w_out, w_down)


def _t5_bucket(rel):
    half = REL_BUCKETS // 2
    max_exact = half // 2
    ret = jnp.where(rel > 0, half, 0)
    n = jnp.abs(rel)
    nf = jnp.maximum(n, 1).astype(F32)
    large = max_exact + (jnp.log(nf / max_exact) / math.log(REL_MAX_DISTANCE / max_exact)
                         * (half - max_exact)).astype(jnp.int32)
    large = jnp.minimum(large, half - 1)
    return ret + jnp.where(n < max_exact, n, large)


def _dil_bias_tiles(rel_bias):
    rel = jnp.arange(DIL_TK)[None, :] - DIL_BAND - jnp.arange(DIL_TQ)[:, None]
    band = jnp.abs(rel) <= DIL_BAND
    tiles = []
    for d in DIL_DILATIONS:
        onehot = (_t5_bucket(rel * d)[..., None] == jnp.arange(REL_BUCKETS)).astype(F32)
        bias = jnp.einsum("qkb,bh->hqk", onehot, rel_bias.astype(F32), precision=lax.Precision.HIGHEST)
        tiles.append(jnp.where(band[None], bias, NEG_INF))
    return jnp.stack(tiles, axis=0)


def _mem_kv_kernel(mem_ref, g_ref, w_ref, o_ref):
    h = _rms(mem_ref[...], g_ref[...]).astype(BF16)
    o_ref[...] = _dot(h, w_ref[...].astype(BF16)).astype(o_ref.dtype)


def _mem_kv(mem, gain, w):
    m, d = mem.shape
    n = w.shape[-1]
    return pl.pallas_call(
        _mem_kv_kernel,
        grid=(DEPTH,),
        in_specs=[
            pl.BlockSpec((m, d), lambda l: (0, 0)),
            pl.BlockSpec((None, 1, d), lambda l: (l, 0, 0)),
            pl.BlockSpec((None, d, n), lambda l: (l, 0, 0)),
        ],
        out_specs=pl.BlockSpec((None, m, n), lambda l: (l, 0, 0)),
        out_shape=jax.ShapeDtypeStruct((DEPTH, m, n), BF16),
        compiler_params=_params(1),
        name="mem_kv",
    )(mem, gain, w)


def _head_rms(o, gain):
    return o * lax.rsqrt(jnp.mean(o * o, axis=-1, keepdims=True) + EPS) * gain


def _mix_kernel(of_ref, ob_ref, gr_ref, dil_ref, mq_ref, km_ref, vm_ref, ggla_ref, gmem_ref, w_ref, x_ref,
                out_ref, gla_ref, mem_ref):
    dil0 = GLA_V_WIDTH
    mem0 = GLA_V_WIDTH + DIL_WIDTH

    def mem_head(h):
        sl = slice(h * MEM_HEAD_DIM, (h + 1) * MEM_HEAD_DIM)
        raw = _dot_nt(mq_ref[:, sl], km_ref[:, sl])
        yield
        sc = raw * (MEM_HEAD_DIM ** -0.5)
        m = jnp.max(sc, axis=-1, keepdims=True)
        p = jnp.exp(sc - m)
        den = jnp.sum(p, axis=-1, keepdims=True)
        pv = _dot(p.astype(BF16), vm_ref[:, sl])
        yield
        mem_ref[:, sl] = _head_rms(pv / den, gmem_ref[:, sl]).astype(mem_ref.dtype)

    def gla_heads():
        for h in range(GLA_HEADS):
            sl = slice(h * GLA_DV, (h + 1) * GLA_DV)
            o = of_ref[:, sl].astype(F32) + ob_ref[:, sl].astype(F32)
            r = gr_ref[:, sl].astype(F32)
            gate = r * (1.0 / (1.0 + jnp.exp(-r)))
            gla_ref[:, sl] = (_head_rms(o, ggla_ref[:, sl]) * gate).astype(gla_ref.dtype)
        yield
        out_ref[...] += _dot(gla_ref[...], w_ref[0:dil0, :])

    def dil_group():
        out_ref[...] = x_ref[...] + _dot(dil_ref[...], w_ref[dil0:mem0, :])
        yield

    _run_interleaved([mem_head(h) for h in range(MEM_HEADS)] + [dil_group(), gla_heads()])
    out_ref[...] += _dot(mem_ref[...], w_ref[mem0:, :])


def _mix_out(x, proj, o_f, o_b, dil, kv, g_gla, g_mem, w_out, layer, t=512):
    s, d = x.shape
    rows = lambda w, c=0: pl.BlockSpec((t, w), lambda i: (i, c))
    per_layer = lambda shape, c=0: pl.BlockSpec((None,) + shape, lambda i: (layer, 0, c),
                                                pipeline_mode=pl.Buffered(1))
    return pl.pallas_call(
        _mix_kernel,
        grid=(s // t,),
        in_specs=[
            rows(GLA_V_WIDTH), rows(GLA_V_WIDTH),
            rows(GLA_V_WIDTH, COL_GR * LANES // GLA_V_WIDTH),
            rows(DIL_WIDTH),
            rows(MEM_WIDTH, COL_MQ * LANES // MEM_WIDTH),
            per_layer((MEM_LEN, MEM_WIDTH), 0), per_layer((MEM_LEN, MEM_WIDTH), 1),
            per_layer((1, GLA_V_WIDTH)), per_layer((1, MEM_WIDTH)),
            pl.BlockSpec((MIX_WIDTH, d), lambda i: (0, 0), pipeline_mode=pl.Buffered(1)),
            rows(d),
        ],
        out_specs=rows(d),
        out_shape=jax.ShapeDtypeStruct((s, d), F32),
        scratch_shapes=[pltpu.VMEM((t, GLA_V_WIDTH), BF16), pltpu.VMEM((t, MEM_WIDTH), BF16)],
        compiler_params=_params(1),
        name="mix_out_proj",
    )(o_f, o_b, proj, dil, proj, kv, kv, g_gla, g_mem, w_out, x)


def _mlp_kernel(x_ref, g_ref, wu_ref, wd_ref, gf_ref, o_ref, h_ref, *, final_norm, n_sub):
    f = pl.program_id(1)
    sub = x_ref.shape[0] // n_sub

    def hidden(h):
        return jnp.square(jnp.maximum(_dot(h, wu_ref[...]), 0.0)).astype(BF16)

    @pl.when(f == 0)
    def _():
        for c in range(n_sub):
            rows = slice(c * sub, (c + 1) * sub)
            x = x_ref[rows, :]
            h = _rms(x, g_ref[...]).astype(h_ref.dtype)
            h_ref[rows, :] = h
            o_ref[rows, :] = x + _dot(hidden(h), wd_ref[...])

    @pl.when(f != 0)
    def _():
        for c in range(n_sub):
            rows = slice(c * sub, (c + 1) * sub)
            o_ref[rows, :] += _dot(hidden(h_ref[rows, :]), wd_ref[...])

    if final_norm:
        @pl.when(f == pl.num_programs(1) - 1)
        def _():
            for c in range(n_sub):
                rows = slice(c * sub, (c + 1) * sub)
                o_ref[rows, :] = _rms(o_ref[rows, :], gf_ref[...])


def _mlp(x, gain, w_up, w_down, gain_final, layer, final_norm, tm=1024, tf=1024, n_sub=4):
    s, d = x.shape
    ff = w_up.shape[-1]
    return pl.pallas_call(
        functools.partial(_mlp_kernel, final_norm=final_norm, n_sub=n_sub),
        grid=(s // tm, ff // tf),
        in_specs=[
            pl.BlockSpec((tm, d), lambda i, f: (i, 0)),
            pl.BlockSpec((None, 1, d), lambda i, f: (layer, 0, 0)),
            pl.BlockSpec((d, tf), lambda i, f: (0, f)),
            pl.BlockSpec((tf, d), lambda i, f: (f, 0)),
            pl.BlockSpec((1, d), lambda i, f: (0, 0)),
        ],
        out_specs=pl.BlockSpec((tm, d), lambda i, f: (i, 0)),
        out_shape=jax.ShapeDtypeStruct((s, d), F32),
        scratch_shapes=[pltpu.VMEM((tm, d), BF16)],
        compiler_params=_params(2, BIG_VMEM_LIMIT),
        name="mlp",
    )(x, gain, w_up, w_down, gain_final)


def _w_in_prep_kernel(w_ref, o_ref):
    lr0 = 2 * GLA_QK_WIDTH + 2 * GLA_V_WIDTH
    lr_w = 2 * GLA_GATE_RANK
    tail = w_ref.shape[0] - lr0 - lr_w
    o_ref[0:lr0, :] = w_ref[0:lr0, :].astype(o_ref.dtype)
    o_ref[lr0:lr0 + tail, :] = w_ref[lr0 + lr_w:, :].astype(o_ref.dtype)
    o_ref[lr0 + tail:lr0 + tail + lr_w, :] = w_ref[lr0:lr0 + lr_w, :].astype(o_ref.dtype)
    o_ref[lr0 + tail + lr_w:, :] = jnp.zeros((o_ref.shape[0] - lr0 - tail - lr_w, o_ref.shape[1]), o_ref.dtype)


def _reorder_w_in(w_in, cols=256):
    depth, d, n = w_in.shape
    assert COL_LR * LANES == n - 2 * GLA_GATE_RANK
    return pl.pallas_call(
        _w_in_prep_kernel,
        grid=(depth, d // cols),
        in_specs=[pl.BlockSpec((None, n, cols), lambda l, i: (l, 0, i))],
        out_specs=pl.BlockSpec((None, PROJ_WIDTH, cols), lambda l, i: (l, 0, i)),
        out_shape=jax.ShapeDtypeStruct((depth, PROJ_WIDTH, d), BF16),
        compiler_params=_params(2),
        name="w_in_prep",
    )(jnp.swapaxes(w_in, 1, 2))


def _pad_gate_up(up, row0):
    out = jnp.zeros((up.shape[0], LANES, up.shape[-1]), BF16)
    return out.at[:, row0:row0 + GLA_GATE_RANK, :].set(up.astype(BF16))


def kernel(x, mem, norm_mix, w_in, gla_gate_up_fwd, gla_gate_bias_fwd, gla_gate_up_bwd, gla_gate_bias_bwd, gla_norm, rel_bias, dil_norm, mem_norm, w_mem_kv, mem_out_norm, w_out, norm_mlp, w_up, w_down, norm_final):
    batch, seq, d = x.shape
    assert batch == 1 and seq == SEQ and d == D_MODEL
    xs = x.reshape(seq, d)
    row = lambda g: g.reshape(g.shape[0], 1, g.shape[-1])

    w_in_r = _reorder_w_in(w_in)
    up_f = _pad_gate_up(gla_gate_up_fwd, 0)
    up_b = _pad_gate_up(gla_gate_up_bwd, GLA_GATE_RANK)
    dil_bias = _dil_bias_tiles(rel_bias)
    kv = _mem_kv(mem.reshape(MEM_LEN, d), row(mem_norm), w_mem_kv)

    for layer in range(DEPTH):
        proj = _in_proj(xs, row(norm_mix), w_in_r, layer)
        o_f, o_b, w_up_b = _gla(proj, up_f, up_b, row(gla_gate_bias_fwd), row(gla_gate_bias_bwd), w_up, layer)
        dil, w_out_b, w_down_b = _dil_attention(proj, dil_bias, row(dil_norm), w_out, w_down, layer)
        xs = _mix_out(xs, proj, o_f, o_b, dil, kv, row(gla_norm), row(mem_out_norm), w_out_b, layer)
        xs = _mlp(xs, row(norm_mlp), w_up_b, w_down_b, norm_final.reshape(1, d), layer,
                  final_norm=(layer == DEPTH - 1))
    return xs.reshape(batch, seq, d)
```

```python
import functools
import math

import jax
import jax.numpy as jnp
from jax import lax
from jax.experimental import pallas as pl
from jax.experimental.pallas import tpu as pltpu

F32 = jnp.float32
BF16 = jnp.bfloat16

D_MODEL = 2048
SEQ = 8192
DEPTH = 4
MEM_LEN = 256
GLA_HEADS = 4
GLA_DK = 128
GLA_DV = 256
GLA_GATE_RANK = 16
GLA_GATE_NORMALIZER = 16.0
GLA_CHUNK = 64
DIL_HEADS = 4
DIL_HEAD_DIM = 128
DIL_CONFIGS = ((128, 1), (512, 4), (2048, 16))
MEM_HEADS = 4
MEM_HEAD_DIM = 128
REL_BUCKETS = 32
REL_MAX_DISTANCE = 1024
D_FF = 4 * D_MODEL
EPS = 1e-6
NEG_INF = -1e30

GLA_QK_WIDTH = GLA_HEADS * GLA_DK
GLA_V_WIDTH = GLA_HEADS * GLA_DV
DIL_WIDTH = DIL_HEADS * DIL_HEAD_DIM
MEM_WIDTH = MEM_HEADS * MEM_HEAD_DIM
MIX_WIDTH = GLA_V_WIDTH + DIL_WIDTH + MEM_WIDTH

LANES = 128

COL_GQ = 0
COL_GK = COL_GQ + GLA_QK_WIDTH // LANES
COL_GV = COL_GK + GLA_QK_WIDTH // LANES
COL_GR = COL_GV + GLA_V_WIDTH // LANES
COL_DQ = COL_GR + GLA_V_WIDTH // LANES
COL_DK = COL_DQ + DIL_WIDTH // LANES
COL_DV = COL_DK + DIL_WIDTH // LANES
COL_MQ = COL_DV + DIL_WIDTH // LANES
COL_LR = COL_MQ + MEM_WIDTH // LANES
PROJ_COLS = COL_LR + 2
PROJ_WIDTH = PROJ_COLS * LANES

GLA_BLOCK = 256

DIL_BAND = 64
DIL_TQ = 128
DIL_TK = DIL_TQ + 2 * DIL_BAND
DIL_DILATIONS = tuple(d for _, d in DIL_CONFIGS)
DIL_TB = DIL_TQ * max(DIL_DILATIONS)
DIL_INTERLEAVE = 8

VMEM_LIMIT = 48 * 1024 * 1024
BIG_VMEM_LIMIT = 60 * 1024 * 1024


def _params(n_axes, vmem=VMEM_LIMIT):
    return pltpu.CompilerParams(dimension_semantics=("arbitrary",) * n_axes, vmem_limit_bytes=vmem)


def _dot(a, b):
    return jnp.dot(a, b, preferred_element_type=F32)


def _dot_nt(a, b):
    return lax.dot_general(a, b, (((1,), (1,)), ((), ())), preferred_element_type=F32)


def _dot_tn(a, b):
    return lax.dot_general(a, b, (((0,), (0,)), ((), ())), preferred_element_type=F32)


def _rms(x, gain):
    return x * lax.rsqrt(jnp.mean(x * x, axis=-1, keepdims=True) + EPS) * gain


def _run_interleaved(stage_generators):
    live = list(stage_generators)
    while live:
        live = [gen for gen in live if next(gen, StopIteration) is not StopIteration]


def _in_proj_kernel(x_ref, g_ref, w_ref, o_ref, h_ref, *, n_sub):
    j = pl.program_id(1)
    sub = x_ref.shape[0] // n_sub

    @pl.when(j == 0)
    def _():
        for c in range(n_sub):
            rows = slice(c * sub, (c + 1) * sub)
            h = _rms(x_ref[rows, :], g_ref[...]).astype(h_ref.dtype)
            h_ref[rows, :] = h
            o_ref[rows, :] = _dot_nt(h, w_ref[...]).astype(o_ref.dtype)

    @pl.when(j != 0)
    def _():
        for c in range(n_sub):
            rows = slice(c * sub, (c + 1) * sub)
            o_ref[rows, :] = _dot_nt(h_ref[rows, :], w_ref[...]).astype(o_ref.dtype)


def _in_proj(x, gain, w, layer, tm=1024, tn=1792, n_sub=4):
    s, d = x.shape
    n = w.shape[0]
    return pl.pallas_call(
        functools.partial(_in_proj_kernel, n_sub=n_sub),
        grid=(s // tm, n // tn),
        in_specs=[
            pl.BlockSpec((tm, d), lambda i, j: (i, 0)),
            pl.BlockSpec((None, 1, d), lambda i, j: (layer, 0, 0)),
            pl.BlockSpec((tn, d), lambda i, j: (j, 0)),
        ],
        out_specs=pl.BlockSpec((tm, tn), lambda i, j: (i, j)),
        out_shape=jax.ShapeDtypeStruct((s, n), BF16),
        scratch_shapes=[pltpu.VMEM((tm, d), BF16)],
        compiler_params=_params(2),
        name="in_proj",
    )(x, gain, w)


def _log_sigmoid(x):
    return jnp.minimum(x, 0.0) - jnp.log(1.0 + jnp.exp(-jnp.abs(x)))


GLA_LEVELS = (GLA_BLOCK // GLA_CHUNK).bit_length() - 1


def _gla_tables(tri_ref, lvl_ref, reverse):
    t = GLA_BLOCK
    row = lax.broadcasted_iota(jnp.int32, (t, t), 0)
    col = lax.broadcasted_iota(jnp.int32, (t, t), 1)
    valid = (col >= row) if reverse else (col <= row)
    tri_ref[...] = jnp.where(valid, 1.0, 0.0).astype(tri_ref.dtype)
    shift = GLA_CHUNK.bit_length() - 1
    lvl_ref[...] = jnp.where(jnp.right_shift(row, shift) == jnp.right_shift(col, shift),
                             jnp.where(valid, 0, -1), -1)


def _gla_boundary(cum, group, reverse):
    t = cum.shape[0]
    pieces = []
    for a in range(0, t, group):
        idx = a + group if reverse else a - 1
        if idx < 0 or idx >= t:
            ref_row = jnp.zeros((1, cum.shape[1]), cum.dtype)
        else:
            ref_row = cum[idx:idx + 1]
        pieces.append(jnp.broadcast_to(ref_row, (group, cum.shape[1])))
    return jnp.concatenate(pieces, axis=0)


def _place_rows(part, row0, total_rows):
    pieces = []
    if row0 > 0:
        pieces.append(jnp.zeros((row0, part.shape[1]), part.dtype))
    pieces.append(part)
    rest = total_rows - row0 - part.shape[0]
    if rest > 0:
        pieces.append(jnp.zeros((rest, part.shape[1]), part.dtype))
    return jnp.concatenate(pieces, axis=0)


def _gla_block(q_ref, k_ref, v_ref, lr_ref, up_ref, bias_ref, o_ref, s_ref, tri_ref, lvl_ref, r0, reverse):
    t = GLA_BLOCK
    rows = slice(r0, r0 + t)
    logits = _dot(lr_ref[rows, :], up_ref[...]) + bias_ref[...]
    yield
    g = _log_sigmoid(logits) * (1.0 / GLA_GATE_NORMALIZER)
    g_hi = g.astype(BF16)
    g_lo = (g - g_hi.astype(F32)).astype(BF16)
    cum2 = _dot(tri_ref[...], jnp.concatenate([g_hi, g_lo], axis=1))
    yield
    cum = cum2[:, :GLA_DK] + cum2[:, GLA_DK:]

    q = q_ref[rows, :].astype(F32) * (GLA_DK ** -0.5)
    k = k_ref[rows, :].astype(F32)
    v = v_ref[rows, :]
    lvl = lvl_ref[...]

    ref0 = _gla_boundary(cum, GLA_CHUNK, reverse)
    q_dec = (q * jnp.exp(cum - ref0)).astype(BF16)
    k_inv = (k * jnp.exp(ref0 - cum)).astype(BF16)
    q_segs, k_segs = [], []
    for level in range(1, GLA_LEVELS + 1):
        group = GLA_CHUNK << level
        for a0 in range(0, t, group):
            mid = a0 + group // 2
            att = slice(a0, mid) if reverse else slice(mid, a0 + group)
            src = slice(mid, a0 + group) if reverse else slice(a0, mid)
            ref_row = cum[mid:mid + 1] if reverse else cum[mid - 1:mid]
            q_part = (q[att] * jnp.exp(cum[att] - ref_row)).astype(BF16)
            k_part = (k[src] * jnp.exp(ref_row - cum[src])).astype(BF16)
            q_segs.append(_place_rows(q_part, att.start, t))
            k_segs.append(_place_rows(k_part, src.start, t))
    cross = _dot_nt(jnp.concatenate(q_segs, axis=1), jnp.concatenate(k_segs, axis=1))
    diag = _dot_nt(q_dec, k_inv)
    total = cum[0:1] if reverse else cum[t - 1:t]
    q_in = (q * jnp.exp(cum)).astype(BF16)
    k_out = (k * jnp.exp(total - cum)).astype(BF16)
    carry_in = _dot_tn(v, k_out)
    yield
    a = jnp.where(lvl == 0, diag, cross)
    o_local = _dot(a.astype(BF16), v)
    yield
    state = s_ref[...]
    o = o_local + _dot_nt(q_in, state.astype(BF16))
    o_ref[rows, :] = o.astype(o_ref.dtype)
    s_ref[...] = state * jnp.exp(total) + carry_in


def _cast_slab(weights, layer, n_steps, step_of):
    rows, cols = weights.shape[1:]
    slab = rows // n_steps
    return (pl.BlockSpec((None, slab, cols), lambda *g: (layer, step_of(*g), 0)),
            pl.BlockSpec((slab, cols), lambda *g: (step_of(*g), 0)),
            jax.ShapeDtypeStruct((rows, cols), BF16))


def _gla_kernel(qf, kf, vf, lrf, qb, kb, vb, lrb, upf, upb, bsf, bsb, wu_ref, of, ob, wu_out,
                sf, sb, tri_f, tri_b, lvl_f, lvl_b, *, interleave):
    @pl.when(pl.program_id(1) == 0)
    def _():
        sf[...] = jnp.zeros_like(sf)
        sb[...] = jnp.zeros_like(sb)
        _gla_tables(tri_f, lvl_f, reverse=False)
        _gla_tables(tri_b, lvl_b, reverse=True)

    starts = list(range(0, qf.shape[0], GLA_BLOCK))
    n_groups = len(starts) // interleave

    def cast_piece(piece):
        n = wu_ref.shape[0] // n_groups
        wu_out[piece * n:(piece + 1) * n, :] = wu_ref[piece * n:(piece + 1) * n, :].astype(wu_out.dtype)

    for gi in range(n_groups):
        blocks = []
        for r_fwd, r_bwd in list(zip(starts, reversed(starts)))[gi * interleave:(gi + 1) * interleave]:
            blocks.append(_gla_block(qf, kf, vf, lrf, upf, bsf, of, sf, tri_f, lvl_f, r_fwd, reverse=False))
            blocks.append(_gla_block(qb, kb, vb, lrb, upb, bsb, ob, sb, tri_b, lvl_b, r_bwd, reverse=True))
        _run_interleaved(blocks)
        cast_piece(gi)


def _gla(proj, up_f, up_b, bias_f, bias_b, w_up, layer, t=1024, interleave=4):
    s = proj.shape[0]
    nb = s // t
    vcol = COL_GV * LANES // GLA_DV
    wu_in, wu_out, wu_sds = _cast_slab(w_up, layer, GLA_HEADS * nb, lambda h, i: h * nb + i)

    def row_specs(rmap):
        return [
            pl.BlockSpec((t, GLA_DK), lambda h, i: (rmap(i), COL_GQ + h)),
            pl.BlockSpec((t, GLA_DK), lambda h, i: (rmap(i), COL_GK + h)),
            pl.BlockSpec((t, GLA_DV), lambda h, i: (rmap(i), vcol + h)),
            pl.BlockSpec((t, LANES), lambda h, i: (rmap(i), COL_LR)),
        ]

    fwd = lambda i: i
    bwd = lambda i: nb - 1 - i
    up_spec = pl.BlockSpec((None, LANES, GLA_DK), lambda h, i: (layer, 0, h))
    bias_spec = pl.BlockSpec((None, 1, GLA_DK), lambda h, i: (layer, 0, h))
    out_sds = jax.ShapeDtypeStruct((s, GLA_V_WIDTH), BF16)
    state = pltpu.VMEM((GLA_DV, GLA_DK), F32)
    tri = pltpu.VMEM((GLA_BLOCK, GLA_BLOCK), BF16)
    lvl = pltpu.VMEM((GLA_BLOCK, GLA_BLOCK), jnp.int32)
    return pl.pallas_call(
        functools.partial(_gla_kernel, interleave=interleave),
        grid=(GLA_HEADS, nb),
        in_specs=row_specs(fwd) + row_specs(bwd) + [up_spec, up_spec, bias_spec, bias_spec, wu_in],
        out_specs=[
            pl.BlockSpec((t, GLA_DV), lambda h, i: (fwd(i), h)),
            pl.BlockSpec((t, GLA_DV), lambda h, i: (bwd(i), h)),
            wu_out,
        ],
        out_shape=[out_sds, out_sds, wu_sds],
        scratch_shapes=[state, state, tri, tri, lvl, lvl],
        compiler_params=_params(2),
        name="gla_scan",
    )(proj, proj, proj, proj, proj, proj, proj, proj, up_f, up_b, bias_f, bias_b, w_up)


def _dil_tile(q, k, v, bias, edge, o_dst, l_dst, rows):
    raw = _dot_nt(q, k)
    yield
    sc = raw * (DIL_HEAD_DIM ** -0.5) + bias
    if edge is not None:
        sc = sc + edge
    m = jnp.max(sc, axis=-1, keepdims=True)
    p = jnp.exp(sc - m)
    den = jnp.sum(p, axis=-1, keepdims=True)
    pv = _dot(p.astype(BF16), v)
    yield
    o = pv / den
    o_dst[rows, :] = o
    l_dst[rows, :] = jnp.broadcast_to(m + jnp.log(den), o.shape)


def _dil_kernel(q_ref, k_ref, v_ref, bias_ref, gain_ref, wo_ref, wd_ref, out_ref, wo_out, wd_out,
                stage, stage_mid, kg1, vg1, kg4, vg4, kg16, vg16, qp4, qp16, o1, o4, o16, l1, l4, l16):
    i = pl.program_id(1)
    s = k_ref.shape[0]
    tb = q_ref.shape[0]
    k_bufs = dict(zip(DIL_DILATIONS, (kg1, kg4, kg16)))
    v_bufs = dict(zip(DIL_DILATIONS, (vg1, vg4, vg16)))
    q_perm = dict(zip(DIL_DILATIONS, (None, qp4, qp16)))
    o_bufs = dict(zip(DIL_DILATIONS, (o1, o4, o16)))
    l_bufs = dict(zip(DIL_DILATIONS, (l1, l4, l16)))
    _, d_mid, d_max = DIL_DILATIONS
    assert d_max == d_mid * d_mid

    def deinterleave(dst_mid, dst_max, base_mid, base_max, residue_rows):
        per_mid, per_max = tb // d_mid, tb // d_max

        def at(base, offset):
            start = base + offset
            return start if isinstance(start, int) else pl.multiple_of(start, DIL_BAND)

        for r in range(d_mid):
            part = stage[pl.ds(r, per_mid, stride=d_mid), :]
            stage_mid[r * per_mid:(r + 1) * per_mid, :] = part
            dst_mid[pl.ds(at(base_mid, r * (residue_rows // d_mid)), per_mid), :] = part.astype(dst_mid.dtype)
        for r in range(d_max):
            r_mid, m = r % d_mid, r // d_mid
            part = stage_mid[pl.ds(r_mid * per_mid + m, per_max, stride=d_mid), :]
            dst_max[pl.ds(at(base_max, r * (residue_rows // d_max)), per_max), :] = part.astype(dst_max.dtype)

    @pl.when(i == 0)
    def _():
        zeros = jnp.zeros((DIL_BAND, DIL_HEAD_DIM), BF16)
        for buf in (kg1, vg1, kg4, vg4, kg16, vg16):
            buf[0:DIL_BAND, :] = zeros
            buf[DIL_BAND + s:, :] = zeros
        kg1[DIL_BAND:DIL_BAND + s, :] = k_ref[...]
        vg1[DIL_BAND:DIL_BAND + s, :] = v_ref[...]

        def body(c, carry):
            base = pl.multiple_of(c * tb, tb)
            for src, bufs in ((k_ref, k_bufs), (v_ref, v_bufs)):
                stage[...] = src[pl.ds(base, tb), :].astype(F32)
                deinterleave(bufs[d_mid], bufs[d_max], DIL_BAND + c * (tb // d_mid),
                             DIL_BAND + c * (tb // d_max), s)
            return carry

        lax.fori_loop(0, s // tb, body, 0)

    stage[...] = q_ref[...].astype(F32)
    deinterleave(q_perm[d_mid], q_perm[d_max], 0, 0, tb)

    lane = lax.broadcasted_iota(jnp.int32, (1, DIL_TK), 1)
    tiles = []
    for b, d in enumerate(DIL_DILATIONS):
        sub_len = s // d
        per = tb // d
        n_tiles = per // DIL_TQ
        for r in range(d):
            for tj in range(n_tiles):
                tiles.append((b, d, sub_len, per, n_tiles, r, tj))

    def tile_stages(b, d, sub_len, per, n_tiles, r, tj):
        u0 = i * per + tj * DIL_TQ
        q_src = q_ref if d == 1 else q_perm[d]
        q = q_src[r * per + tj * DIL_TQ:r * per + (tj + 1) * DIL_TQ, :]
        k_start = pl.multiple_of(r * sub_len + u0, DIL_TQ)
        k = k_bufs[d][pl.ds(k_start, DIL_TK), :]
        v = v_bufs[d][pl.ds(k_start, DIL_TK), :]
        edge = None
        if tj == 0 or tj == n_tiles - 1:
            key_pos = u0 - DIL_BAND + lane
            edge = jnp.where((key_pos >= 0) & (key_pos < sub_len), 0.0, NEG_INF)
        if d == 1:
            rows = pl.ds(tj * DIL_TQ, DIL_TQ)
        else:
            rows = pl.ds(r + d * tj * DIL_TQ, DIL_TQ, stride=d)
        return _dil_tile(q, k, v, bias_ref[b], edge, o_bufs[d], l_bufs[d], rows)

    for g0 in range(0, len(tiles), DIL_INTERLEAVE):
        _run_interleaved([tile_stages(*tile) for tile in tiles[g0:g0 + DIL_INTERLEAVE]])

    lses = [l_bufs[d][...] for d in DIL_DILATIONS]
    m = functools.reduce(jnp.maximum, lses)
    es = [jnp.exp(l - m) for l in lses]
    num = sum(e * o_bufs[d][...] for e, d in zip(es, DIL_DILATIONS))
    mixed = num / sum(es)
    out_ref[...] = _head_rms(mixed, gain_ref[...]).astype(out_ref.dtype)

    wo_out[...] = wo_ref[...].astype(wo_out.dtype)
    wd_out[...] = wd_ref[...].astype(wd_out.dtype)


def _dil_attention(proj, bias, gain, w_out, w_down, layer):
    s = proj.shape[0]
    tb = DIL_TB
    nq = s // tb
    step_of = lambda h, i: h * nq + i
    wo_in, wo_out, wo_sds = _cast_slab(w_out, layer, DIL_HEADS * nq, step_of)
    wd_in, wd_out, wd_sds = _cast_slab(w_down, layer, DIL_HEADS * nq, step_of)
    guarded = pltpu.VMEM((s + 2 * DIL_BAND, DIL_HEAD_DIM), BF16)
    tile_f32 = pltpu.VMEM((tb, DIL_HEAD_DIM), F32)
    tile_bf16 = pltpu.VMEM((tb, DIL_HEAD_DIM), BF16)
    return pl.pallas_call(
        _dil_kernel,
        grid=(DIL_HEADS, nq),
        in_specs=[
            pl.BlockSpec((tb, DIL_HEAD_DIM), lambda h, i: (i, COL_DQ + h)),
            pl.BlockSpec((s, DIL_HEAD_DIM), lambda h, i: (0, COL_DK + h)),
            pl.BlockSpec((s, DIL_HEAD_DIM), lambda h, i: (0, COL_DV + h)),
            pl.BlockSpec((len(DIL_DILATIONS), None, DIL_TQ, DIL_TK), lambda h, i: (0, h, 0, 0)),
            pl.BlockSpec((None, 1, DIL_HEAD_DIM), lambda h, i: (layer, 0, h)),
            wo_in, wd_in,
        ],
        out_specs=[pl.BlockSpec((tb, DIL_HEAD_DIM), lambda h, i: (i, h)), wo_out, wd_out],
        out_shape=[jax.ShapeDtypeStruct((s, DIL_WIDTH), BF16), wo_sds, wd_sds],
        scratch_shapes=[tile_f32] * 2 + [guarded] * 6 + [tile_bf16] * 2 + [tile_f32] * 6,
        compiler_params=_params(2, BIG_VMEM_LIMIT),
        name="dil_attn",
    )(proj, proj, proj, bias, gain, w_out, w_down)


def _t5_bucket(rel):
    half = REL_BUCKETS // 2
    max_exact = half // 2
    ret = jnp.where(rel > 0, half, 0)
    n = jnp.abs(rel)
    nf = jnp.maximum(n, 1).astype(F32)
    large = max_exact + (jnp.log(nf / max_exact) / math.log(REL_MAX_DISTANCE / max_exact)
                         * (half - max_exact)).astype(jnp.int32)
    large = jnp.minimum(large, half - 1)
    return ret + jnp.where(n < max_exact, n, large)


def _dil_bias_tiles(rel_bias):
    rel = jnp.arange(DIL_TK)[None, :] - DIL_BAND - jnp.arange(DIL_TQ)[:, None]
    band = jnp.abs(rel) <= DIL_BAND
    tiles = []
    for d in DIL_DILATIONS:
        onehot = (_t5_bucket(rel * d)[..., None] == jnp.arange(REL_BUCKETS)).astype(F32)
        bias = jnp.einsum("qkb,bh->hqk", onehot, rel_bias.astype(F32), precision=lax.Precision.HIGHEST)
        tiles.append(jnp.where(band[None], bias, NEG_INF))
    return jnp.stack(tiles, axis=0)


def _mem_kv_kernel(mem_ref, g_ref, w_ref, o_ref):
    h = _rms(mem_ref[...], g_ref[...]).astype(BF16)
    o_ref[...] = _dot(h, w_ref[...].astype(BF16)).astype(o_ref.dtype)


def _mem_kv(mem, gain, w):
    m, d = mem.shape
    n = w.shape[-1]
    return pl.pallas_call(
        _mem_kv_kernel,
        grid=(DEPTH,),
        in_specs=[
            pl.BlockSpec((m, d), lambda l: (0, 0)),
            pl.BlockSpec((None, 1, d), lambda l: (l, 0, 0)),
            pl.BlockSpec((None, d, n), lambda l: (l, 0, 0)),
        ],
        out_specs=pl.BlockSpec((None, m, n), lambda l: (l, 0, 0)),
        out_shape=jax.ShapeDtypeStruct((DEPTH, m, n), BF16),
        compiler_params=_params(1),
        name="mem_kv",
    )(mem, gain, w)


def _head_rms(o, gain):
    return o * lax.rsqrt(jnp.mean(o * o, axis=-1, keepdims=True) + EPS) * gain


def _mix_kernel(of_ref, ob_ref, gr_ref, dil_ref, mq_ref, km_ref, vm_ref, ggla_ref, gmem_ref, w_ref, x_ref,
                *rest, prep_next):
    if prep_next:
        w_in_ref, out_ref, w_in_out, gla_ref, mem_ref = rest
        _w_in_prep_kernel(w_in_ref, w_in_out)
    else:
        out_ref, gla_ref, mem_ref = rest

    dil0 = GLA_V_WIDTH
    mem0 = GLA_V_WIDTH + DIL_WIDTH

    def mem_head(h):
        sl = slice(h * MEM_HEAD_DIM, (h + 1) * MEM_HEAD_DIM)
        raw = _dot_nt(mq_ref[:, sl], km_ref[:, sl])
        yield
        sc = raw * (MEM_HEAD_DIM ** -0.5)
        m = jnp.max(sc, axis=-1, keepdims=True)
        p = jnp.exp(sc - m)
        den = jnp.sum(p, axis=-1, keepdims=True)
        pv = _dot(p.astype(BF16), vm_ref[:, sl])
        yield
        mem_ref[:, sl] = _head_rms(pv / den, gmem_ref[:, sl]).astype(mem_ref.dtype)

    def gla_heads():
        for h in range(GLA_HEADS):
            sl = slice(h * GLA_DV, (h + 1) * GLA_DV)
            o = of_ref[:, sl].astype(F32) + ob_ref[:, sl].astype(F32)
            r = gr_ref[:, sl].astype(F32)
            gate = r * (1.0 / (1.0 + jnp.exp(-r)))
            gla_ref[:, sl] = (_head_rms(o, ggla_ref[:, sl]) * gate).astype(gla_ref.dtype)
        yield
        out_ref[...] += _dot(gla_ref[...], w_ref[0:dil0, :])

    def dil_group():
        out_ref[...] = x_ref[...] + _dot(dil_ref[...], w_ref[dil0:mem0, :])
        yield

    _run_interleaved([mem_head(h) for h in range(MEM_HEADS)] + [dil_group(), gla_heads()])
    out_ref[...] += _dot(mem_ref[...], w_ref[mem0:, :])


def _mix_out(x, proj, o_f, o_b, dil, kv, g_gla, g_mem, w_out, w_in_t, layer, t=512):
    s, d = x.shape
    n_steps = s // t
    prep_next = layer + 1 < w_in_t.shape[0]
    rows = lambda w, c=0: pl.BlockSpec((t, w), lambda i: (i, c))
    per_layer = lambda shape, c=0: pl.BlockSpec((None,) + shape, lambda i: (layer, 0, c),
                                                pipeline_mode=pl.Buffered(1))
    in_specs = [
        rows(GLA_V_WIDTH), rows(GLA_V_WIDTH),
        rows(GLA_V_WIDTH, COL_GR * LANES // GLA_V_WIDTH),
        rows(DIL_WIDTH),
        rows(MEM_WIDTH, COL_MQ * LANES // MEM_WIDTH),
        per_layer((MEM_LEN, MEM_WIDTH), 0), per_layer((MEM_LEN, MEM_WIDTH), 1),
        per_layer((1, GLA_V_WIDTH)), per_layer((1, MEM_WIDTH)),
        pl.BlockSpec((MIX_WIDTH, d), lambda i: (0, 0), pipeline_mode=pl.Buffered(1)),
        rows(d),
    ]
    operands = [o_f, o_b, proj, dil, proj, kv, kv, g_gla, g_mem, w_out, x]
    out_specs = [rows(d)]
    out_shape = [jax.ShapeDtypeStruct((s, d), F32)]
    if prep_next:
        n_in = w_in_t.shape[1]
        in_specs.append(pl.BlockSpec((None, n_in, d // n_steps), lambda i: (layer + 1, 0, i)))
        operands.append(w_in_t)
        out_specs.append(pl.BlockSpec((PROJ_WIDTH, d // n_steps), lambda i: (0, i)))
        out_shape.append(jax.ShapeDtypeStruct((PROJ_WIDTH, d), BF16))
    outs = pl.pallas_call(
        functools.partial(_mix_kernel, prep_next=prep_next),
        grid=(n_steps,),
        in_specs=in_specs,
        out_specs=out_specs,
        out_shape=out_shape,
        scratch_shapes=[pltpu.VMEM((t, GLA_V_WIDTH), BF16), pltpu.VMEM((t, MEM_WIDTH), BF16)],
        compiler_params=_params(1),
        name="mix_out_proj",
    )(*operands)
    return (outs[0], outs[1]) if prep_next else (outs[0], None)


def _mlp_kernel(x_ref, g_ref, wu_ref, wd_ref, gf_ref, o_ref, h_ref, *, final_norm, n_sub):
    f = pl.program_id(1)
    sub = x_ref.shape[0] // n_sub

    def hidden(h):
        return jnp.square(jnp.maximum(_dot(h, wu_ref[...]), 0.0)).astype(BF16)

    @pl.when(f == 0)
    def _():
        for c in range(n_sub):
            rows = slice(c * sub, (c + 1) * sub)
            x = x_ref[rows, :]
            h = _rms(x, g_ref[...]).astype(h_ref.dtype)
            h_ref[rows, :] = h
            o_ref[rows, :] = x + _dot(hidden(h), wd_ref[...])

    @pl.when(f != 0)
    def _():
        for c in range(n_sub):
            rows = slice(c * sub, (c + 1) * sub)
            o_ref[rows, :] += _dot(hidden(h_ref[rows, :]), wd_ref[...])

    if final_norm:
        @pl.when(f == pl.num_programs(1) - 1)
        def _():
            for c in range(n_sub):
                rows = slice(c * sub, (c + 1) * sub)
                o_ref[rows, :] = _rms(o_ref[rows, :], gf_ref[...])


def _mlp(x, gain, w_up, w_down, gain_final, layer, final_norm, tm=1024, tf=1024, n_sub=4):
    s, d = x.shape
    ff = w_up.shape[-1]
    return pl.pallas_call(
        functools.partial(_mlp_kernel, final_norm=final_norm, n_sub=n_sub),
        grid=(s // tm, ff // tf),
        in_specs=[
            pl.BlockSpec((tm, d), lambda i, f: (i, 0)),
            pl.BlockSpec((None, 1, d), lambda i, f: (layer, 0, 0)),
            pl.BlockSpec((d, tf), lambda i, f: (0, f)),
            pl.BlockSpec((tf, d), lambda i, f: (f, 0)),
            pl.BlockSpec((1, d), lambda i, f: (0, 0)),
        ],
        out_specs=pl.BlockSpec((tm, d), lambda i, f: (i, 0)),
        out_shape=jax.ShapeDtypeStruct((s, d), F32),
        scratch_shapes=[pltpu.VMEM((tm, d), BF16)],
        compiler_params=_params(2, BIG_VMEM_LIMIT),
        name="mlp",
    )(x, gain, w_up, w_down, gain_final)


def _w_in_prep_kernel(w_ref, o_ref):
    lr0 = 2 * GLA_QK_WIDTH + 2 * GLA_V_WIDTH
    lr_w = 2 * GLA_GATE_RANK
    tail = w_ref.shape[0] - lr0 - lr_w
    o_ref[0:lr0, :] = w_ref[0:lr0, :].astype(o_ref.dtype)
    o_ref[lr0:lr0 + tail, :] = w_ref[lr0 + lr_w:, :].astype(o_ref.dtype)
    o_ref[lr0 + tail:lr0 + tail + lr_w, :] = w_ref[lr0:lr0 + lr_w, :].astype(o_ref.dtype)
    o_ref[lr0 + tail + lr_w:, :] = jnp.zeros((o_ref.shape[0] - lr0 - tail - lr_w, o_ref.shape[1]), o_ref.dtype)


def _reorder_w_in(w_in_t, layer, cols=256):
    _, n, d = w_in_t.shape
    assert COL_LR * LANES == n - 2 * GLA_GATE_RANK
    return pl.pallas_call(
        _w_in_prep_kernel,
        grid=(d // cols,),
        in_specs=[pl.BlockSpec((None, n, cols), lambda i: (layer, 0, i))],
        out_specs=pl.BlockSpec((PROJ_WIDTH, cols), lambda i: (0, i)),
        out_shape=jax.ShapeDtypeStruct((PROJ_WIDTH, d), BF16),
        compiler_params=_params(1),
        name="w_in_prep",
    )(w_in_t)


def _pad_gate_up(up, row0):
    out = jnp.zeros((up.shape[0], LANES, up.shape[-1]), BF16)
    return out.at[:, row0:row0 + GLA_GATE_RANK, :].set(up.astype(BF16))


def kernel(x, mem, norm_mix, w_in, gla_gate_up_fwd, gla_gate_bias_fwd, gla_gate_up_bwd, gla_gate_bias_bwd, gla_norm, rel_bias, dil_norm, mem_norm, w_mem_kv, mem_out_norm, w_out, norm_mlp, w_up, w_down, norm_final):
    batch, seq, d = x.shape
    assert batch == 1 and seq == SEQ and d == D_MODEL
    xs = x.reshape(seq, d)
    row = lambda g: g.reshape(g.shape[0], 1, g.shape[-1])

    w_in_t = jnp.swapaxes(w_in, 1, 2)
    w_in_l = _reorder_w_in(w_in_t, 0)
    up_f = _pad_gate_up(gla_gate_up_fwd, 0)
    up_b = _pad_gate_up(gla_gate_up_bwd, GLA_GATE_RANK)
    dil_bias = _dil_bias_tiles(rel_bias)
    kv = _mem_kv(mem.reshape(MEM_LEN, d), row(mem_norm), w_mem_kv)

    for layer in range(DEPTH):
        proj = _in_proj(xs, row(norm_mix), w_in_l, layer)
        o_f, o_b, w_up_b = _gla(proj, up_f, up_b, row(gla_gate_bias_fwd), row(gla_gate_bias_bwd), w_up, layer)
        dil, w_out_b, w_down_b = _dil_attention(proj, dil_bias, row(dil_norm), w_out, w_down, layer)
        xs, w_in_l = _mix_out(xs, proj, o_f, o_b, dil, kv, row(gla_norm), row(mem_out_norm), w_out_b,
                              w_in_t, layer)
        xs = _mlp(xs, row(norm_mlp), w_up_b, w_down_b, norm_final.reshape(1, d), layer,
                  final_norm=(layer == DEPTH - 1))
    return xs.reshape(batch, seq, d)
```

```python
import functools
import math

import jax
import jax.numpy as jnp
from jax import lax
from jax.experimental import pallas as pl
from jax.experimental.pallas import tpu as pltpu

F32 = jnp.float32
BF16 = jnp.bfloat16

D_MODEL = 2048
SEQ = 8192
DEPTH = 4
MEM_LEN = 256
GLA_HEADS = 4
GLA_DK = 128
GLA_DV = 256
GLA_GATE_RANK = 16
GLA_GATE_NORMALIZER = 16.0
GLA_CHUNK = 64
DIL_HEADS = 4
DIL_HEAD_DIM = 128
DIL_CONFIGS = ((128, 1), (512, 4), (2048, 16))
MEM_HEADS = 4
MEM_HEAD_DIM = 128
REL_BUCKETS = 32
REL_MAX_DISTANCE = 1024
D_FF = 4 * D_MODEL
EPS = 1e-6
NEG_INF = -1e30

GLA_QK_WIDTH = GLA_HEADS * GLA_DK
GLA_V_WIDTH = GLA_HEADS * GLA_DV
DIL_WIDTH = DIL_HEADS * DIL_HEAD_DIM
MEM_WIDTH = MEM_HEADS * MEM_HEAD_DIM
MIX_WIDTH = GLA_V_WIDTH + DIL_WIDTH + MEM_WIDTH

LANES = 128

COL_GQ = 0
COL_GK = COL_GQ + GLA_QK_WIDTH // LANES
COL_GV = COL_GK + GLA_QK_WIDTH // LANES
COL_GR = COL_GV + GLA_V_WIDTH // LANES
COL_DQ = COL_GR + GLA_V_WIDTH // LANES
COL_DK = COL_DQ + DIL_WIDTH // LANES
COL_DV = COL_DK + DIL_WIDTH // LANES
COL_MQ = COL_DV + DIL_WIDTH // LANES
COL_LR = COL_MQ + MEM_WIDTH // LANES
PROJ_COLS = COL_LR + 2
PROJ_WIDTH = PROJ_COLS * LANES

GLA_BLOCK = 256

DIL_BAND = 64
DIL_TQ = 128
DIL_TK = DIL_TQ + 2 * DIL_BAND
DIL_DILATIONS = tuple(d for _, d in DIL_CONFIGS)
DIL_TB = DIL_TQ * max(DIL_DILATIONS)
DIL_INTERLEAVE = 8

VMEM_LIMIT = 48 * 1024 * 1024
BIG_VMEM_LIMIT = 60 * 1024 * 1024


def _params(n_axes, vmem=VMEM_LIMIT):
    return pltpu.CompilerParams(dimension_semantics=("arbitrary",) * n_axes, vmem_limit_bytes=vmem)


def _dot(a, b):
    return jnp.dot(a, b, preferred_element_type=F32)


def _dot_nt(a, b):
    return lax.dot_general(a, b, (((1,), (1,)), ((), ())), preferred_element_type=F32)


def _dot_tn(a, b):
    return lax.dot_general(a, b, (((0,), (0,)), ((), ())), preferred_element_type=F32)


def _rms(x, gain):
    return x * lax.rsqrt(jnp.mean(x * x, axis=-1, keepdims=True) + EPS) * gain


def _run_interleaved(stage_generators):
    live = list(stage_generators)
    while live:
        live = [gen for gen in live if next(gen, StopIteration) is not StopIteration]


def _in_proj_kernel(x_ref, g_ref, w_ref, o_ref, h_ref, *, n_sub):
    j = pl.program_id(1)
    sub = x_ref.shape[0] // n_sub

    @pl.when(j == 0)
    def _():
        for c in range(n_sub):
            rows = slice(c * sub, (c + 1) * sub)
            h = _rms(x_ref[rows, :], g_ref[...]).astype(h_ref.dtype)
            h_ref[rows, :] = h
            o_ref[rows, :] = _dot_nt(h, w_ref[...]).astype(o_ref.dtype)

    @pl.when(j != 0)
    def _():
        for c in range(n_sub):
            rows = slice(c * sub, (c + 1) * sub)
            o_ref[rows, :] = _dot_nt(h_ref[rows, :], w_ref[...]).astype(o_ref.dtype)


def _in_proj(x, gain, w, layer, tm=1024, tn=1792, n_sub=4):
    s, d = x.shape
    n = w.shape[0]
    return pl.pallas_call(
        functools.partial(_in_proj_kernel, n_sub=n_sub),
        grid=(s // tm, n // tn),
        in_specs=[
            pl.BlockSpec((tm, d), lambda i, j: (i, 0)),
            pl.BlockSpec((None, 1, d), lambda i, j: (layer, 0, 0)),
            pl.BlockSpec((tn, d), lambda i, j: (j, 0)),
        ],
        out_specs=pl.BlockSpec((tm, tn), lambda i, j: (i, j)),
        out_shape=jax.ShapeDtypeStruct((s, n), BF16),
        scratch_shapes=[pltpu.VMEM((tm, d), BF16)],
        compiler_params=_params(2),
        name="in_proj",
    )(x, gain, w)


def _log_sigmoid(x):
    return jnp.minimum(x, 0.0) - jnp.log(1.0 + jnp.exp(-jnp.abs(x)))


GLA_LEVELS = (GLA_BLOCK // GLA_CHUNK).bit_length() - 1


def _gla_tables(tri_ref, lvl_ref, reverse):
    t = GLA_BLOCK
    row = lax.broadcasted_iota(jnp.int32, (t, t), 0)
    col = lax.broadcasted_iota(jnp.int32, (t, t), 1)
    valid = (col >= row) if reverse else (col <= row)
    tri_ref[...] = jnp.where(valid, 1.0, 0.0).astype(tri_ref.dtype)
    shift = GLA_CHUNK.bit_length() - 1
    lvl_ref[...] = jnp.where(jnp.right_shift(row, shift) == jnp.right_shift(col, shift),
                             jnp.where(valid, 0, -1), -1)


def _gla_boundary(cum, group, reverse):
    t = cum.shape[0]
    pieces = []
    for a in range(0, t, group):
        idx = a + group if reverse else a - 1
        if idx < 0 or idx >= t:
            ref_row = jnp.zeros((1, cum.shape[1]), cum.dtype)
        else:
            ref_row = cum[idx:idx + 1]
        pieces.append(jnp.broadcast_to(ref_row, (group, cum.shape[1])))
    return jnp.concatenate(pieces, axis=0)


def _place_rows(part, row0, total_rows):
    pieces = []
    if row0 > 0:
        pieces.append(jnp.zeros((row0, part.shape[1]), part.dtype))
    pieces.append(part)
    rest = total_rows - row0 - part.shape[0]
    if rest > 0:
        pieces.append(jnp.zeros((rest, part.shape[1]), part.dtype))
    return jnp.concatenate(pieces, axis=0)


def _gla_block(q_ref, k_ref, v_ref, lr_ref, up_ref, bias_ref, o_ref, s_ref, tri_ref, lvl_ref, r0, reverse):
    t = GLA_BLOCK
    rows = slice(r0, r0 + t)
    logits = _dot(lr_ref[rows, :], up_ref[...]) + bias_ref[...]
    yield
    g = _log_sigmoid(logits) * (1.0 / GLA_GATE_NORMALIZER)
    g_hi = g.astype(BF16)
    g_lo = (g - g_hi.astype(F32)).astype(BF16)
    cum2 = _dot(tri_ref[...], jnp.concatenate([g_hi, g_lo], axis=1))
    yield
    cum = cum2[:, :GLA_DK] + cum2[:, GLA_DK:]

    q = q_ref[rows, :].astype(F32) * (GLA_DK ** -0.5)
    k = k_ref[rows, :].astype(F32)
    v = v_ref[rows, :]
    lvl = lvl_ref[...]

    ref0 = _gla_boundary(cum, GLA_CHUNK, reverse)
    q_dec = (q * jnp.exp(cum - ref0)).astype(BF16)
    k_inv = (k * jnp.exp(ref0 - cum)).astype(BF16)
    q_segs, k_segs = [], []
    for level in range(1, GLA_LEVELS + 1):
        group = GLA_CHUNK << level
        for a0 in range(0, t, group):
            mid = a0 + group // 2
            att = slice(a0, mid) if reverse else slice(mid, a0 + group)
            src = slice(mid, a0 + group) if reverse else slice(a0, mid)
            ref_row = cum[mid:mid + 1] if reverse else cum[mid - 1:mid]
            q_part = (q[att] * jnp.exp(cum[att] - ref_row)).astype(BF16)
            k_part = (k[src] * jnp.exp(ref_row - cum[src])).astype(BF16)
            q_segs.append(_place_rows(q_part, att.start, t))
            k_segs.append(_place_rows(k_part, src.start, t))
    cross = _dot_nt(jnp.concatenate(q_segs, axis=1), jnp.concatenate(k_segs, axis=1))
    diag = _dot_nt(q_dec, k_inv)
    total = cum[0:1] if reverse else cum[t - 1:t]
    q_in = (q * jnp.exp(cum)).astype(BF16)
    k_out = (k * jnp.exp(total - cum)).astype(BF16)
    carry_in = _dot_tn(v, k_out)
    yield
    a = jnp.where(lvl == 0, diag, cross)
    o_local = _dot(a.astype(BF16), v)
    yield
    state = s_ref[...]
    o = o_local + _dot_nt(q_in, state.astype(BF16))
    o_ref[rows, :] = o.astype(o_ref.dtype)
    s_ref[...] = state * jnp.exp(total) + carry_in


def _cast_slab(weights, layer, n_steps, step_of):
    rows, cols = weights.shape[1:]
    slab = rows // n_steps
    return (pl.BlockSpec((None, slab, cols), lambda *g: (layer, step_of(*g), 0)),
            pl.BlockSpec((slab, cols), lambda *g: (step_of(*g), 0)),
            jax.ShapeDtypeStruct((rows, cols), BF16))


def _gla_kernel(qf, kf, vf, lrf, qb, kb, vb, lrb, upf, upb, bsf, bsb, wu_ref, of, ob, wu_out,
                sf, sb, tri_f, tri_b, lvl_f, lvl_b, *, interleave):
    @pl.when(pl.program_id(1) == 0)
    def _():
        sf[...] = jnp.zeros_like(sf)
        sb[...] = jnp.zeros_like(sb)
        _gla_tables(tri_f, lvl_f, reverse=False)
        _gla_tables(tri_b, lvl_b, reverse=True)

    starts = list(range(0, qf.shape[0], GLA_BLOCK))
    n_groups = len(starts) // interleave

    def cast_piece(piece):
        n = wu_ref.shape[0] // n_groups
        wu_out[piece * n:(piece + 1) * n, :] = wu_ref[piece * n:(piece + 1) * n, :].astype(wu_out.dtype)

    for gi in range(n_groups):
        blocks = []
        for r_fwd, r_bwd in list(zip(starts, reversed(starts)))[gi * interleave:(gi + 1) * interleave]:
            blocks.append(_gla_block(qf, kf, vf, lrf, upf, bsf, of, sf, tri_f, lvl_f, r_fwd, reverse=False))
            blocks.append(_gla_block(qb, kb, vb, lrb, upb, bsb, ob, sb, tri_b, lvl_b, r_bwd, reverse=True))
        _run_interleaved(blocks)
        cast_piece(gi)


def _gla(proj, up_f, up_b, bias_f, bias_b, w_up, layer, t=1024, interleave=4):
    s = proj.shape[0]
    nb = s // t
    vcol = COL_GV * LANES // GLA_DV
    wu_in, wu_out, wu_sds = _cast_slab(w_up, layer, GLA_HEADS * nb, lambda h, i: h * nb + i)

    def row_specs(rmap):
        return [
            pl.BlockSpec((t, GLA_DK), lambda h, i: (rmap(i), COL_GQ + h)),
            pl.BlockSpec((t, GLA_DK), lambda h, i: (rmap(i), COL_GK + h)),
            pl.BlockSpec((t, GLA_DV), lambda h, i: (rmap(i), vcol + h)),
            pl.BlockSpec((t, LANES), lambda h, i: (rmap(i), COL_LR)),
        ]

    fwd = lambda i: i
    bwd = lambda i: nb - 1 - i
    up_spec = pl.BlockSpec((None, LANES, GLA_DK), lambda h, i: (layer, 0, h))
    bias_spec = pl.BlockSpec((None, 1, GLA_DK), lambda h, i: (layer, 0, h))
    out_sds = jax.ShapeDtypeStruct((s, GLA_V_WIDTH), BF16)
    state = pltpu.VMEM((GLA_DV, GLA_DK), F32)
    tri = pltpu.VMEM((GLA_BLOCK, GLA_BLOCK), BF16)
    lvl = pltpu.VMEM((GLA_BLOCK, GLA_BLOCK), jnp.int32)
    return pl.pallas_call(
        functools.partial(_gla_kernel, interleave=interleave),
        grid=(GLA_HEADS, nb),
        in_specs=row_specs(fwd) + row_specs(bwd) + [up_spec, up_spec, bias_spec, bias_spec, wu_in],
        out_specs=[
            pl.BlockSpec((t, GLA_DV), lambda h, i: (fwd(i), h)),
            pl.BlockSpec((t, GLA_DV), lambda h, i: (bwd(i), h)),
            wu_out,
        ],
        out_shape=[out_sds, out_sds, wu_sds],
        scratch_shapes=[state, state, tri, tri, lvl, lvl],
        compiler_params=_params(2),
        name="gla_scan",
    )(proj, proj, proj, proj, proj, proj, proj, proj, up_f, up_b, bias_f, bias_b, w_up)


def _dil_tile(q, k, v, bias, edge, o_dst, l_dst, rows):
    raw = _dot_nt(q, k)
    yield
    sc = raw * (DIL_HEAD_DIM ** -0.5) + bias
    if edge is not None:
        sc = sc + edge
    m = jnp.max(sc, axis=-1, keepdims=True)
    p = jnp.exp(sc - m).astype(BF16)
    pv = _dot(p, jnp.concatenate([v, jnp.ones_like(v)], axis=1))
    yield
    den = pv[:, DIL_HEAD_DIM:]
    o_dst[rows, :] = pv[:, :DIL_HEAD_DIM] / den
    l_dst[rows, :] = m + jnp.log(den)


def _dil_kernel(q_ref, k_ref, v_ref, bias_ref, gain_ref, wo_ref, wd_ref, out_ref, wo_out, wd_out,
                stage, stage_mid, kg1, vg1, kg4, vg4, kg16, vg16, qp4, qp16, o1, o4, o16, l1, l4, l16):
    i = pl.program_id(1)
    s = k_ref.shape[0]
    tb = q_ref.shape[0]
    k_bufs = dict(zip(DIL_DILATIONS, (kg1, kg4, kg16)))
    v_bufs = dict(zip(DIL_DILATIONS, (vg1, vg4, vg16)))
    q_perm = dict(zip(DIL_DILATIONS, (None, qp4, qp16)))
    o_bufs = dict(zip(DIL_DILATIONS, (o1, o4, o16)))
    l_bufs = dict(zip(DIL_DILATIONS, (l1, l4, l16)))
    _, d_mid, d_max = DIL_DILATIONS
    assert d_max == d_mid * d_mid

    def deinterleave(dst_mid, dst_max, base_mid, base_max, residue_rows):
        per_mid, per_max = tb // d_mid, tb // d_max

        def at(base, offset):
            start = base + offset
            return start if isinstance(start, int) else pl.multiple_of(start, DIL_BAND)

        for r in range(d_mid):
            part = stage[pl.ds(r, per_mid, stride=d_mid), :]
            stage_mid[r * per_mid:(r + 1) * per_mid, :] = part
            dst_mid[pl.ds(at(base_mid, r * (residue_rows // d_mid)), per_mid), :] = part.astype(dst_mid.dtype)
        for r in range(d_max):
            r_mid, m = r % d_mid, r // d_mid
            part = stage_mid[pl.ds(r_mid * per_mid + m, per_max, stride=d_mid), :]
            dst_max[pl.ds(at(base_max, r * (residue_rows // d_max)), per_max), :] = part.astype(dst_max.dtype)

    @pl.when(i == 0)
    def _():
        zeros = jnp.zeros((DIL_BAND, DIL_HEAD_DIM), BF16)
        for buf in (kg1, vg1, kg4, vg4, kg16, vg16):
            buf[0:DIL_BAND, :] = zeros
            buf[DIL_BAND + s:, :] = zeros
        kg1[DIL_BAND:DIL_BAND + s, :] = k_ref[...]
        vg1[DIL_BAND:DIL_BAND + s, :] = v_ref[...]

        def body(c, carry):
            base = pl.multiple_of(c * tb, tb)
            for src, bufs in ((k_ref, k_bufs), (v_ref, v_bufs)):
                stage[...] = src[pl.ds(base, tb), :].astype(F32)
                deinterleave(bufs[d_mid], bufs[d_max], DIL_BAND + c * (tb // d_mid),
                             DIL_BAND + c * (tb // d_max), s)
            return carry

        lax.fori_loop(0, s // tb, body, 0)

    stage[...] = q_ref[...].astype(F32)
    deinterleave(q_perm[d_mid], q_perm[d_max], 0, 0, tb)

    lane = lax.broadcasted_iota(jnp.int32, (1, DIL_TK), 1)
    tiles = []
    for b, d in enumerate(DIL_DILATIONS):
        sub_len = s // d
        per = tb // d
        n_tiles = per // DIL_TQ
        for r in range(d):
            for tj in range(n_tiles):
                tiles.append((b, d, sub_len, per, n_tiles, r, tj))

    def tile_stages(b, d, sub_len, per, n_tiles, r, tj):
        u0 = i * per + tj * DIL_TQ
        q_src = q_ref if d == 1 else q_perm[d]
        q = q_src[r * per + tj * DIL_TQ:r * per + (tj + 1) * DIL_TQ, :]
        k_start = pl.multiple_of(r * sub_len + u0, DIL_TQ)
        k = k_bufs[d][pl.ds(k_start, DIL_TK), :]
        v = v_bufs[d][pl.ds(k_start, DIL_TK), :]
        edge = None
        if tj == 0 or tj == n_tiles - 1:
            key_pos = u0 - DIL_BAND + lane
            edge = jnp.where((key_pos >= 0) & (key_pos < sub_len), 0.0, NEG_INF)
        if d == 1:
            rows = pl.ds(tj * DIL_TQ, DIL_TQ)
        else:
            rows = pl.ds(r + d * tj * DIL_TQ, DIL_TQ, stride=d)
        return _dil_tile(q, k, v, bias_ref[b], edge, o_bufs[d], l_bufs[d], rows)

    for g0 in range(0, len(tiles), DIL_INTERLEAVE):
        _run_interleaved([tile_stages(*tile) for tile in tiles[g0:g0 + DIL_INTERLEAVE]])

    lses = [l_bufs[d][...] for d in DIL_DILATIONS]
    m = functools.reduce(jnp.maximum, lses)
    es = [jnp.exp(l - m) for l in lses]
    num = sum(e * o_bufs[d][...] for e, d in zip(es, DIL_DILATIONS))
    mixed = num / sum(es)
    out_ref[...] = _head_rms(mixed, gain_ref[...]).astype(out_ref.dtype)

    wo_out[...] = wo_ref[...].astype(wo_out.dtype)
    wd_out[...] = wd_ref[...].astype(wd_out.dtype)


def _dil_attention(proj, bias, gain, w_out, w_down, layer):
    s = proj.shape[0]
    tb = DIL_TB
    nq = s // tb
    step_of = lambda h, i: h * nq + i
    wo_in, wo_out, wo_sds = _cast_slab(w_out, layer, DIL_HEADS * nq, step_of)
    wd_in, wd_out, wd_sds = _cast_slab(w_down, layer, DIL_HEADS * nq, step_of)
    guarded = pltpu.VMEM((s + 2 * DIL_BAND, DIL_HEAD_DIM), BF16)
    tile_f32 = pltpu.VMEM((tb, DIL_HEAD_DIM), F32)
    tile_bf16 = pltpu.VMEM((tb, DIL_HEAD_DIM), BF16)
    return pl.pallas_call(
        _dil_kernel,
        grid=(DIL_HEADS, nq),
        in_specs=[
            pl.BlockSpec((tb, DIL_HEAD_DIM), lambda h, i: (i, COL_DQ + h)),
            pl.BlockSpec((s, DIL_HEAD_DIM), lambda h, i: (0, COL_DK + h)),
            pl.BlockSpec((s, DIL_HEAD_DIM), lambda h, i: (0, COL_DV + h)),
            pl.BlockSpec((len(DIL_DILATIONS), None, DIL_TQ, DIL_TK), lambda h, i: (0, h, 0, 0)),
            pl.BlockSpec((None, 1, DIL_HEAD_DIM), lambda h, i: (layer, 0, h)),
            wo_in, wd_in,
        ],
        out_specs=[pl.BlockSpec((tb, DIL_HEAD_DIM), lambda h, i: (i, h)), wo_out, wd_out],
        out_shape=[jax.ShapeDtypeStruct((s, DIL_WIDTH), BF16), wo_sds, wd_sds],
        scratch_shapes=[tile_f32] * 2 + [guarded] * 6 + [tile_bf16] * 2 + [tile_f32] * 6,
        compiler_params=_params(2, BIG_VMEM_LIMIT),
        name="dil_attn",
    )(proj, proj, proj, bias, gain, w_out, w_down)


def _t5_bucket(rel):
    half = REL_BUCKETS // 2
    max_exact = half // 2
    ret = jnp.where(rel > 0, half, 0)
    n = jnp.abs(rel)
    nf = jnp.maximum(n, 1).astype(F32)
    large = max_exact + (jnp.log(nf / max_exact) / math.log(REL_MAX_DISTANCE / max_exact)
                         * (half - max_exact)).astype(jnp.int32)
    large = jnp.minimum(large, half - 1)
    return ret + jnp.where(n < max_exact, n, large)


def _dil_bias_tiles(rel_bias):
    rel = jnp.arange(DIL_TK)[None, :] - DIL_BAND - jnp.arange(DIL_TQ)[:, None]
    band = jnp.abs(rel) <= DIL_BAND
    tiles = []
    for d in DIL_DILATIONS:
        onehot = (_t5_bucket(rel * d)[..., None] == jnp.arange(REL_BUCKETS)).astype(F32)
        bias = jnp.einsum("qkb,bh->hqk", onehot, rel_bias.astype(F32), precision=lax.Precision.HIGHEST)
        tiles.append(jnp.where(band[None], bias, NEG_INF))
    return jnp.stack(tiles, axis=0)


def _mem_kv_kernel(mem_ref, g_ref, w_ref, o_ref):
    h = _rms(mem_ref[...], g_ref[...]).astype(BF16)
    o_ref[...] = _dot(h, w_ref[...].astype(BF16)).astype(o_ref.dtype)


def _mem_kv(mem, gain, w):
    m, d = mem.shape
    n = w.shape[-1]
    return pl.pallas_call(
        _mem_kv_kernel,
        grid=(DEPTH,),
        in_specs=[
            pl.BlockSpec((m, d), lambda l: (0, 0)),
            pl.BlockSpec((None, 1, d), lambda l: (l, 0, 0)),
            pl.BlockSpec((None, d, n), lambda l: (l, 0, 0)),
        ],
        out_specs=pl.BlockSpec((None, m, n), lambda l: (l, 0, 0)),
        out_shape=jax.ShapeDtypeStruct((DEPTH, m, n), BF16),
        compiler_params=_params(1),
        name="mem_kv",
    )(mem, gain, w)


def _head_rms(o, gain):
    return o * lax.rsqrt(jnp.mean(o * o, axis=-1, keepdims=True) + EPS) * gain


def _mix_kernel(of_ref, ob_ref, gr_ref, dil_ref, mq_ref, km_ref, vm_ref, ggla_ref, gmem_ref, w_ref, x_ref,
                *rest, prep_next):
    if prep_next:
        w_in_ref, out_ref, w_in_out, gla_ref, mem_ref = rest
        _w_in_prep_kernel(w_in_ref, w_in_out)
    else:
        out_ref, gla_ref, mem_ref = rest

    dil0 = GLA_V_WIDTH
    mem0 = GLA_V_WIDTH + DIL_WIDTH

    def mem_head(h):
        sl = slice(h * MEM_HEAD_DIM, (h + 1) * MEM_HEAD_DIM)
        raw = _dot_nt(mq_ref[:, sl], km_ref[:, sl])
        yield
        sc = raw * (MEM_HEAD_DIM ** -0.5)
        m = jnp.max(sc, axis=-1, keepdims=True)
        p = jnp.exp(sc - m)
        den = jnp.sum(p, axis=-1, keepdims=True)
        pv = _dot(p.astype(BF16), vm_ref[:, sl])
        yield
        mem_ref[:, sl] = _head_rms(pv / den, gmem_ref[:, sl]).astype(mem_ref.dtype)

    def gla_heads():
        for h in range(GLA_HEADS):
            sl = slice(h * GLA_DV, (h + 1) * GLA_DV)
            o = of_ref[:, sl].astype(F32) + ob_ref[:, sl].astype(F32)
            r = gr_ref[:, sl].astype(F32)
            gate = r * (1.0 / (1.0 + jnp.exp(-r)))
            gla_ref[:, sl] = (_head_rms(o, ggla_ref[:, sl]) * gate).astype(gla_ref.dtype)
        yield
        out_ref[...] += _dot(gla_ref[...], w_ref[0:dil0, :])

    def dil_group():
        out_ref[...] = x_ref[...] + _dot(dil_ref[...], w_ref[dil0:mem0, :])
        yield

    _run_interleaved([mem_head(h) for h in range(MEM_HEADS)] + [dil_group(), gla_heads()])
    out_ref[...] += _dot(mem_ref[...], w_ref[mem0:, :])


def _mix_out(x, proj, o_f, o_b, dil, kv, g_gla, g_mem, w_out, w_in_t, layer, t=512):
    s, d = x.shape
    n_steps = s // t
    prep_next = layer + 1 < w_in_t.shape[0]
    rows = lambda w, c=0: pl.BlockSpec((t, w), lambda i: (i, c))
    per_layer = lambda shape, c=0: pl.BlockSpec((None,) + shape, lambda i: (layer, 0, c),
                                                pipeline_mode=pl.Buffered(1))
    in_specs = [
        rows(GLA_V_WIDTH), rows(GLA_V_WIDTH),
        rows(GLA_V_WIDTH, COL_GR * LANES // GLA_V_WIDTH),
        rows(DIL_WIDTH),
        rows(MEM_WIDTH, COL_MQ * LANES // MEM_WIDTH),
        per_layer((MEM_LEN, MEM_WIDTH), 0), per_layer((MEM_LEN, MEM_WIDTH), 1),
        per_layer((1, GLA_V_WIDTH)), per_layer((1, MEM_WIDTH)),
        pl.BlockSpec((MIX_WIDTH, d), lambda i: (0, 0), pipeline_mode=pl.Buffered(1)),
        rows(d),
    ]
    operands = [o_f, o_b, proj, dil, proj, kv, kv, g_gla, g_mem, w_out, x]
    out_specs = [rows(d)]
    out_shape = [jax.ShapeDtypeStruct((s, d), F32)]
    if prep_next:
        n_in = w_in_t.shape[1]
        in_specs.append(pl.BlockSpec((None, n_in, d // n_steps), lambda i: (layer + 1, 0, i)))
        operands.append(w_in_t)
        out_specs.append(pl.BlockSpec((PROJ_WIDTH, d // n_steps), lambda i: (0, i)))
        out_shape.append(jax.ShapeDtypeStruct((PROJ_WIDTH, d), BF16))
    outs = pl.pallas_call(
        functools.partial(_mix_kernel, prep_next=prep_next),
        grid=(n_steps,),
        in_specs=in_specs,
        out_specs=out_specs,
        out_shape=out_shape,
        scratch_shapes=[pltpu.VMEM((t, GLA_V_WIDTH), BF16), pltpu.VMEM((t, MEM_WIDTH), BF16)],
        compiler_params=_params(1),
        name="mix_out_proj",
    )(*operands)
    return (outs[0], outs[1]) if prep_next else (outs[0], None)


def _mlp_kernel(x_ref, g_ref, wu_ref, wd_ref, gf_ref, o_ref, h_ref, *, final_norm, n_sub):
    f = pl.program_id(1)
    sub = x_ref.shape[0] // n_sub

    def hidden(h):
        return jnp.square(jnp.maximum(_dot(h, wu_ref[...]), 0.0)).astype(BF16)

    @pl.when(f == 0)
    def _():
        for c in range(n_sub):
            rows = slice(c * sub, (c + 1) * sub)
            x = x_ref[rows, :]
            h = _rms(x, g_ref[...]).astype(h_ref.dtype)
            h_ref[rows, :] = h
            o_ref[rows, :] = x + _dot(hidden(h), wd_ref[...])

    @pl.when(f != 0)
    def _():
        for c in range(n_sub):
            rows = slice(c * sub, (c + 1) * sub)
            o_ref[rows, :] += _dot(hidden(h_ref[rows, :]), wd_ref[...])

    if final_norm:
        @pl.when(f == pl.num_programs(1) - 1)
        def _():
            for c in range(n_sub):
                rows = slice(c * sub, (c + 1) * sub)
                o_ref[rows, :] = _rms(o_ref[rows, :], gf_ref[...])


def _mlp(x, gain, w_up, w_down, gain_final, layer, final_norm, tm=1024, tf=1024, n_sub=4):
    s, d = x.shape
    ff = w_up.shape[-1]
    return pl.pallas_call(
        functools.partial(_mlp_kernel, final_norm=final_norm, n_sub=n_sub),
        grid=(s // tm, ff // tf),
        in_specs=[
            pl.BlockSpec((tm, d), lambda i, f: (i, 0)),
            pl.BlockSpec((None, 1, d), lambda i, f: (layer, 0, 0)),
            pl.BlockSpec((d, tf), lambda i, f: (0, f)),
            pl.BlockSpec((tf, d), lambda i, f: (f, 0)),
            pl.BlockSpec((1, d), lambda i, f: (0, 0)),
        ],
        out_specs=pl.BlockSpec((tm, d), lambda i, f: (i, 0)),
        out_shape=jax.ShapeDtypeStruct((s, d), F32),
        scratch_shapes=[pltpu.VMEM((tm, d), BF16)],
        compiler_params=_params(2, BIG_VMEM_LIMIT),
        name="mlp",
    )(x, gain, w_up, w_down, gain_final)


def _w_in_prep_kernel(w_ref, o_ref):
    lr0 = 2 * GLA_QK_WIDTH + 2 * GLA_V_WIDTH
    lr_w = 2 * GLA_GATE_RANK
    tail = w_ref.shape[0] - lr0 - lr_w
    o_ref[0:lr0, :] = w_ref[0:lr0, :].astype(o_ref.dtype)
    o_ref[lr0:lr0 + tail, :] = w_ref[lr0 + lr_w:, :].astype(o_ref.dtype)
    o_ref[lr0 + tail:lr0 + tail + lr_w, :] = w_ref[lr0:lr0 + lr_w, :].astype(o_ref.dtype)
    o_ref[lr0 + tail + lr_w:, :] = jnp.zeros((o_ref.shape[0] - lr0 - tail - lr_w, o_ref.shape[1]), o_ref.dtype)


def _reorder_w_in(w_in_t, layer, cols=256):
    _, n, d = w_in_t.shape
    assert COL_LR * LANES == n - 2 * GLA_GATE_RANK
    return pl.pallas_call(
        _w_in_prep_kernel,
        grid=(d // cols,),
        in_specs=[pl.BlockSpec((None, n, cols), lambda i: (layer, 0, i))],
        out_specs=pl.BlockSpec((PROJ_WIDTH, cols), lambda i: (0, i)),
        out_shape=jax.ShapeDtypeStruct((PROJ_WIDTH, d), BF16),
        compiler_params=_params(1),
        name="w_in_prep",
    )(w_in_t)


def _pad_gate_up(up, row0):
    out = jnp.zeros((up.shape[0], LANES, up.shape[-1]), BF16)
    return out.at[:, row0:row0 + GLA_GATE_RANK, :].set(up.astype(BF16))


def kernel(x, mem, norm_mix, w_in, gla_gate_up_fwd, gla_gate_bias_fwd, gla_gate_up_bwd, gla_gate_bias_bwd, gla_norm, rel_bias, dil_norm, mem_norm, w_mem_kv, mem_out_norm, w_out, norm_mlp, w_up, w_down, norm_final):
    batch, seq, d = x.shape
    assert batch == 1 and seq == SEQ and d == D_MODEL
    xs = x.reshape(seq, d)
    row = lambda g: g.reshape(g.shape[0], 1, g.shape[-1])

    w_in_t = jnp.swapaxes(w_in, 1, 2)
    w_in_l = _reorder_w_in(w_in_t, 0)
    up_f = _pad_gate_up(gla_gate_up_fwd, 0)
    up_b = _pad_gate_up(gla_gate_up_bwd, GLA_GATE_RANK)
    dil_bias = _dil_bias_tiles(rel_bias)
    kv = _mem_kv(mem.reshape(MEM_LEN, d), row(mem_norm), w_mem_kv)

    for layer in range(DEPTH):
        proj = _in_proj(xs, row(norm_mix), w_in_l, layer)
        o_f, o_b, w_up_b = _gla(proj, up_f, up_b, row(gla_gate_bias_fwd), row(gla_gate_bias_bwd), w_up, layer)
        dil, w_out_b, w_down_b = _dil_attention(proj, dil_bias, row(dil_norm), w_out, w_down, layer)
        xs, w_in_l = _mix_out(xs, proj, o_f, o_b, dil, kv, row(gla_norm), row(mem_out_norm), w_out_b,
                              w_in_t, layer)
        xs = _mlp(xs, row(norm_mlp), w_up_b, w_down_b, norm_final.reshape(1, d), layer,
                  final_norm=(layer == DEPTH - 1))
    return xs.reshape(batch, seq, d)
```

```python
import functools
import math

import jax
import jax.numpy as jnp
from jax import lax
from jax.experimental import pallas as pl
from jax.experimental.pallas import tpu as pltpu

F32 = jnp.float32
BF16 = jnp.bfloat16

D_MODEL = 2048
SEQ = 8192
DEPTH = 4
MEM_LEN = 256
GLA_HEADS = 4
GLA_DK = 128
GLA_DV = 256
GLA_GATE_RANK = 16
GLA_GATE_NORMALIZER = 16.0
GLA_CHUNK = 64
DIL_HEADS = 4
DIL_HEAD_DIM = 128
DIL_CONFIGS = ((128, 1), (512, 4), (2048, 16))
MEM_HEADS = 4
MEM_HEAD_DIM = 128
REL_BUCKETS = 32
REL_MAX_DISTANCE = 1024
D_FF = 4 * D_MODEL
EPS = 1e-6
NEG_INF = -1e30

GLA_QK_WIDTH = GLA_HEADS * GLA_DK
GLA_V_WIDTH = GLA_HEADS * GLA_DV
DIL_WIDTH = DIL_HEADS * DIL_HEAD_DIM
MEM_WIDTH = MEM_HEADS * MEM_HEAD_DIM
MIX_WIDTH = GLA_V_WIDTH + DIL_WIDTH + MEM_WIDTH

LANES = 128

COL_GQ = 0
COL_GK = COL_GQ + GLA_QK_WIDTH // LANES
COL_GV = COL_GK + GLA_QK_WIDTH // LANES
COL_GR = COL_GV + GLA_V_WIDTH // LANES
COL_DQ = COL_GR + GLA_V_WIDTH // LANES
COL_DK = COL_DQ + DIL_WIDTH // LANES
COL_DV = COL_DK + DIL_WIDTH // LANES
COL_MQ = COL_DV + DIL_WIDTH // LANES
COL_LR = COL_MQ + MEM_WIDTH // LANES
PROJ_COLS = COL_LR + 2
PROJ_WIDTH = PROJ_COLS * LANES

GLA_BLOCK = 256

DIL_BAND = 64
DIL_TQ = 128
DIL_TK = DIL_TQ + 2 * DIL_BAND
DIL_DILATIONS = tuple(d for _, d in DIL_CONFIGS)
DIL_TB = DIL_TQ * max(DIL_DILATIONS)
DIL_INTERLEAVE = 8

VMEM_LIMIT = 48 * 1024 * 1024
BIG_VMEM_LIMIT = 60 * 1024 * 1024


def _params(n_axes, vmem=VMEM_LIMIT):
    return pltpu.CompilerParams(dimension_semantics=("arbitrary",) * n_axes, vmem_limit_bytes=vmem)


def _dot(a, b):
    return jnp.dot(a, b, preferred_element_type=F32)


def _dot_nt(a, b):
    return lax.dot_general(a, b, (((1,), (1,)), ((), ())), preferred_element_type=F32)


def _dot_tn(a, b):
    return lax.dot_general(a, b, (((0,), (0,)), ((), ())), preferred_element_type=F32)


def _rms(x, gain):
    return x * lax.rsqrt(jnp.mean(x * x, axis=-1, keepdims=True) + EPS) * gain


def _run_interleaved(stage_generators):
    live = list(stage_generators)
    while live:
        live = [gen for gen in live if next(gen, StopIteration) is not StopIteration]


def _in_proj_kernel(x_ref, g_ref, w_ref, o_ref, h_ref, *, n_sub):
    j = pl.program_id(1)
    sub = x_ref.shape[0] // n_sub

    @pl.when(j == 0)
    def _():
        for c in range(n_sub):
            rows = slice(c * sub, (c + 1) * sub)
            h = _rms(x_ref[rows, :], g_ref[...]).astype(h_ref.dtype)
            h_ref[rows, :] = h
            o_ref[rows, :] = _dot_nt(h, w_ref[...]).astype(o_ref.dtype)

    @pl.when(j != 0)
    def _():
        for c in range(n_sub):
            rows = slice(c * sub, (c + 1) * sub)
            o_ref[rows, :] = _dot_nt(h_ref[rows, :], w_ref[...]).astype(o_ref.dtype)


def _in_proj(x, gain, w, layer, tm=1024, tn=1792, n_sub=4):
    s, d = x.shape
    n = w.shape[0]
    return pl.pallas_call(
        functools.partial(_in_proj_kernel, n_sub=n_sub),
        grid=(s // tm, n // tn),
        in_specs=[
            pl.BlockSpec((tm, d), lambda i, j: (i, 0)),
            pl.BlockSpec((None, 1, d), lambda i, j: (layer, 0, 0)),
            pl.BlockSpec((tn, d), lambda i, j: (j, 0)),
        ],
        out_specs=pl.BlockSpec((tm, tn), lambda i, j: (i, j)),
        out_shape=jax.ShapeDtypeStruct((s, n), BF16),
        scratch_shapes=[pltpu.VMEM((tm, d), BF16)],
        compiler_params=_params(2),
        name="in_proj",
    )(x, gain, w)


def _log_sigmoid(x):
    return jnp.minimum(x, 0.0) - jnp.log(1.0 + jnp.exp(-jnp.abs(x)))


GLA_LEVELS = (GLA_BLOCK // GLA_CHUNK).bit_length() - 1


def _gla_tables(tri_ref, lvl_ref, reverse):
    t = GLA_BLOCK
    row = lax.broadcasted_iota(jnp.int32, (t, t), 0)
    col = lax.broadcasted_iota(jnp.int32, (t, t), 1)
    valid = (col >= row) if reverse else (col <= row)
    tri_ref[...] = jnp.where(valid, 1.0, 0.0).astype(tri_ref.dtype)
    shift = GLA_CHUNK.bit_length() - 1
    lvl_ref[...] = jnp.where(jnp.right_shift(row, shift) == jnp.right_shift(col, shift),
                             jnp.where(valid, 0, -1), -1)


def _gla_boundary(cum, group, reverse):
    t = cum.shape[0]
    pieces = []
    for a in range(0, t, group):
        idx = a + group if reverse else a - 1
        if idx < 0 or idx >= t:
            ref_row = jnp.zeros((1, cum.shape[1]), cum.dtype)
        else:
            ref_row = cum[idx:idx + 1]
        pieces.append(jnp.broadcast_to(ref_row, (group, cum.shape[1])))
    return jnp.concatenate(pieces, axis=0)


def _place_rows(part, row0, total_rows):
    pieces = []
    if row0 > 0:
        pieces.append(jnp.zeros((row0, part.shape[1]), part.dtype))
    pieces.append(part)
    rest = total_rows - row0 - part.shape[0]
    if rest > 0:
        pieces.append(jnp.zeros((rest, part.shape[1]), part.dtype))
    return jnp.concatenate(pieces, axis=0)


def _gla_block(q_ref, k_ref, v_ref, lr_ref, up_ref, bias_ref, o_ref, s_ref, tri_ref, lvl_ref, r0, reverse):
    t = GLA_BLOCK
    rows = slice(r0, r0 + t)
    logits = _dot(lr_ref[rows, :], up_ref[...]) + bias_ref[...]
    yield
    g = _log_sigmoid(logits) * (1.0 / GLA_GATE_NORMALIZER)
    g_hi = g.astype(BF16)
    g_lo = (g - g_hi.astype(F32)).astype(BF16)
    cum2 = _dot(tri_ref[...], jnp.concatenate([g_hi, g_lo], axis=1))
    yield
    cum = cum2[:, :GLA_DK] + cum2[:, GLA_DK:]

    q = q_ref[rows, :].astype(F32) * (GLA_DK ** -0.5)
    k = k_ref[rows, :].astype(F32)
    v = v_ref[rows, :]
    lvl = lvl_ref[...]

    ref0 = _gla_boundary(cum, GLA_CHUNK, reverse)
    q_dec = (q * jnp.exp(cum - ref0)).astype(BF16)
    k_inv = (k * jnp.exp(ref0 - cum)).astype(BF16)
    q_segs, k_segs = [], []
    for level in range(1, GLA_LEVELS + 1):
        group = GLA_CHUNK << level
        for a0 in range(0, t, group):
            mid = a0 + group // 2
            att = slice(a0, mid) if reverse else slice(mid, a0 + group)
            src = slice(mid, a0 + group) if reverse else slice(a0, mid)
            ref_row = cum[mid:mid + 1] if reverse else cum[mid - 1:mid]
            q_part = (q[att] * jnp.exp(cum[att] - ref_row)).astype(BF16)
            k_part = (k[src] * jnp.exp(ref_row - cum[src])).astype(BF16)
            q_segs.append(_place_rows(q_part, att.start, t))
            k_segs.append(_place_rows(k_part, src.start, t))
    cross = _dot_nt(jnp.concatenate(q_segs, axis=1), jnp.concatenate(k_segs, axis=1))
    diag = _dot_nt(q_dec, k_inv)
    total = cum[0:1] if reverse else cum[t - 1:t]
    q_in = (q * jnp.exp(cum)).astype(BF16)
    k_out = (k * jnp.exp(total - cum)).astype(BF16)
    carry_in = _dot_tn(v, k_out)
    yield
    a = jnp.where(lvl == 0, diag, cross)
    o_local = _dot(a.astype(BF16), v)
    yield
    state = s_ref[...]
    o = o_local + _dot_nt(q_in, state.astype(BF16))
    o_ref[rows, :] = o.astype(o_ref.dtype)
    s_ref[...] = state * jnp.exp(total) + carry_in


def _cast_slab(weights, layer, n_steps, step_of):
    rows, cols = weights.shape[1:]
    slab = rows // n_steps
    return (pl.BlockSpec((None, slab, cols), lambda *g: (layer, step_of(*g), 0)),
            pl.BlockSpec((slab, cols), lambda *g: (step_of(*g), 0)),
            jax.ShapeDtypeStruct((rows, cols), BF16))


def _gla_kernel(qf, kf, vf, lrf, qb, kb, vb, lrb, upf, upb, bsf, bsb, wu_ref, of, ob, wu_out,
                sf, sb, tri_f, tri_b, lvl_f, lvl_b, *, interleave):
    @pl.when(pl.program_id(1) == 0)
    def _():
        sf[...] = jnp.zeros_like(sf)
        sb[...] = jnp.zeros_like(sb)
        _gla_tables(tri_f, lvl_f, reverse=False)
        _gla_tables(tri_b, lvl_b, reverse=True)

    starts = list(range(0, qf.shape[0], GLA_BLOCK))
    n_groups = len(starts) // interleave

    def cast_piece(piece):
        n = wu_ref.shape[0] // n_groups
        wu_out[piece * n:(piece + 1) * n, :] = wu_ref[piece * n:(piece + 1) * n, :].astype(wu_out.dtype)

    for gi in range(n_groups):
        blocks = []
        for r_fwd, r_bwd in list(zip(starts, reversed(starts)))[gi * interleave:(gi + 1) * interleave]:
            blocks.append(_gla_block(qf, kf, vf, lrf, upf, bsf, of, sf, tri_f, lvl_f, r_fwd, reverse=False))
            blocks.append(_gla_block(qb, kb, vb, lrb, upb, bsb, ob, sb, tri_b, lvl_b, r_bwd, reverse=True))
        _run_interleaved(blocks)
        cast_piece(gi)


def _gla(proj, up_f, up_b, bias_f, bias_b, w_up, layer, t=1024, interleave=4):
    s = proj.shape[0]
    nb = s // t
    vcol = COL_GV * LANES // GLA_DV
    wu_in, wu_out, wu_sds = _cast_slab(w_up, layer, GLA_HEADS * nb, lambda h, i: h * nb + i)

    def row_specs(rmap):
        return [
            pl.BlockSpec((t, GLA_DK), lambda h, i: (rmap(i), COL_GQ + h)),
            pl.BlockSpec((t, GLA_DK), lambda h, i: (rmap(i), COL_GK + h)),
            pl.BlockSpec((t, GLA_DV), lambda h, i: (rmap(i), vcol + h)),
            pl.BlockSpec((t, LANES), lambda h, i: (rmap(i), COL_LR)),
        ]

    fwd = lambda i: i
    bwd = lambda i: nb - 1 - i
    up_spec = pl.BlockSpec((None, LANES, GLA_DK), lambda h, i: (layer, 0, h))
    bias_spec = pl.BlockSpec((None, 1, GLA_DK), lambda h, i: (layer, 0, h))
    out_sds = jax.ShapeDtypeStruct((s, GLA_V_WIDTH), BF16)
    state = pltpu.VMEM((GLA_DV, GLA_DK), F32)
    tri = pltpu.VMEM((GLA_BLOCK, GLA_BLOCK), BF16)
    lvl = pltpu.VMEM((GLA_BLOCK, GLA_BLOCK), jnp.int32)
    return pl.pallas_call(
        functools.partial(_gla_kernel, interleave=interleave),
        grid=(GLA_HEADS, nb),
        in_specs=row_specs(fwd) + row_specs(bwd) + [up_spec, up_spec, bias_spec, bias_spec, wu_in],
        out_specs=[
            pl.BlockSpec((t, GLA_DV), lambda h, i: (fwd(i), h)),
            pl.BlockSpec((t, GLA_DV), lambda h, i: (bwd(i), h)),
            wu_out,
        ],
        out_shape=[out_sds, out_sds, wu_sds],
        scratch_shapes=[state, state, tri, tri, lvl, lvl],
        compiler_params=_params(2),
        name="gla_scan",
    )(proj, proj, proj, proj, proj, proj, proj, proj, up_f, up_b, bias_f, bias_b, w_up)


def _dil_tile(q, k, v, bias, edge, o_dst, l_dst, rows):
    raw = _dot_nt(q, k)
    yield
    sc = raw * (DIL_HEAD_DIM ** -0.5) + bias
    if edge is not None:
        sc = sc + edge
    m = jnp.max(sc, axis=-1, keepdims=True)
    p = jnp.exp(sc - m).astype(BF16)
    pv = _dot(p, jnp.concatenate([v, jnp.ones_like(v)], axis=1))
    yield
    den = pv[:, DIL_HEAD_DIM:]
    o_dst[rows, :] = pv[:, :DIL_HEAD_DIM] / den
    l_dst[rows, :] = m + jnp.log(den)


def _dil_kernel(q_ref, k_ref, v_ref, bias_ref, gain_ref, wo_ref, wd_ref, out_ref, wo_out, wd_out,
                stage, stage_mid, kg1, vg1, kg4, vg4, kg16, vg16, qp4, qp16, o1, o4, o16, l1, l4, l16):
    i = pl.program_id(1)
    s = k_ref.shape[0]
    tb = q_ref.shape[0]
    k_bufs = dict(zip(DIL_DILATIONS, (kg1, kg4, kg16)))
    v_bufs = dict(zip(DIL_DILATIONS, (vg1, vg4, vg16)))
    q_perm = dict(zip(DIL_DILATIONS, (None, qp4, qp16)))
    o_bufs = dict(zip(DIL_DILATIONS, (o1, o4, o16)))
    l_bufs = dict(zip(DIL_DILATIONS, (l1, l4, l16)))
    _, d_mid, d_max = DIL_DILATIONS
    assert d_max == d_mid * d_mid

    def deinterleave(dst_mid, dst_max, base_mid, base_max, residue_rows):
        per_mid, per_max = tb // d_mid, tb // d_max

        def at(base, offset):
            start = base + offset
            return start if isinstance(start, int) else pl.multiple_of(start, DIL_BAND)

        for r in range(d_mid):
            part = stage[pl.ds(r, per_mid, stride=d_mid), :]
            stage_mid[r * per_mid:(r + 1) * per_mid, :] = part
            dst_mid[pl.ds(at(base_mid, r * (residue_rows // d_mid)), per_mid), :] = part.astype(dst_mid.dtype)
        for r in range(d_max):
            r_mid, m = r % d_mid, r // d_mid
            part = stage_mid[pl.ds(r_mid * per_mid + m, per_max, stride=d_mid), :]
            dst_max[pl.ds(at(base_max, r * (residue_rows // d_max)), per_max), :] = part.astype(dst_max.dtype)

    @pl.when(i == 0)
    def _():
        zeros = jnp.zeros((DIL_BAND, DIL_HEAD_DIM), BF16)
        for buf in (kg1, vg1, kg4, vg4, kg16, vg16):
            buf[0:DIL_BAND, :] = zeros
            buf[DIL_BAND + s:, :] = zeros
        kg1[DIL_BAND:DIL_BAND + s, :] = k_ref[...]
        vg1[DIL_BAND:DIL_BAND + s, :] = v_ref[...]

        def body(c, carry):
            base = pl.multiple_of(c * tb, tb)
            for src, bufs in ((k_ref, k_bufs), (v_ref, v_bufs)):
                stage[...] = src[pl.ds(base, tb), :].astype(F32)
                deinterleave(bufs[d_mid], bufs[d_max], DIL_BAND + c * (tb // d_mid),
                             DIL_BAND + c * (tb // d_max), s)
            return carry

        lax.fori_loop(0, s // tb, body, 0)

    stage[...] = q_ref[...].astype(F32)
    deinterleave(q_perm[d_mid], q_perm[d_max], 0, 0, tb)

    lane = lax.broadcasted_iota(jnp.int32, (1, DIL_TK), 1)
    tiles = []
    for b, d in enumerate(DIL_DILATIONS):
        sub_len = s // d
        per = tb // d
        n_tiles = per // DIL_TQ
        for r in range(d):
            for tj in range(n_tiles):
                tiles.append((b, d, sub_len, per, n_tiles, r, tj))

    def tile_stages(b, d, sub_len, per, n_tiles, r, tj):
        u0 = i * per + tj * DIL_TQ
        q_src = q_ref if d == 1 else q_perm[d]
        q = q_src[r * per + tj * DIL_TQ:r * per + (tj + 1) * DIL_TQ, :]
        k_start = pl.multiple_of(r * sub_len + u0, DIL_TQ)
        k = k_bufs[d][pl.ds(k_start, DIL_TK), :]
        v = v_bufs[d][pl.ds(k_start, DIL_TK), :]
        edge = None
        if tj == 0 or tj == n_tiles - 1:
            key_pos = u0 - DIL_BAND + lane
            edge = jnp.where((key_pos >= 0) & (key_pos < sub_len), 0.0, NEG_INF)
        if d == 1:
            rows = pl.ds(tj * DIL_TQ, DIL_TQ)
        else:
            rows = pl.ds(r + d * tj * DIL_TQ, DIL_TQ, stride=d)
        return _dil_tile(q, k, v, bias_ref[b], edge, o_bufs[d], l_bufs[d], rows)

    for g0 in range(0, len(tiles), DIL_INTERLEAVE):
        _run_interleaved([tile_stages(*tile) for tile in tiles[g0:g0 + DIL_INTERLEAVE]])

    lses = [l_bufs[d][...] for d in DIL_DILATIONS]
    m = functools.reduce(jnp.maximum, lses)
    es = [jnp.exp(l - m) for l in lses]
    num = sum(e * o_bufs[d][...] for e, d in zip(es, DIL_DILATIONS))
    mixed = num / sum(es)
    out_ref[...] = _head_rms(mixed, gain_ref[...]).astype(out_ref.dtype)

    wo_out[...] = wo_ref[...].astype(wo_out.dtype)
    wd_out[...] = wd_ref[...].astype(wd_out.dtype)


def _dil_attention(proj, bias, gain, w_out, w_down, layer):
    s = proj.shape[0]
    tb = DIL_TB
    nq = s // tb
    step_of = lambda h, i: h * nq + i
    wo_in, wo_out, wo_sds = _cast_slab(w_out, layer, DIL_HEADS * nq, step_of)
    wd_in, wd_out, wd_sds = _cast_slab(w_down, layer, DIL_HEADS * nq, step_of)
    guarded = pltpu.VMEM((s + 2 * DIL_BAND, DIL_HEAD_DIM), BF16)
    tile_f32 = pltpu.VMEM((tb, DIL_HEAD_DIM), F32)
    tile_bf16 = pltpu.VMEM((tb, DIL_HEAD_DIM), BF16)
    return pl.pallas_call(
        _dil_kernel,
        grid=(DIL_HEADS, nq),
        in_specs=[
            pl.BlockSpec((tb, DIL_HEAD_DIM), lambda h, i: (i, COL_DQ + h)),
            pl.BlockSpec((s, DIL_HEAD_DIM), lambda h, i: (0, COL_DK + h)),
            pl.BlockSpec((s, DIL_HEAD_DIM), lambda h, i: (0, COL_DV + h)),
            pl.BlockSpec((len(DIL_DILATIONS), None, DIL_TQ, DIL_TK), lambda h, i: (0, h, 0, 0)),
            pl.BlockSpec((None, 1, DIL_HEAD_DIM), lambda h, i: (layer, 0, h)),
            wo_in, wd_in,
        ],
        out_specs=[pl.BlockSpec((tb, DIL_HEAD_DIM), lambda h, i: (i, h)), wo_out, wd_out],
        out_shape=[jax.ShapeDtypeStruct((s, DIL_WIDTH), BF16), wo_sds, wd_sds],
        scratch_shapes=[tile_f32] * 2 + [guarded] * 6 + [tile_bf16] * 2 + [tile_f32] * 6,
        compiler_params=_params(2, BIG_VMEM_LIMIT),
        name="dil_attn",
    )(proj, proj, proj, bias, gain, w_out, w_down)


def _gla_dil_kernel(qf, kf, vf, lrf, qb, kb, vb, lrb, upf, upb, bsf, bsb, wu_ref,
                    q_ref, k_ref, v_ref, bias_ref, gain_ref, wo_ref, wd_ref,
                    of, ob, wu_out, out_ref, wo_out, wd_out,
                    sf, sb, tri_f, tri_b, lvl_f, lvl_b,
                    stage, stage_mid, kg1, vg1, kg4, vg4, kg16, vg16, qp4, qp16, o1, o4, o16, l1, l4, l16):
    step = pl.program_id(1)
    i = step // 2
    s = k_ref.shape[0]
    tb = q_ref.shape[0]
    k_bufs = dict(zip(DIL_DILATIONS, (kg1, kg4, kg16)))
    v_bufs = dict(zip(DIL_DILATIONS, (vg1, vg4, vg16)))
    q_perm = dict(zip(DIL_DILATIONS, (None, qp4, qp16)))
    o_bufs = dict(zip(DIL_DILATIONS, (o1, o4, o16)))
    l_bufs = dict(zip(DIL_DILATIONS, (l1, l4, l16)))
    _, d_mid, d_max = DIL_DILATIONS
    assert d_max == d_mid * d_mid

    def deinterleave(dst_mid, dst_max, base_mid, base_max, residue_rows):
        per_mid, per_max = tb // d_mid, tb // d_max

        def at(base, offset):
            start = base + offset
            return start if isinstance(start, int) else pl.multiple_of(start, DIL_BAND)

        for r in range(d_mid):
            part = stage[pl.ds(r, per_mid, stride=d_mid), :]
            stage_mid[r * per_mid:(r + 1) * per_mid, :] = part
            dst_mid[pl.ds(at(base_mid, r * (residue_rows // d_mid)), per_mid), :] = part.astype(dst_mid.dtype)
        for r in range(d_max):
            r_mid, m = r % d_mid, r // d_mid
            part = stage_mid[pl.ds(r_mid * per_mid + m, per_max, stride=d_mid), :]
            dst_max[pl.ds(at(base_max, r * (residue_rows // d_max)), per_max), :] = part.astype(dst_max.dtype)

    @pl.when(step == 0)
    def _():
        sf[...] = jnp.zeros_like(sf)
        sb[...] = jnp.zeros_like(sb)
        _gla_tables(tri_f, lvl_f, reverse=False)
        _gla_tables(tri_b, lvl_b, reverse=True)

        zeros = jnp.zeros((DIL_BAND, DIL_HEAD_DIM), BF16)
        for buf in (kg1, vg1, kg4, vg4, kg16, vg16):
            buf[0:DIL_BAND, :] = zeros
            buf[DIL_BAND + s:, :] = zeros
        kg1[DIL_BAND:DIL_BAND + s, :] = k_ref[...]
        vg1[DIL_BAND:DIL_BAND + s, :] = v_ref[...]

        def body(c, carry):
            base = pl.multiple_of(c * tb, tb)
            for src, bufs in ((k_ref, k_bufs), (v_ref, v_bufs)):
                stage[...] = src[pl.ds(base, tb), :].astype(F32)
                deinterleave(bufs[d_mid], bufs[d_max], DIL_BAND + c * (tb // d_mid),
                             DIL_BAND + c * (tb // d_max), s)
            return carry

        lax.fori_loop(0, s // tb, body, 0)

    lane = lax.broadcasted_iota(jnp.int32, (1, DIL_TK), 1)
    tiles = []
    for b, d in enumerate(DIL_DILATIONS):
        sub_len = s // d
        per = tb // d
        n_tiles = per // DIL_TQ
        for r in range(d):
            for tj in range(n_tiles):
                tiles.append((b, d, sub_len, per, n_tiles, r, tj))

    def tile_stages(b, d, sub_len, per, n_tiles, r, tj):
        u0 = i * per + tj * DIL_TQ
        q_src = q_ref if d == 1 else q_perm[d]
        q = q_src[r * per + tj * DIL_TQ:r * per + (tj + 1) * DIL_TQ, :]
        k_start = pl.multiple_of(r * sub_len + u0, DIL_TQ)
        k = k_bufs[d][pl.ds(k_start, DIL_TK), :]
        v = v_bufs[d][pl.ds(k_start, DIL_TK), :]
        edge = None
        if tj == 0 or tj == n_tiles - 1:
            key_pos = u0 - DIL_BAND + lane
            edge = jnp.where((key_pos >= 0) & (key_pos < sub_len), 0.0, NEG_INF)
        if d == 1:
            rows = pl.ds(tj * DIL_TQ, DIL_TQ)
        else:
            rows = pl.ds(r + d * tj * DIL_TQ, DIL_TQ, stride=d)
        return _dil_tile(q, k, v, bias_ref[b], edge, o_bufs[d], l_bufs[d], rows)

    def scan_and_tiles(tile_subset):
        starts = list(range(0, qf.shape[0], GLA_BLOCK))
        blocks = []
        for r_fwd, r_bwd in zip(starts, reversed(starts)):
            blocks.append(_gla_block(qf, kf, vf, lrf, upf, bsf, of, sf, tri_f, lvl_f, r_fwd, reverse=False))
            blocks.append(_gla_block(qb, kb, vb, lrb, upb, bsb, ob, sb, tri_b, lvl_b, r_bwd, reverse=True))
        groups = [tile_subset[g0:g0 + DIL_INTERLEAVE] for g0 in range(0, len(tile_subset), DIL_INTERLEAVE)]
        casts = [(wu_ref, wu_out), (wo_ref, wo_out), (wd_ref, wd_out)]
        while blocks:
            blocks = [blk for blk in blocks if next(blk, StopIteration) is not StopIteration]
            if groups:
                _run_interleaved([tile_stages(*tile) for tile in groups.pop(0)])
            if casts:
                src, dst = casts.pop(0)
                dst[...] = src[...].astype(dst.dtype)
        assert not groups and not casts

    half = len(tiles) // 2

    @pl.when(step % 2 == 0)
    def _():
        stage[...] = q_ref[...].astype(F32)
        deinterleave(q_perm[d_mid], q_perm[d_max], 0, 0, tb)
        scan_and_tiles(tiles[:half])

    @pl.when(step % 2 == 1)
    def _():
        scan_and_tiles(tiles[half:])
        lses = [l_bufs[d][...] for d in DIL_DILATIONS]
        m = functools.reduce(jnp.maximum, lses)
        es = [jnp.exp(l - m) for l in lses]
        num = sum(e * o_bufs[d][...] for e, d in zip(es, DIL_DILATIONS))
        mixed = num / sum(es)
        out_ref[...] = _head_rms(mixed, gain_ref[...]).astype(out_ref.dtype)


def _gla_dil(proj, up_f, up_b, bias_f, bias_b, dil_bias, dil_gain, w_up, w_out, w_down, layer, t=1024):
    s = proj.shape[0]
    nb = s // t
    tb = DIL_TB
    assert tb == 2 * t and GLA_HEADS == DIL_HEADS
    vcol = COL_GV * LANES // GLA_DV
    step_of = lambda h, i: h * nb + i
    n_steps = GLA_HEADS * nb
    wu_in, wu_out, wu_sds = _cast_slab(w_up, layer, n_steps, step_of)
    wo_in, wo_out, wo_sds = _cast_slab(w_out, layer, n_steps, step_of)
    wd_in, wd_out, wd_sds = _cast_slab(w_down, layer, n_steps, step_of)

    def row_specs(rmap):
        return [
            pl.BlockSpec((t, GLA_DK), lambda h, i: (rmap(i), COL_GQ + h)),
            pl.BlockSpec((t, GLA_DK), lambda h, i: (rmap(i), COL_GK + h)),
            pl.BlockSpec((t, GLA_DV), lambda h, i: (rmap(i), vcol + h)),
            pl.BlockSpec((t, LANES), lambda h, i: (rmap(i), COL_LR)),
        ]

    fwd = lambda i: i
    bwd = lambda i: nb - 1 - i
    up_spec = pl.BlockSpec((None, LANES, GLA_DK), lambda h, i: (layer, 0, h))
    bias_spec = pl.BlockSpec((None, 1, GLA_DK), lambda h, i: (layer, 0, h))
    once = pl.Buffered(1)
    dil_specs = [
        pl.BlockSpec((tb, DIL_HEAD_DIM), lambda h, i: (i // 2, COL_DQ + h)),
        pl.BlockSpec((s, DIL_HEAD_DIM), lambda h, i: (0, COL_DK + h), pipeline_mode=once),
        pl.BlockSpec((s, DIL_HEAD_DIM), lambda h, i: (0, COL_DV + h), pipeline_mode=once),
        pl.BlockSpec((len(DIL_DILATIONS), None, DIL_TQ, DIL_TK), lambda h, i: (0, h, 0, 0), pipeline_mode=once),
        pl.BlockSpec((None, 1, DIL_HEAD_DIM), lambda h, i: (layer, 0, h)),
    ]
    gla_sds = jax.ShapeDtypeStruct((s, GLA_V_WIDTH), BF16)
    state = pltpu.VMEM((GLA_DV, GLA_DK), F32)
    tri = pltpu.VMEM((GLA_BLOCK, GLA_BLOCK), BF16)
    lvl = pltpu.VMEM((GLA_BLOCK, GLA_BLOCK), jnp.int32)
    guarded = pltpu.VMEM((s + 2 * DIL_BAND, DIL_HEAD_DIM), BF16)
    tile_f32 = pltpu.VMEM((tb, DIL_HEAD_DIM), F32)
    tile_bf16 = pltpu.VMEM((tb, DIL_HEAD_DIM), BF16)
    o_f, o_b, w_up_b, dil, w_out_b, w_down_b = pl.pallas_call(
        _gla_dil_kernel,
        grid=(GLA_HEADS, nb),
        in_specs=(row_specs(fwd) + row_specs(bwd) + [up_spec, up_spec, bias_spec, bias_spec, wu_in]
                  + dil_specs + [wo_in, wd_in]),
        out_specs=[
            pl.BlockSpec((t, GLA_DV), lambda h, i: (fwd(i), h)),
            pl.BlockSpec((t, GLA_DV), lambda h, i: (bwd(i), h)),
            wu_out,
            pl.BlockSpec((tb, DIL_HEAD_DIM), lambda h, i: (i // 2, h)),
            wo_out, wd_out,
        ],
        out_shape=[gla_sds, gla_sds, wu_sds, jax.ShapeDtypeStruct((s, DIL_WIDTH), BF16), wo_sds, wd_sds],
        scratch_shapes=([state, state, tri, tri, lvl, lvl]
                        + [tile_f32] * 2 + [guarded] * 6 + [tile_bf16] * 2 + [tile_f32] * 6),
        compiler_params=_params(2, BIG_VMEM_LIMIT),
        name="gla_dil",
    )(proj, proj, proj, proj, proj, proj, proj, proj, up_f, up_b, bias_f, bias_b, w_up,
      proj, proj, proj, dil_bias, dil_gain, w_out, w_down)
    return o_f, o_b, dil, w_up_b, w_out_b, w_down_b


def _t5_bucket(rel):
    half = REL_BUCKETS // 2
    max_exact = half // 2
    ret = jnp.where(rel > 0, half, 0)
    n = jnp.abs(rel)
    nf = jnp.maximum(n, 1).astype(F32)
    large = max_exact + (jnp.log(nf / max_exact) / math.log(REL_MAX_DISTANCE / max_exact)
                         * (half - max_exact)).astype(jnp.int32)
    large = jnp.minimum(large, half - 1)
    return ret + jnp.where(n < max_exact, n, large)


def _dil_bias_tiles(rel_bias):
    rel = jnp.arange(DIL_TK)[None, :] - DIL_BAND - jnp.arange(DIL_TQ)[:, None]
    band = jnp.abs(rel) <= DIL_BAND
    tiles = []
    for d in DIL_DILATIONS:
        onehot = (_t5_bucket(rel * d)[..., None] == jnp.arange(REL_BUCKETS)).astype(F32)
        bias = jnp.einsum("qkb,bh->hqk", onehot, rel_bias.astype(F32), precision=lax.Precision.HIGHEST)
        tiles.append(jnp.where(band[None], bias, NEG_INF))
    return jnp.stack(tiles, axis=0)


def _mem_kv_kernel(mem_ref, g_ref, w_ref, o_ref):
    h = _rms(mem_ref[...], g_ref[...]).astype(BF16)
    o_ref[...] = _dot(h, w_ref[...].astype(BF16)).astype(o_ref.dtype)


def _mem_kv(mem, gain, w):
    m, d = mem.shape
    n = w.shape[-1]
    return pl.pallas_call(
        _mem_kv_kernel,
        grid=(DEPTH,),
        in_specs=[
            pl.BlockSpec((m, d), lambda l: (0, 0)),
            pl.BlockSpec((None, 1, d), lambda l: (l, 0, 0)),
            pl.BlockSpec((None, d, n), lambda l: (l, 0, 0)),
        ],
        out_specs=pl.BlockSpec((None, m, n), lambda l: (l, 0, 0)),
        out_shape=jax.ShapeDtypeStruct((DEPTH, m, n), BF16),
        compiler_params=_params(1),
        name="mem_kv",
    )(mem, gain, w)


def _head_rms(o, gain):
    return o * lax.rsqrt(jnp.mean(o * o, axis=-1, keepdims=True) + EPS) * gain


def _mix_kernel(of_ref, ob_ref, gr_ref, dil_ref, mq_ref, km_ref, vm_ref, ggla_ref, gmem_ref, w_ref, x_ref,
                *rest, prep_next):
    if prep_next:
        w_in_ref, out_ref, w_in_out, gla_ref, mem_ref = rest
        _w_in_prep_kernel(w_in_ref, w_in_out)
    else:
        out_ref, gla_ref, mem_ref = rest

    dil0 = GLA_V_WIDTH
    mem0 = GLA_V_WIDTH + DIL_WIDTH

    def mem_head(h):
        sl = slice(h * MEM_HEAD_DIM, (h + 1) * MEM_HEAD_DIM)
        raw = _dot_nt(mq_ref[:, sl], km_ref[:, sl])
        yield
        sc = raw * (MEM_HEAD_DIM ** -0.5)
        m = jnp.max(sc, axis=-1, keepdims=True)
        p = jnp.exp(sc - m)
        den = jnp.sum(p, axis=-1, keepdims=True)
        pv = _dot(p.astype(BF16), vm_ref[:, sl])
        yield
        mem_ref[:, sl] = _head_rms(pv / den, gmem_ref[:, sl]).astype(mem_ref.dtype)

    def gla_heads():
        for h in range(GLA_HEADS):
            sl = slice(h * GLA_DV, (h + 1) * GLA_DV)
            o = of_ref[:, sl].astype(F32) + ob_ref[:, sl].astype(F32)
            r = gr_ref[:, sl].astype(F32)
            gate = r * (1.0 / (1.0 + jnp.exp(-r)))
            gla_ref[:, sl] = (_head_rms(o, ggla_ref[:, sl]) * gate).astype(gla_ref.dtype)
        yield
        out_ref[...] += _dot(gla_ref[...], w_ref[0:dil0, :])

    def dil_group():
        out_ref[...] = x_ref[...] + _dot(dil_ref[...], w_ref[dil0:mem0, :])
        yield

    _run_interleaved([mem_head(h) for h in range(MEM_HEADS)] + [dil_group(), gla_heads()])
    out_ref[...] += _dot(mem_ref[...], w_ref[mem0:, :])


def _mix_out(x, proj, o_f, o_b, dil, kv, g_gla, g_mem, w_out, w_in_t, layer, t=512):
    s, d = x.shape
    n_steps = s // t
    prep_next = layer + 1 < w_in_t.shape[0]
    rows = lambda w, c=0: pl.BlockSpec((t, w), lambda i: (i, c))
    per_layer = lambda shape, c=0: pl.BlockSpec((None,) + shape, lambda i: (layer, 0, c),
                                                pipeline_mode=pl.Buffered(1))
    in_specs = [
        rows(GLA_V_WIDTH), rows(GLA_V_WIDTH),
        rows(GLA_V_WIDTH, COL_GR * LANES // GLA_V_WIDTH),
        rows(DIL_WIDTH),
        rows(MEM_WIDTH, COL_MQ * LANES // MEM_WIDTH),
        per_layer((MEM_LEN, MEM_WIDTH), 0), per_layer((MEM_LEN, MEM_WIDTH), 1),
        per_layer((1, GLA_V_WIDTH)), per_layer((1, MEM_WIDTH)),
        pl.BlockSpec((MIX_WIDTH, d), lambda i: (0, 0), pipeline_mode=pl.Buffered(1)),
        rows(d),
    ]
    operands = [o_f, o_b, proj, dil, proj, kv, kv, g_gla, g_mem, w_out, x]
    out_specs = [rows(d)]
    out_shape = [jax.ShapeDtypeStruct((s, d), F32)]
    if prep_next:
        n_in = w_in_t.shape[1]
        in_specs.append(pl.BlockSpec((None, n_in, d // n_steps), lambda i: (layer + 1, 0, i)))
        operands.append(w_in_t)
        out_specs.append(pl.BlockSpec((PROJ_WIDTH, d // n_steps), lambda i: (0, i)))
        out_shape.append(jax.ShapeDtypeStruct((PROJ_WIDTH, d), BF16))
    outs = pl.pallas_call(
        functools.partial(_mix_kernel, prep_next=prep_next),
        grid=(n_steps,),
        in_specs=in_specs,
        out_specs=out_specs,
        out_shape=out_shape,
        scratch_shapes=[pltpu.VMEM((t, GLA_V_WIDTH), BF16), pltpu.VMEM((t, MEM_WIDTH), BF16)],
        compiler_params=_params(1),
        name="mix_out_proj",
    )(*operands)
    return (outs[0], outs[1]) if prep_next else (outs[0], None)


def _mlp_kernel(x_ref, g_ref, wu_ref, wd_ref, gf_ref, o_ref, h_ref, *, final_norm, n_sub):
    f = pl.program_id(1)
    sub = x_ref.shape[0] // n_sub

    def hidden(h):
        return jnp.square(jnp.maximum(_dot(h, wu_ref[...]), 0.0)).astype(BF16)

    @pl.when(f == 0)
    def _():
        for c in range(n_sub):
            rows = slice(c * sub, (c + 1) * sub)
            x = x_ref[rows, :]
            h = _rms(x, g_ref[...]).astype(h_ref.dtype)
            h_ref[rows, :] = h
            o_ref[rows, :] = x + _dot(hidden(h), wd_ref[...])

    @pl.when(f != 0)
    def _():
        for c in range(n_sub):
            rows = slice(c * sub, (c + 1) * sub)
            o_ref[rows, :] += _dot(hidden(h_ref[rows, :]), wd_ref[...])

    if final_norm:
        @pl.when(f == pl.num_programs(1) - 1)
        def _():
            for c in range(n_sub):
                rows = slice(c * sub, (c + 1) * sub)
                o_ref[rows, :] = _rms(o_ref[rows, :], gf_ref[...])


def _mlp(x, gain, w_up, w_down, gain_final, layer, final_norm, tm=1024, tf=1024, n_sub=4):
    s, d = x.shape
    ff = w_up.shape[-1]
    return pl.pallas_call(
        functools.partial(_mlp_kernel, final_norm=final_norm, n_sub=n_sub),
        grid=(s // tm, ff // tf),
        in_specs=[
            pl.BlockSpec((tm, d), lambda i, f: (i, 0)),
            pl.BlockSpec((None, 1, d), lambda i, f: (layer, 0, 0)),
            pl.BlockSpec((d, tf), lambda i, f: (0, f)),
            pl.BlockSpec((tf, d), lambda i, f: (f, 0)),
            pl.BlockSpec((1, d), lambda i, f: (0, 0)),
        ],
        out_specs=pl.BlockSpec((tm, d), lambda i, f: (i, 0)),
        out_shape=jax.ShapeDtypeStruct((s, d), F32),
        scratch_shapes=[pltpu.VMEM((tm, d), BF16)],
        compiler_params=_params(2, BIG_VMEM_LIMIT),
        name="mlp",
    )(x, gain, w_up, w_down, gain_final)


def _w_in_prep_kernel(w_ref, o_ref):
    lr0 = 2 * GLA_QK_WIDTH + 2 * GLA_V_WIDTH
    lr_w = 2 * GLA_GATE_RANK
    tail = w_ref.shape[0] - lr0 - lr_w
    o_ref[0:lr0, :] = w_ref[0:lr0, :].astype(o_ref.dtype)
    o_ref[lr0:lr0 + tail, :] = w_ref[lr0 + lr_w:, :].astype(o_ref.dtype)
    o_ref[lr0 + tail:lr0 + tail + lr_w, :] = w_ref[lr0:lr0 + lr_w, :].astype(o_ref.dtype)
    o_ref[lr0 + tail + lr_w:, :] = jnp.zeros((o_ref.shape[0] - lr0 - tail - lr_w, o_ref.shape[1]), o_ref.dtype)


def _reorder_w_in(w_in_t, layer, cols=256):
    _, n, d = w_in_t.shape
    assert COL_LR * LANES == n - 2 * GLA_GATE_RANK
    return pl.pallas_call(
        _w_in_prep_kernel,
        grid=(d // cols,),
        in_specs=[pl.BlockSpec((None, n, cols), lambda i: (layer, 0, i))],
        out_specs=pl.BlockSpec((PROJ_WIDTH, cols), lambda i: (0, i)),
        out_shape=jax.ShapeDtypeStruct((PROJ_WIDTH, d), BF16),
        compiler_params=_params(1),
        name="w_in_prep",
    )(w_in_t)


def _pad_gate_up(up, row0):
    out = jnp.zeros((up.shape[0], LANES, up.shape[-1]), BF16)
    return out.at[:, row0:row0 + GLA_GATE_RANK, :].set(up.astype(BF16))


def kernel(x, mem, norm_mix, w_in, gla_gate_up_fwd, gla_gate_bias_fwd, gla_gate_up_bwd, gla_gate_bias_bwd, gla_norm, rel_bias, dil_norm, mem_norm, w_mem_kv, mem_out_norm, w_out, norm_mlp, w_up, w_down, norm_final):
    batch, seq, d = x.shape
    assert batch == 1 and seq == SEQ and d == D_MODEL
    xs = x.reshape(seq, d)
    row = lambda g: g.reshape(g.shape[0], 1, g.shape[-1])

    w_in_t = jnp.swapaxes(w_in, 1, 2)
    w_in_l = _reorder_w_in(w_in_t, 0)
    up_f = _pad_gate_up(gla_gate_up_fwd, 0)
    up_b = _pad_gate_up(gla_gate_up_bwd, GLA_GATE_RANK)
    dil_bias = _dil_bias_tiles(rel_bias)
    kv = _mem_kv(mem.reshape(MEM_LEN, d), row(mem_norm), w_mem_kv)

    for layer in range(DEPTH):
        proj = _in_proj(xs, row(norm_mix), w_in_l, layer)
        o_f, o_b, dil, w_up_b, w_out_b, w_down_b = _gla_dil(
            proj, up_f, up_b, row(gla_gate_bias_fwd), row(gla_gate_bias_bwd), dil_bias, row(dil_norm),
            w_up, w_out, w_down, layer)
        xs, w_in_l = _mix_out(xs, proj, o_f, o_b, dil, kv, row(gla_norm), row(mem_out_norm), w_out_b,
                              w_in_t, layer)
        xs = _mlp(xs, row(norm_mlp), w_up_b, w_down_b, norm_final.reshape(1, d), layer,
                  final_norm=(layer == DEPTH - 1))
    return xs.reshape(batch, seq, d)
```

```python
import functools
import math

import jax
import jax.numpy as jnp
from jax import lax
from jax.experimental import pallas as pl
from jax.experimental.pallas import tpu as pltpu

F32 = jnp.float32
BF16 = jnp.bfloat16

D_MODEL = 2048
SEQ = 8192
DEPTH = 4
MEM_LEN = 256
GLA_HEADS = 4
GLA_DK = 128
GLA_DV = 256
GLA_GATE_RANK = 16
GLA_GATE_NORMALIZER = 16.0
GLA_CHUNK = 64
DIL_HEADS = 4
DIL_HEAD_DIM = 128
DIL_CONFIGS = ((128, 1), (512, 4), (2048, 16))
MEM_HEADS = 4
MEM_HEAD_DIM = 128
REL_BUCKETS = 32
REL_MAX_DISTANCE = 1024
D_FF = 4 * D_MODEL
EPS = 1e-6
NEG_INF = -1e30

GLA_QK_WIDTH = GLA_HEADS * GLA_DK
GLA_V_WIDTH = GLA_HEADS * GLA_DV
DIL_WIDTH = DIL_HEADS * DIL_HEAD_DIM
MEM_WIDTH = MEM_HEADS * MEM_HEAD_DIM
MIX_WIDTH = GLA_V_WIDTH + DIL_WIDTH + MEM_WIDTH

LANES = 128

COL_GQ = 0
COL_GK = COL_GQ + GLA_QK_WIDTH // LANES
COL_GV = COL_GK + GLA_QK_WIDTH // LANES
COL_GR = COL_GV + GLA_V_WIDTH // LANES
COL_DQ = COL_GR + GLA_V_WIDTH // LANES
COL_DK = COL_DQ + DIL_WIDTH // LANES
COL_DV = COL_DK + DIL_WIDTH // LANES
COL_MQ = COL_DV + DIL_WIDTH // LANES
COL_LR = COL_MQ + MEM_WIDTH // LANES
PROJ_COLS = COL_LR + 2
PROJ_WIDTH = PROJ_COLS * LANES

GLA_BLOCK = 256

DIL_BAND = 64
DIL_TQ = 128
DIL_TK = DIL_TQ + 2 * DIL_BAND
DIL_DILATIONS = tuple(d for _, d in DIL_CONFIGS)
DIL_TB = DIL_TQ * max(DIL_DILATIONS)
DIL_INTERLEAVE = 8

VMEM_LIMIT = 48 * 1024 * 1024
BIG_VMEM_LIMIT = 60 * 1024 * 1024


def _params(n_axes, vmem=VMEM_LIMIT):
    return pltpu.CompilerParams(dimension_semantics=("arbitrary",) * n_axes, vmem_limit_bytes=vmem)


def _dot(a, b):
    return jnp.dot(a, b, preferred_element_type=F32)


def _dot_nt(a, b):
    return lax.dot_general(a, b, (((1,), (1,)), ((), ())), preferred_element_type=F32)


def _dot_tn(a, b):
    return lax.dot_general(a, b, (((0,), (0,)), ((), ())), preferred_element_type=F32)


def _rms(x, gain):
    return x * lax.rsqrt(jnp.mean(x * x, axis=-1, keepdims=True) + EPS) * gain


def _run_interleaved(stage_generators):
    live = list(stage_generators)
    while live:
        live = [gen for gen in live if next(gen, StopIteration) is not StopIteration]


def _in_proj_kernel(x_ref, g_ref, w_ref, o_ref, h_ref, *, n_sub):
    j = pl.program_id(1)
    sub = x_ref.shape[0] // n_sub

    @pl.when(j == 0)
    def _():
        for c in range(n_sub):
            rows = slice(c * sub, (c + 1) * sub)
            h = _rms(x_ref[rows, :], g_ref[...]).astype(h_ref.dtype)
            h_ref[rows, :] = h
            o_ref[rows, :] = _dot_nt(h, w_ref[...]).astype(o_ref.dtype)

    @pl.when(j != 0)
    def _():
        for c in range(n_sub):
            rows = slice(c * sub, (c + 1) * sub)
            o_ref[rows, :] = _dot_nt(h_ref[rows, :], w_ref[...]).astype(o_ref.dtype)


def _in_proj(x, gain, w, layer, tm=1024, tn=1792, n_sub=4):
    s, d = x.shape
    n = w.shape[0]
    return pl.pallas_call(
        functools.partial(_in_proj_kernel, n_sub=n_sub),
        grid=(s // tm, n // tn),
        in_specs=[
            pl.BlockSpec((tm, d), lambda i, j: (i, 0)),
            pl.BlockSpec((None, 1, d), lambda i, j: (layer, 0, 0)),
            pl.BlockSpec((tn, d), lambda i, j: (j, 0)),
        ],
        out_specs=pl.BlockSpec((tm, tn), lambda i, j: (i, j)),
        out_shape=jax.ShapeDtypeStruct((s, n), BF16),
        scratch_shapes=[pltpu.VMEM((tm, d), BF16)],
        compiler_params=_params(2),
        name="in_proj",
    )(x, gain, w)


def _log_sigmoid(x):
    return jnp.minimum(x, 0.0) - jnp.log(1.0 + jnp.exp(-jnp.abs(x)))


GLA_LEVELS = (GLA_BLOCK // GLA_CHUNK).bit_length() - 1


def _gla_tables(tri_ref, lvl_ref, reverse):
    t = GLA_BLOCK
    row = lax.broadcasted_iota(jnp.int32, (t, t), 0)
    col = lax.broadcasted_iota(jnp.int32, (t, t), 1)
    valid = (col >= row) if reverse else (col <= row)
    tri_ref[...] = jnp.where(valid, 1.0, 0.0).astype(tri_ref.dtype)
    shift = GLA_CHUNK.bit_length() - 1
    lvl_ref[...] = jnp.where(jnp.right_shift(row, shift) == jnp.right_shift(col, shift),
                             jnp.where(valid, 0, -1), -1)


def _gla_boundary(cum, group, reverse):
    t = cum.shape[0]
    pieces = []
    for a in range(0, t, group):
        idx = a + group if reverse else a - 1
        if idx < 0 or idx >= t:
            ref_row = jnp.zeros((1, cum.shape[1]), cum.dtype)
        else:
            ref_row = cum[idx:idx + 1]
        pieces.append(jnp.broadcast_to(ref_row, (group, cum.shape[1])))
    return jnp.concatenate(pieces, axis=0)


def _place_rows(part, row0, total_rows):
    pieces = []
    if row0 > 0:
        pieces.append(jnp.zeros((row0, part.shape[1]), part.dtype))
    pieces.append(part)
    rest = total_rows - row0 - part.shape[0]
    if rest > 0:
        pieces.append(jnp.zeros((rest, part.shape[1]), part.dtype))
    return jnp.concatenate(pieces, axis=0)


def _gla_block(q_ref, k_ref, v_ref, lr_ref, up_ref, bias_ref, o_ref, s_ref, tri_ref, lvl_ref, r0, reverse):
    t = GLA_BLOCK
    rows = slice(r0, r0 + t)
    logits = _dot(lr_ref[rows, :], up_ref[...]) + bias_ref[...]
    yield
    g = _log_sigmoid(logits) * (1.0 / GLA_GATE_NORMALIZER)
    g_hi = g.astype(BF16)
    g_lo = (g - g_hi.astype(F32)).astype(BF16)
    cum2 = _dot(tri_ref[...], jnp.concatenate([g_hi, g_lo], axis=1))
    yield
    cum = cum2[:, :GLA_DK] + cum2[:, GLA_DK:]

    q = q_ref[rows, :].astype(F32) * (GLA_DK ** -0.5)
    k = k_ref[rows, :].astype(F32)
    v = v_ref[rows, :]
    lvl = lvl_ref[...]

    ref0 = _gla_boundary(cum, GLA_CHUNK, reverse)
    q_dec = (q * jnp.exp(cum - ref0)).astype(BF16)
    k_inv = (k * jnp.exp(ref0 - cum)).astype(BF16)
    q_segs, k_segs = [], []
    for level in range(1, GLA_LEVELS + 1):
        group = GLA_CHUNK << level
        for a0 in range(0, t, group):
            mid = a0 + group // 2
            att = slice(a0, mid) if reverse else slice(mid, a0 + group)
            src = slice(mid, a0 + group) if reverse else slice(a0, mid)
            ref_row = cum[mid:mid + 1] if reverse else cum[mid - 1:mid]
            q_part = (q[att] * jnp.exp(cum[att] - ref_row)).astype(BF16)
            k_part = (k[src] * jnp.exp(ref_row - cum[src])).astype(BF16)
            q_segs.append(_place_rows(q_part, att.start, t))
            k_segs.append(_place_rows(k_part, src.start, t))
    cross = _dot_nt(jnp.concatenate(q_segs, axis=1), jnp.concatenate(k_segs, axis=1))
    diag = _dot_nt(q_dec, k_inv)
    total = cum[0:1] if reverse else cum[t - 1:t]
    q_in = (q * jnp.exp(cum)).astype(BF16)
    k_out = (k * jnp.exp(total - cum)).astype(BF16)
    carry_in = _dot_tn(v, k_out)
    yield
    a = jnp.where(lvl == 0, diag, cross)
    o_local = _dot(a.astype(BF16), v)
    yield
    state = s_ref[...]
    o = o_local + _dot_nt(q_in, state.astype(BF16))
    o_ref[rows, :] = o.astype(o_ref.dtype)
    s_ref[...] = state * jnp.exp(total) + carry_in


def _cast_slab(weights, layer, n_steps, step_of):
    rows, cols = weights.shape[1:]
    slab = rows // n_steps
    return (pl.BlockSpec((None, slab, cols), lambda *g: (layer, step_of(*g), 0)),
            pl.BlockSpec((slab, cols), lambda *g: (step_of(*g), 0)),
            jax.ShapeDtypeStruct((rows, cols), BF16))


def _gla_kernel(qf, kf, vf, lrf, qb, kb, vb, lrb, upf, upb, bsf, bsb, wu_ref, of, ob, wu_out,
                sf, sb, tri_f, tri_b, lvl_f, lvl_b, *, interleave):
    @pl.when(pl.program_id(1) == 0)
    def _():
        sf[...] = jnp.zeros_like(sf)
        sb[...] = jnp.zeros_like(sb)
        _gla_tables(tri_f, lvl_f, reverse=False)
        _gla_tables(tri_b, lvl_b, reverse=True)

    starts = list(range(0, qf.shape[0], GLA_BLOCK))
    n_groups = len(starts) // interleave

    def cast_piece(piece):
        n = wu_ref.shape[0] // n_groups
        wu_out[piece * n:(piece + 1) * n, :] = wu_ref[piece * n:(piece + 1) * n, :].astype(wu_out.dtype)

    for gi in range(n_groups):
        blocks = []
        for r_fwd, r_bwd in list(zip(starts, reversed(starts)))[gi * interleave:(gi + 1) * interleave]:
            blocks.append(_gla_block(qf, kf, vf, lrf, upf, bsf, of, sf, tri_f, lvl_f, r_fwd, reverse=False))
            blocks.append(_gla_block(qb, kb, vb, lrb, upb, bsb, ob, sb, tri_b, lvl_b, r_bwd, reverse=True))
        _run_interleaved(blocks)
        cast_piece(gi)


def _gla(proj, up_f, up_b, bias_f, bias_b, w_up, layer, t=2048, interleave=8):
    s = proj.shape[0]
    nb = s // t
    vcol = COL_GV * LANES // GLA_DV
    wu_in, wu_out, wu_sds = _cast_slab(w_up, layer, GLA_HEADS * nb, lambda h, i: h * nb + i)

    def row_specs(rmap):
        return [
            pl.BlockSpec((t, GLA_DK), lambda h, i: (rmap(i), COL_GQ + h)),
            pl.BlockSpec((t, GLA_DK), lambda h, i: (rmap(i), COL_GK + h)),
            pl.BlockSpec((t, GLA_DV), lambda h, i: (rmap(i), vcol + h)),
            pl.BlockSpec((t, LANES), lambda h, i: (rmap(i), COL_LR)),
        ]

    fwd = lambda i: i
    bwd = lambda i: nb - 1 - i
    up_spec = pl.BlockSpec((None, LANES, GLA_DK), lambda h, i: (layer, 0, h))
    bias_spec = pl.BlockSpec((None, 1, GLA_DK), lambda h, i: (layer, 0, h))
    out_sds = jax.ShapeDtypeStruct((s, GLA_V_WIDTH), BF16)
    state = pltpu.VMEM((GLA_DV, GLA_DK), F32)
    tri = pltpu.VMEM((GLA_BLOCK, GLA_BLOCK), BF16)
    lvl = pltpu.VMEM((GLA_BLOCK, GLA_BLOCK), jnp.int32)
    return pl.pallas_call(
        functools.partial(_gla_kernel, interleave=interleave),
        grid=(GLA_HEADS, nb),
        in_specs=row_specs(fwd) + row_specs(bwd) + [up_spec, up_spec, bias_spec, bias_spec, wu_in],
        out_specs=[
            pl.BlockSpec((t, GLA_DV), lambda h, i: (fwd(i), h)),
            pl.BlockSpec((t, GLA_DV), lambda h, i: (bwd(i), h)),
            wu_out,
        ],
        out_shape=[out_sds, out_sds, wu_sds],
        scratch_shapes=[state, state, tri, tri, lvl, lvl],
        compiler_params=_params(2),
        name="gla_scan",
    )(proj, proj, proj, proj, proj, proj, proj, proj, up_f, up_b, bias_f, bias_b, w_up)


def _dil_tile(q, k, v, bias, edge, o_dst, l_dst, rows):
    raw = _dot_nt(q, k)
    yield
    sc = raw * (DIL_HEAD_DIM ** -0.5) + bias
    if edge is not None:
        sc = sc + edge
    m = jnp.max(sc, axis=-1, keepdims=True)
    p = jnp.exp(sc - m).astype(BF16)
    pv = _dot(p, jnp.concatenate([v, jnp.ones_like(v)], axis=1))
    yield
    den = pv[:, DIL_HEAD_DIM:]
    o_dst[rows, :] = pv[:, :DIL_HEAD_DIM] / den
    l_dst[rows, :] = m + jnp.log(den)


def _dil_kernel(q_ref, k_ref, v_ref, bias_ref, gain_ref, wo_ref, wd_ref, out_ref, wo_out, wd_out,
                stage, stage_mid, kg1, vg1, kg4, vg4, kg16, vg16, qp4, qp16, o1, o4, o16, l1, l4, l16):
    i = pl.program_id(1)
    s = k_ref.shape[0]
    tb = q_ref.shape[0]
    k_bufs = dict(zip(DIL_DILATIONS, (kg1, kg4, kg16)))
    v_bufs = dict(zip(DIL_DILATIONS, (vg1, vg4, vg16)))
    q_perm = dict(zip(DIL_DILATIONS, (None, qp4, qp16)))
    o_bufs = dict(zip(DIL_DILATIONS, (o1, o4, o16)))
    l_bufs = dict(zip(DIL_DILATIONS, (l1, l4, l16)))
    _, d_mid, d_max = DIL_DILATIONS
    assert d_max == d_mid * d_mid

    def deinterleave(dst_mid, dst_max, base_mid, base_max, residue_rows):
        per_mid, per_max = tb // d_mid, tb // d_max

        def at(base, offset):
            start = base + offset
            return start if isinstance(start, int) else pl.multiple_of(start, DIL_BAND)

        for r in range(d_mid):
            part = stage[pl.ds(r, per_mid, stride=d_mid), :]
            stage_mid[r * per_mid:(r + 1) * per_mid, :] = part
            dst_mid[pl.ds(at(base_mid, r * (residue_rows // d_mid)), per_mid), :] = part.astype(dst_mid.dtype)
        for r in range(d_max):
            r_mid, m = r % d_mid, r // d_mid
            part = stage_mid[pl.ds(r_mid * per_mid + m, per_max, stride=d_mid), :]
            dst_max[pl.ds(at(base_max, r * (residue_rows // d_max)), per_max), :] = part.astype(dst_max.dtype)

    @pl.when(i == 0)
    def _():
        zeros = jnp.zeros((DIL_BAND, DIL_HEAD_DIM), BF16)
        for buf in (kg1, vg1, kg4, vg4, kg16, vg16):
            buf[0:DIL_BAND, :] = zeros
            buf[DIL_BAND + s:, :] = zeros
        kg1[DIL_BAND:DIL_BAND + s, :] = k_ref[...]
        vg1[DIL_BAND:DIL_BAND + s, :] = v_ref[...]

        def body(c, carry):
            base = pl.multiple_of(c * tb, tb)
            for src, bufs in ((k_ref, k_bufs), (v_ref, v_bufs)):
                stage[...] = src[pl.ds(base, tb), :].astype(F32)
                deinterleave(bufs[d_mid], bufs[d_max], DIL_BAND + c * (tb // d_mid),
                             DIL_BAND + c * (tb // d_max), s)
            return carry

        lax.fori_loop(0, s // tb, body, 0)

    stage[...] = q_ref[...].astype(F32)
    deinterleave(q_perm[d_mid], q_perm[d_max], 0, 0, tb)

    lane = lax.broadcasted_iota(jnp.int32, (1, DIL_TK), 1)
    tiles = []
    for b, d in enumerate(DIL_DILATIONS):
        sub_len = s // d
        per = tb // d
        n_tiles = per // DIL_TQ
        for r in range(d):
            for tj in range(n_tiles):
                tiles.append((b, d, sub_len, per, n_tiles, r, tj))

    def tile_stages(b, d, sub_len, per, n_tiles, r, tj):
        u0 = i * per + tj * DIL_TQ
        q_src = q_ref if d == 1 else q_perm[d]
        q = q_src[r * per + tj * DIL_TQ:r * per + (tj + 1) * DIL_TQ, :]
        k_start = pl.multiple_of(r * sub_len + u0, DIL_TQ)
        k = k_bufs[d][pl.ds(k_start, DIL_TK), :]
        v = v_bufs[d][pl.ds(k_start, DIL_TK), :]
        edge = None
        if tj == 0 or tj == n_tiles - 1:
            key_pos = u0 - DIL_BAND + lane
            edge = jnp.where((key_pos >= 0) & (key_pos < sub_len), 0.0, NEG_INF)
        if d == 1:
            rows = pl.ds(tj * DIL_TQ, DIL_TQ)
        else:
            rows = pl.ds(r + d * tj * DIL_TQ, DIL_TQ, stride=d)
        return _dil_tile(q, k, v, bias_ref[b], edge, o_bufs[d], l_bufs[d], rows)

    for g0 in range(0, len(tiles), DIL_INTERLEAVE):
        _run_interleaved([tile_stages(*tile) for tile in tiles[g0:g0 + DIL_INTERLEAVE]])

    lses = [l_bufs[d][...] for d in DIL_DILATIONS]
    m = functools.reduce(jnp.maximum, lses)
    es = [jnp.exp(l - m) for l in lses]
    num = sum(e * o_bufs[d][...] for e, d in zip(es, DIL_DILATIONS))
    mixed = num / sum(es)
    out_ref[...] = _head_rms(mixed, gain_ref[...]).astype(out_ref.dtype)

    wo_out[...] = wo_ref[...].astype(wo_out.dtype)
    wd_out[...] = wd_ref[...].astype(wd_out.dtype)


def _dil_attention(proj, bias, gain, w_out, w_down, layer):
    s = proj.shape[0]
    tb = DIL_TB
    nq = s // tb
    step_of = lambda h, i: h * nq + i
    wo_in, wo_out, wo_sds = _cast_slab(w_out, layer, DIL_HEADS * nq, step_of)
    wd_in, wd_out, wd_sds = _cast_slab(w_down, layer, DIL_HEADS * nq, step_of)
    guarded = pltpu.VMEM((s + 2 * DIL_BAND, DIL_HEAD_DIM), BF16)
    tile_f32 = pltpu.VMEM((tb, DIL_HEAD_DIM), F32)
    tile_bf16 = pltpu.VMEM((tb, DIL_HEAD_DIM), BF16)
    return pl.pallas_call(
        _dil_kernel,
        grid=(DIL_HEADS, nq),
        in_specs=[
            pl.BlockSpec((tb, DIL_HEAD_DIM), lambda h, i: (i, COL_DQ + h)),
            pl.BlockSpec((s, DIL_HEAD_DIM), lambda h, i: (0, COL_DK + h)),
            pl.BlockSpec((s, DIL_HEAD_DIM), lambda h, i: (0, COL_DV + h)),
            pl.BlockSpec((len(DIL_DILATIONS), None, DIL_TQ, DIL_TK), lambda h, i: (0, h, 0, 0)),
            pl.BlockSpec((None, 1, DIL_HEAD_DIM), lambda h, i: (layer, 0, h)),
            wo_in, wd_in,
        ],
        out_specs=[pl.BlockSpec((tb, DIL_HEAD_DIM), lambda h, i: (i, h)), wo_out, wd_out],
        out_shape=[jax.ShapeDtypeStruct((s, DIL_WIDTH), BF16), wo_sds, wd_sds],
        scratch_shapes=[tile_f32] * 2 + [guarded] * 6 + [tile_bf16] * 2 + [tile_f32] * 6,
        compiler_params=_params(2, BIG_VMEM_LIMIT),
        name="dil_attn",
    )(proj, proj, proj, bias, gain, w_out, w_down)


def _t5_bucket(rel):
    half = REL_BUCKETS // 2
    max_exact = half // 2
    ret = jnp.where(rel > 0, half, 0)
    n = jnp.abs(rel)
    nf = jnp.maximum(n, 1).astype(F32)
    large = max_exact + (jnp.log(nf / max_exact) / math.log(REL_MAX_DISTANCE / max_exact)
                         * (half - max_exact)).astype(jnp.int32)
    large = jnp.minimum(large, half - 1)
    return ret + jnp.where(n < max_exact, n, large)


def _dil_bias_tiles(rel_bias):
    rel = jnp.arange(DIL_TK)[None, :] - DIL_BAND - jnp.arange(DIL_TQ)[:, None]
    band = jnp.abs(rel) <= DIL_BAND
    tiles = []
    for d in DIL_DILATIONS:
        onehot = (_t5_bucket(rel * d)[..., None] == jnp.arange(REL_BUCKETS)).astype(F32)
        bias = jnp.einsum("qkb,bh->hqk", onehot, rel_bias.astype(F32), precision=lax.Precision.HIGHEST)
        tiles.append(jnp.where(band[None], bias, NEG_INF))
    return jnp.stack(tiles, axis=0)


def _mem_kv_kernel(mem_ref, g_ref, w_ref, o_ref):
    h = _rms(mem_ref[...], g_ref[...]).astype(BF16)
    o_ref[...] = _dot(h, w_ref[...].astype(BF16)).astype(o_ref.dtype)


def _mem_kv(mem, gain, w):
    m, d = mem.shape
    n = w.shape[-1]
    return pl.pallas_call(
        _mem_kv_kernel,
        grid=(DEPTH,),
        in_specs=[
            pl.BlockSpec((m, d), lambda l: (0, 0)),
            pl.BlockSpec((None, 1, d), lambda l: (l, 0, 0)),
            pl.BlockSpec((None, d, n), lambda l: (l, 0, 0)),
        ],
        out_specs=pl.BlockSpec((None, m, n), lambda l: (l, 0, 0)),
        out_shape=jax.ShapeDtypeStruct((DEPTH, m, n), BF16),
        compiler_params=_params(1),
        name="mem_kv",
    )(mem, gain, w)


def _head_rms(o, gain):
    return o * lax.rsqrt(jnp.mean(o * o, axis=-1, keepdims=True) + EPS) * gain


def _mix_kernel(of_ref, ob_ref, gr_ref, dil_ref, mq_ref, km_ref, vm_ref, ggla_ref, gmem_ref, w_ref, x_ref,
                *rest, prep_next):
    if prep_next:
        w_in_ref, out_ref, w_in_out, gla_ref, mem_ref = rest
        _w_in_prep_kernel(w_in_ref, w_in_out)
    else:
        out_ref, gla_ref, mem_ref = rest

    dil0 = GLA_V_WIDTH
    mem0 = GLA_V_WIDTH + DIL_WIDTH

    def mem_head(h):
        sl = slice(h * MEM_HEAD_DIM, (h + 1) * MEM_HEAD_DIM)
        raw = _dot_nt(mq_ref[:, sl], km_ref[:, sl])
        yield
        sc = raw * (MEM_HEAD_DIM ** -0.5)
        m = jnp.max(sc, axis=-1, keepdims=True)
        p = jnp.exp(sc - m)
        den = jnp.sum(p, axis=-1, keepdims=True)
        pv = _dot(p.astype(BF16), vm_ref[:, sl])
        yield
        mem_ref[:, sl] = _head_rms(pv / den, gmem_ref[:, sl]).astype(mem_ref.dtype)

    def gla_heads():
        for h in range(GLA_HEADS):
            sl = slice(h * GLA_DV, (h + 1) * GLA_DV)
            o = of_ref[:, sl].astype(F32) + ob_ref[:, sl].astype(F32)
            r = gr_ref[:, sl].astype(F32)
            gate = r * (1.0 / (1.0 + jnp.exp(-r)))
            gla_ref[:, sl] = (_head_rms(o, ggla_ref[:, sl]) * gate).astype(gla_ref.dtype)
        yield
        out_ref[...] += _dot(gla_ref[...], w_ref[0:dil0, :])

    def dil_group():
        out_ref[...] = x_ref[...] + _dot(dil_ref[...], w_ref[dil0:mem0, :])
        yield

    _run_interleaved([mem_head(h) for h in range(MEM_HEADS)] + [dil_group(), gla_heads()])
    out_ref[...] += _dot(mem_ref[...], w_ref[mem0:, :])


def _mix_out(x, proj, o_f, o_b, dil, kv, g_gla, g_mem, w_out, w_in_t, layer, t=512):
    s, d = x.shape
    n_steps = s // t
    prep_next = layer + 1 < w_in_t.shape[0]
    rows = lambda w, c=0: pl.BlockSpec((t, w), lambda i: (i, c))
    per_layer = lambda shape, c=0: pl.BlockSpec((None,) + shape, lambda i: (layer, 0, c),
                                                pipeline_mode=pl.Buffered(1))
    in_specs = [
        rows(GLA_V_WIDTH), rows(GLA_V_WIDTH),
        rows(GLA_V_WIDTH, COL_GR * LANES // GLA_V_WIDTH),
        rows(DIL_WIDTH),
        rows(MEM_WIDTH, COL_MQ * LANES // MEM_WIDTH),
        per_layer((MEM_LEN, MEM_WIDTH), 0), per_layer((MEM_LEN, MEM_WIDTH), 1),
        per_layer((1, GLA_V_WIDTH)), per_layer((1, MEM_WIDTH)),
        pl.BlockSpec((MIX_WIDTH, d), lambda i: (0, 0), pipeline_mode=pl.Buffered(1)),
        rows(d),
    ]
    operands = [o_f, o_b, proj, dil, proj, kv, kv, g_gla, g_mem, w_out, x]
    out_specs = [rows(d)]
    out_shape = [jax.ShapeDtypeStruct((s, d), F32)]
    if prep_next:
        n_in = w_in_t.shape[1]
        in_specs.append(pl.BlockSpec((None, n_in, d // n_steps), lambda i: (layer + 1, 0, i)))
        operands.append(w_in_t)
        out_specs.append(pl.BlockSpec((PROJ_WIDTH, d // n_steps), lambda i: (0, i)))
        out_shape.append(jax.ShapeDtypeStruct((PROJ_WIDTH, d), BF16))
    outs = pl.pallas_call(
        functools.partial(_mix_kernel, prep_next=prep_next),
        grid=(n_steps,),
        in_specs=in_specs,
        out_specs=out_specs,
        out_shape=out_shape,
        scratch_shapes=[pltpu.VMEM((t, GLA_V_WIDTH), BF16), pltpu.VMEM((t, MEM_WIDTH), BF16)],
        compiler_params=_params(1),
        name="mix_out_proj",
    )(*operands)
    return (outs[0], outs[1]) if prep_next else (outs[0], None)


def _mlp_kernel(x_ref, g_ref, wu_ref, wd_ref, gf_ref, o_ref, h_ref, *, final_norm, n_sub):
    f = pl.program_id(1)
    sub = x_ref.shape[0] // n_sub

    def hidden(h):
        return jnp.square(jnp.maximum(_dot(h, wu_ref[...]), 0.0)).astype(BF16)

    @pl.when(f == 0)
    def _():
        for c in range(n_sub):
            rows = slice(c * sub, (c + 1) * sub)
            x = x_ref[rows, :]
            h = _rms(x, g_ref[...]).astype(h_ref.dtype)
            h_ref[rows, :] = h
            o_ref[rows, :] = x + _dot(hidden(h), wd_ref[...])

    last = pl.num_programs(1) - 1
    middle = (f != 0) & (f != last) if final_norm else f != 0

    @pl.when(middle)
    def _():
        for c in range(n_sub):
            rows = slice(c * sub, (c + 1) * sub)
            o_ref[rows, :] += _dot(hidden(h_ref[rows, :]), wd_ref[...])

    if final_norm:
        @pl.when(f == last)
        def _():
            for c in range(n_sub):
                rows = slice(c * sub, (c + 1) * sub)
                acc = o_ref[rows, :] + _dot(hidden(h_ref[rows, :]), wd_ref[...])
                o_ref[rows, :] = _rms(acc, gf_ref[...])


def _mlp(x, gain, w_up, w_down, gain_final, layer, final_norm, tm=1024, tf=1024, n_sub=4):
    s, d = x.shape
    ff = w_up.shape[-1]
    return pl.pallas_call(
        functools.partial(_mlp_kernel, final_norm=final_norm, n_sub=n_sub),
        grid=(s // tm, ff // tf),
        in_specs=[
            pl.BlockSpec((tm, d), lambda i, f: (i, 0)),
            pl.BlockSpec((None, 1, d), lambda i, f: (layer, 0, 0)),
            pl.BlockSpec((d, tf), lambda i, f: (0, f)),
            pl.BlockSpec((tf, d), lambda i, f: (f, 0)),
            pl.BlockSpec((1, d), lambda i, f: (0, 0)),
        ],
        out_specs=pl.BlockSpec((tm, d), lambda i, f: (i, 0)),
        out_shape=jax.ShapeDtypeStruct((s, d), F32),
        scratch_shapes=[pltpu.VMEM((tm, d), BF16)],
        compiler_params=_params(2, BIG_VMEM_LIMIT),
        name="mlp",
    )(x, gain, w_up, w_down, gain_final)


def _w_in_prep_kernel(w_ref, o_ref):
    lr0 = 2 * GLA_QK_WIDTH + 2 * GLA_V_WIDTH
    lr_w = 2 * GLA_GATE_RANK
    tail = w_ref.shape[0] - lr0 - lr_w
    o_ref[0:lr0, :] = w_ref[0:lr0, :].astype(o_ref.dtype)
    o_ref[lr0:lr0 + tail, :] = w_ref[lr0 + lr_w:, :].astype(o_ref.dtype)
    o_ref[lr0 + tail:lr0 + tail + lr_w, :] = w_ref[lr0:lr0 + lr_w, :].astype(o_ref.dtype)
    o_ref[lr0 + tail + lr_w:, :] = jnp.zeros((o_ref.shape[0] - lr0 - tail - lr_w, o_ref.shape[1]), o_ref.dtype)


def _reorder_w_in(w_in_t, layer, cols=256):
    _, n, d = w_in_t.shape
    assert COL_LR * LANES == n - 2 * GLA_GATE_RANK
    return pl.pallas_call(
        _w_in_prep_kernel,
        grid=(d // cols,),
        in_specs=[pl.BlockSpec((None, n, cols), lambda i: (layer, 0, i))],
        out_specs=pl.BlockSpec((PROJ_WIDTH, cols), lambda i: (0, i)),
        out_shape=jax.ShapeDtypeStruct((PROJ_WIDTH, d), BF16),
        compiler_params=_params(1),
        name="w_in_prep",
    )(w_in_t)


def _pad_gate_up(up, row0):
    out = jnp.zeros((up.shape[0], LANES, up.shape[-1]), BF16)
    return out.at[:, row0:row0 + GLA_GATE_RANK, :].set(up.astype(BF16))


def kernel(x, mem, norm_mix, w_in, gla_gate_up_fwd, gla_gate_bias_fwd, gla_gate_up_bwd, gla_gate_bias_bwd, gla_norm, rel_bias, dil_norm, mem_norm, w_mem_kv, mem_out_norm, w_out, norm_mlp, w_up, w_down, norm_final):
    batch, seq, d = x.shape
    assert batch == 1 and seq == SEQ and d == D_MODEL
    xs = x.reshape(seq, d)
    row = lambda g: g.reshape(g.shape[0], 1, g.shape[-1])

    w_in_t = jnp.swapaxes(w_in, 1, 2)
    w_in_l = _reorder_w_in(w_in_t, 0)
    up_f = _pad_gate_up(gla_gate_up_fwd, 0)
    up_b = _pad_gate_up(gla_gate_up_bwd, GLA_GATE_RANK)
    dil_bias = _dil_bias_tiles(rel_bias)
    kv = _mem_kv(mem.reshape(MEM_LEN, d), row(mem_norm), w_mem_kv)

    for layer in range(DEPTH):
        proj = _in_proj(xs, row(norm_mix), w_in_l, layer)
        o_f, o_b, w_up_b = _gla(proj, up_f, up_b, row(gla_gate_bias_fwd), row(gla_gate_bias_bwd), w_up, layer)
        dil, w_out_b, w_down_b = _dil_attention(proj, dil_bias, row(dil_norm), w_out, w_down, layer)
        xs, w_in_l = _mix_out(xs, proj, o_f, o_b, dil, kv, row(gla_norm), row(mem_out_norm), w_out_b,
                              w_in_t, layer)
        xs = _mlp(xs, row(norm_mlp), w_up_b, w_down_b, norm_final.reshape(1, d), layer,
                  final_norm=(layer == DEPTH - 1))
    return xs.reshape(batch, seq, d)
```

```python
import functools
import math

import jax
import jax.numpy as jnp
from jax import lax
from jax.experimental import pallas as pl
from jax.experimental.pallas import tpu as pltpu

F32 = jnp.float32
BF16 = jnp.bfloat16

D_MODEL = 2048
SEQ = 8192
DEPTH = 4
MEM_LEN = 256
GLA_HEADS = 4
GLA_DK = 128
GLA_DV = 256
GLA_GATE_RANK = 16
GLA_GATE_NORMALIZER = 16.0
GLA_CHUNK = 64
DIL_HEADS = 4
DIL_HEAD_DIM = 128
DIL_CONFIGS = ((128, 1), (512, 4), (2048, 16))
MEM_HEADS = 4
MEM_HEAD_DIM = 128
REL_BUCKETS = 32
REL_MAX_DISTANCE = 1024
D_FF = 4 * D_MODEL
EPS = 1e-6
NEG_INF = -1e30

GLA_QK_WIDTH = GLA_HEADS * GLA_DK
GLA_V_WIDTH = GLA_HEADS * GLA_DV
DIL_WIDTH = DIL_HEADS * DIL_HEAD_DIM
MEM_WIDTH = MEM_HEADS * MEM_HEAD_DIM
MIX_WIDTH = GLA_V_WIDTH + DIL_WIDTH + MEM_WIDTH

LANES = 128

COL_GQ = 0
COL_GK = COL_GQ + GLA_QK_WIDTH // LANES
COL_GV = COL_GK + GLA_QK_WIDTH // LANES
COL_GR = COL_GV + GLA_V_WIDTH // LANES
COL_DQ = COL_GR + GLA_V_WIDTH // LANES
COL_DK = COL_DQ + DIL_WIDTH // LANES
COL_DV = COL_DK + DIL_WIDTH // LANES
COL_MQ = COL_DV + DIL_WIDTH // LANES
COL_LR = COL_MQ + MEM_WIDTH // LANES
PROJ_COLS = COL_LR + 2
PROJ_WIDTH = PROJ_COLS * LANES

GLA_BLOCK = 256

DIL_BAND = 64
DIL_TQ = 128
DIL_TK = DIL_TQ + 2 * DIL_BAND
DIL_DILATIONS = tuple(d for _, d in DIL_CONFIGS)
DIL_TB = DIL_TQ * max(DIL_DILATIONS)
DIL_INTERLEAVE = 8

VMEM_LIMIT = 48 * 1024 * 1024
BIG_VMEM_LIMIT = 60 * 1024 * 1024


def _params(n_axes, vmem=VMEM_LIMIT):
    return pltpu.CompilerParams(dimension_semantics=("arbitrary",) * n_axes, vmem_limit_bytes=vmem)


def _dot(a, b):
    return jnp.dot(a, b, preferred_element_type=F32)


def _dot_nt(a, b):
    return lax.dot_general(a, b, (((1,), (1,)), ((), ())), preferred_element_type=F32)


def _dot_tn(a, b):
    return lax.dot_general(a, b, (((0,), (0,)), ((), ())), preferred_element_type=F32)


def _rms(x, gain):
    return x * lax.rsqrt(jnp.mean(x * x, axis=-1, keepdims=True) + EPS) * gain


def _run_interleaved(stage_generators):
    live = list(stage_generators)
    while live:
        live = [gen for gen in live if next(gen, StopIteration) is not StopIteration]


def _in_proj_kernel(x_ref, g_ref, w_ref, o_ref, h_ref, *, n_sub):
    j = pl.program_id(1)
    sub = x_ref.shape[0] // n_sub

    @pl.when(j == 0)
    def _():
        for c in range(n_sub):
            rows = slice(c * sub, (c + 1) * sub)
            h = _rms(x_ref[rows, :], g_ref[...]).astype(h_ref.dtype)
            h_ref[rows, :] = h
            o_ref[rows, :] = _dot_nt(h, w_ref[...]).astype(o_ref.dtype)

    @pl.when(j != 0)
    def _():
        for c in range(n_sub):
            rows = slice(c * sub, (c + 1) * sub)
            o_ref[rows, :] = _dot_nt(h_ref[rows, :], w_ref[...]).astype(o_ref.dtype)


def _in_proj(x, gain, w, layer, tm=1024, tn=1792, n_sub=4):
    s, d = x.shape
    n = w.shape[0]
    return pl.pallas_call(
        functools.partial(_in_proj_kernel, n_sub=n_sub),
        grid=(s // tm, n // tn),
        in_specs=[
            pl.BlockSpec((tm, d), lambda i, j: (i, 0)),
            pl.BlockSpec((None, 1, d), lambda i, j: (layer, 0, 0)),
            pl.BlockSpec((tn, d), lambda i, j: (j, 0)),
        ],
        out_specs=pl.BlockSpec((tm, tn), lambda i, j: (i, j)),
        out_shape=jax.ShapeDtypeStruct((s, n), BF16),
        scratch_shapes=[pltpu.VMEM((tm, d), BF16)],
        compiler_params=_params(2),
        name="in_proj",
    )(x, gain, w)


def _log_sigmoid(x):
    return jnp.minimum(x, 0.0) - jnp.log(1.0 + jnp.exp(-jnp.abs(x)))


GLA_LEVELS = (GLA_BLOCK // GLA_CHUNK).bit_length() - 1


def _gla_tables(tri_ref, lvl_ref, reverse):
    t = GLA_BLOCK
    row = lax.broadcasted_iota(jnp.int32, (t, t), 0)
    col = lax.broadcasted_iota(jnp.int32, (t, t), 1)
    valid = (col >= row) if reverse else (col <= row)
    tri_ref[...] = jnp.where(valid, 1.0, 0.0).astype(tri_ref.dtype)
    shift = GLA_CHUNK.bit_length() - 1
    lvl_ref[...] = jnp.where(jnp.right_shift(row, shift) == jnp.right_shift(col, shift),
                             jnp.where(valid, 0, -1), -1)


def _gla_boundary(cum, group, reverse):
    t = cum.shape[0]
    pieces = []
    for a in range(0, t, group):
        idx = a + group if reverse else a - 1
        if idx < 0 or idx >= t:
            ref_row = jnp.zeros((1, cum.shape[1]), cum.dtype)
        else:
            ref_row = cum[idx:idx + 1]
        pieces.append(jnp.broadcast_to(ref_row, (group, cum.shape[1])))
    return jnp.concatenate(pieces, axis=0)


def _place_rows(part, row0, total_rows):
    pieces = []
    if row0 > 0:
        pieces.append(jnp.zeros((row0, part.shape[1]), part.dtype))
    pieces.append(part)
    rest = total_rows - row0 - part.shape[0]
    if rest > 0:
        pieces.append(jnp.zeros((rest, part.shape[1]), part.dtype))
    return jnp.concatenate(pieces, axis=0)


def _gla_block(q_ref, k_ref, v_ref, lr_ref, up_ref, bias_ref, o_ref, s_ref, tri_ref, lvl_ref, r0, reverse):
    t = GLA_BLOCK
    rows = slice(r0, r0 + t)
    logits = _dot(lr_ref[rows, :], up_ref[...]) + bias_ref[...]
    yield
    g = _log_sigmoid(logits) * (1.0 / GLA_GATE_NORMALIZER)
    g_hi = g.astype(BF16)
    g_lo = (g - g_hi.astype(F32)).astype(BF16)
    cum2 = _dot(tri_ref[...], jnp.concatenate([g_hi, g_lo], axis=1))
    yield
    cum = cum2[:, :GLA_DK] + cum2[:, GLA_DK:]

    q = q_ref[rows, :].astype(F32)
    k = k_ref[rows, :].astype(F32)
    v = v_ref[rows, :]
    lvl = lvl_ref[...]

    ref0 = _gla_boundary(cum, GLA_CHUNK, reverse)
    q_dec = (q * jnp.exp(cum - ref0)).astype(BF16)
    k_inv = (k * jnp.exp(ref0 - cum)).astype(BF16)
    q_segs, k_segs = [], []
    for level in range(1, GLA_LEVELS + 1):
        group = GLA_CHUNK << level
        for a0 in range(0, t, group):
            mid = a0 + group // 2
            att = slice(a0, mid) if reverse else slice(mid, a0 + group)
            src = slice(mid, a0 + group) if reverse else slice(a0, mid)
            ref_row = cum[mid:mid + 1] if reverse else cum[mid - 1:mid]
            q_part = (q[att] * jnp.exp(cum[att] - ref_row)).astype(BF16)
            k_part = (k[src] * jnp.exp(ref_row - cum[src])).astype(BF16)
            q_segs.append(_place_rows(q_part, att.start, t))
            k_segs.append(_place_rows(k_part, src.start, t))
    cross = _dot_nt(jnp.concatenate(q_segs, axis=1), jnp.concatenate(k_segs, axis=1))
    diag = _dot_nt(q_dec, k_inv)
    total = cum[0:1] if reverse else cum[t - 1:t]
    q_in = (q * jnp.exp(cum)).astype(BF16)
    k_out = (k * jnp.exp(total - cum)).astype(BF16)
    carry_in = _dot_tn(v, k_out)
    yield
    a = jnp.where(lvl == 0, diag, cross)
    o_local = _dot(a.astype(BF16), v)
    yield
    state = s_ref[...]
    o = o_local + _dot_nt(q_in, state.astype(BF16))
    o_ref[rows, :] = o.astype(o_ref.dtype)
    s_ref[...] = state * jnp.exp(total) + carry_in


def _cast_slab(weights, layer, n_steps, step_of):
    rows, cols = weights.shape[1:]
    slab = rows // n_steps
    return (pl.BlockSpec((None, slab, cols), lambda *g: (layer, step_of(*g), 0)),
            pl.BlockSpec((slab, cols), lambda *g: (step_of(*g), 0)),
            jax.ShapeDtypeStruct((rows, cols), BF16))


def _gla_kernel(qf, kf, vf, lrf, qb, kb, vb, lrb, upf, upb, bsf, bsb, wu_ref, of, ob, wu_out,
                sf, sb, tri_f, tri_b, lvl_f, lvl_b, *, interleave):
    @pl.when(pl.program_id(1) == 0)
    def _():
        sf[...] = jnp.zeros_like(sf)
        sb[...] = jnp.zeros_like(sb)
        _gla_tables(tri_f, lvl_f, reverse=False)
        _gla_tables(tri_b, lvl_b, reverse=True)

    starts = list(range(0, qf.shape[0], GLA_BLOCK))
    n_groups = len(starts) // interleave

    def cast_piece(piece):
        n = wu_ref.shape[0] // n_groups
        wu_out[piece * n:(piece + 1) * n, :] = wu_ref[piece * n:(piece + 1) * n, :].astype(wu_out.dtype)

    for gi in range(n_groups):
        blocks = []
        for r_fwd, r_bwd in list(zip(starts, reversed(starts)))[gi * interleave:(gi + 1) * interleave]:
            blocks.append(_gla_block(qf, kf, vf, lrf, upf, bsf, of, sf, tri_f, lvl_f, r_fwd, reverse=False))
            blocks.append(_gla_block(qb, kb, vb, lrb, upb, bsb, ob, sb, tri_b, lvl_b, r_bwd, reverse=True))
        _run_interleaved(blocks)
        cast_piece(gi)


def _gla(proj, up_f, up_b, bias_f, bias_b, w_up, layer, t=2048, interleave=8):
    s = proj.shape[0]
    nb = s // t
    vcol = COL_GV * LANES // GLA_DV
    wu_in, wu_out, wu_sds = _cast_slab(w_up, layer, GLA_HEADS * nb, lambda h, i: h * nb + i)

    def row_specs(rmap):
        return [
            pl.BlockSpec((t, GLA_DK), lambda h, i: (rmap(i), COL_GQ + h)),
            pl.BlockSpec((t, GLA_DK), lambda h, i: (rmap(i), COL_GK + h)),
            pl.BlockSpec((t, GLA_DV), lambda h, i: (rmap(i), vcol + h)),
            pl.BlockSpec((t, LANES), lambda h, i: (rmap(i), COL_LR)),
        ]

    fwd = lambda i: i
    bwd = lambda i: nb - 1 - i
    up_spec = pl.BlockSpec((None, LANES, GLA_DK), lambda h, i: (layer, 0, h))
    bias_spec = pl.BlockSpec((None, 1, GLA_DK), lambda h, i: (layer, 0, h))
    out_sds = jax.ShapeDtypeStruct((s, GLA_V_WIDTH), BF16)
    state = pltpu.VMEM((GLA_DV, GLA_DK), F32)
    tri = pltpu.VMEM((GLA_BLOCK, GLA_BLOCK), BF16)
    lvl = pltpu.VMEM((GLA_BLOCK, GLA_BLOCK), jnp.int32)
    return pl.pallas_call(
        functools.partial(_gla_kernel, interleave=interleave),
        grid=(GLA_HEADS, nb),
        in_specs=row_specs(fwd) + row_specs(bwd) + [up_spec, up_spec, bias_spec, bias_spec, wu_in],
        out_specs=[
            pl.BlockSpec((t, GLA_DV), lambda h, i: (fwd(i), h)),
            pl.BlockSpec((t, GLA_DV), lambda h, i: (bwd(i), h)),
            wu_out,
        ],
        out_shape=[out_sds, out_sds, wu_sds],
        scratch_shapes=[state, state, tri, tri, lvl, lvl],
        compiler_params=_params(2),
        name="gla_scan",
    )(proj, proj, proj, proj, proj, proj, proj, proj, up_f, up_b, bias_f, bias_b, w_up)


def _dil_tile(q, k, v, bias, edge, o_dst, l_dst, rows):
    raw = _dot_nt(q, k)
    yield
    sc = raw + bias
    if edge is not None:
        sc = sc + edge
    m = jnp.max(sc, axis=-1, keepdims=True)
    p = jnp.exp(sc - m).astype(BF16)
    pv = _dot(p, jnp.concatenate([v, jnp.ones_like(v)], axis=1))
    yield
    den = pv[:, DIL_HEAD_DIM:]
    o_dst[rows, :] = pv[:, :DIL_HEAD_DIM] / den
    l_dst[rows, :] = m + jnp.log(den)


def _dil_kernel(q_ref, k_ref, v_ref, bias_ref, gain_ref, wo_ref, wd_ref, out_ref, wo_out, wd_out,
                stage, stage_mid, kg1, vg1, kg4, vg4, kg16, vg16, qp4, qp16, o1, o4, o16, l1, l4, l16):
    i = pl.program_id(1)
    s = k_ref.shape[0]
    tb = q_ref.shape[0]
    k_bufs = dict(zip(DIL_DILATIONS, (kg1, kg4, kg16)))
    v_bufs = dict(zip(DIL_DILATIONS, (vg1, vg4, vg16)))
    q_perm = dict(zip(DIL_DILATIONS, (None, qp4, qp16)))
    o_bufs = dict(zip(DIL_DILATIONS, (o1, o4, o16)))
    l_bufs = dict(zip(DIL_DILATIONS, (l1, l4, l16)))
    _, d_mid, d_max = DIL_DILATIONS
    assert d_max == d_mid * d_mid

    def deinterleave(dst_mid, dst_max, base_mid, base_max, residue_rows):
        per_mid, per_max = tb // d_mid, tb // d_max

        def at(base, offset):
            start = base + offset
            return start if isinstance(start, int) else pl.multiple_of(start, DIL_BAND)

        for r in range(d_mid):
            part = stage[pl.ds(r, per_mid, stride=d_mid), :]
            stage_mid[r * per_mid:(r + 1) * per_mid, :] = part
            dst_mid[pl.ds(at(base_mid, r * (residue_rows // d_mid)), per_mid), :] = part.astype(dst_mid.dtype)
        for r in range(d_max):
            r_mid, m = r % d_mid, r // d_mid
            part = stage_mid[pl.ds(r_mid * per_mid + m, per_max, stride=d_mid), :]
            dst_max[pl.ds(at(base_max, r * (residue_rows // d_max)), per_max), :] = part.astype(dst_max.dtype)

    @pl.when(i == 0)
    def _():
        zeros = jnp.zeros((DIL_BAND, DIL_HEAD_DIM), BF16)
        for buf in (kg1, vg1, kg4, vg4, kg16, vg16):
            buf[0:DIL_BAND, :] = zeros
            buf[DIL_BAND + s:, :] = zeros
        kg1[DIL_BAND:DIL_BAND + s, :] = k_ref[...]
        vg1[DIL_BAND:DIL_BAND + s, :] = v_ref[...]

        def body(c, carry):
            base = pl.multiple_of(c * tb, tb)
            for src, bufs in ((k_ref, k_bufs), (v_ref, v_bufs)):
                stage[...] = src[pl.ds(base, tb), :].astype(F32)
                deinterleave(bufs[d_mid], bufs[d_max], DIL_BAND + c * (tb // d_mid),
                             DIL_BAND + c * (tb // d_max), s)
            return carry

        lax.fori_loop(0, s // tb, body, 0)

    stage[...] = q_ref[...].astype(F32)
    deinterleave(q_perm[d_mid], q_perm[d_max], 0, 0, tb)

    lane = lax.broadcasted_iota(jnp.int32, (1, DIL_TK), 1)
    tiles = []
    for b, d in enumerate(DIL_DILATIONS):
        sub_len = s // d
        per = tb // d
        n_tiles = per // DIL_TQ
        for r in range(d):
            for tj in range(n_tiles):
                tiles.append((b, d, sub_len, per, n_tiles, r, tj))

    def tile_stages(b, d, sub_len, per, n_tiles, r, tj):
        u0 = i * per + tj * DIL_TQ
        q_src = q_ref if d == 1 else q_perm[d]
        q = q_src[r * per + tj * DIL_TQ:r * per + (tj + 1) * DIL_TQ, :]
        k_start = pl.multiple_of(r * sub_len + u0, DIL_TQ)
        k = k_bufs[d][pl.ds(k_start, DIL_TK), :]
        v = v_bufs[d][pl.ds(k_start, DIL_TK), :]
        edge = None
        if tj == 0 or tj == n_tiles - 1:
            key_pos = u0 - DIL_BAND + lane
            edge = jnp.where((key_pos >= 0) & (key_pos < sub_len), 0.0, NEG_INF)
        if d == 1:
            rows = pl.ds(tj * DIL_TQ, DIL_TQ)
        else:
            rows = pl.ds(r + d * tj * DIL_TQ, DIL_TQ, stride=d)
        return _dil_tile(q, k, v, bias_ref[b], edge, o_bufs[d], l_bufs[d], rows)

    for g0 in range(0, len(tiles), DIL_INTERLEAVE):
        _run_interleaved([tile_stages(*tile) for tile in tiles[g0:g0 + DIL_INTERLEAVE]])

    lses = [l_bufs[d][...] for d in DIL_DILATIONS]
    m = functools.reduce(jnp.maximum, lses)
    es = [jnp.exp(l - m) for l in lses]
    num = sum(e * o_bufs[d][...] for e, d in zip(es, DIL_DILATIONS))
    mixed = num / sum(es)
    out_ref[...] = _head_rms(mixed, gain_ref[...]).astype(out_ref.dtype)

    wo_out[...] = wo_ref[...].astype(wo_out.dtype)
    wd_out[...] = wd_ref[...].astype(wd_out.dtype)


def _dil_attention(proj, bias, gain, w_out, w_down, layer):
    s = proj.shape[0]
    tb = DIL_TB
    nq = s // tb
    step_of = lambda h, i: h * nq + i
    wo_in, wo_out, wo_sds = _cast_slab(w_out, layer, DIL_HEADS * nq, step_of)
    wd_in, wd_out, wd_sds = _cast_slab(w_down, layer, DIL_HEADS * nq, step_of)
    guarded = pltpu.VMEM((s + 2 * DIL_BAND, DIL_HEAD_DIM), BF16)
    tile_f32 = pltpu.VMEM((tb, DIL_HEAD_DIM), F32)
    tile_bf16 = pltpu.VMEM((tb, DIL_HEAD_DIM), BF16)
    return pl.pallas_call(
        _dil_kernel,
        grid=(DIL_HEADS, nq),
        in_specs=[
            pl.BlockSpec((tb, DIL_HEAD_DIM), lambda h, i: (i, COL_DQ + h)),
            pl.BlockSpec((s, DIL_HEAD_DIM), lambda h, i: (0, COL_DK + h)),
            pl.BlockSpec((s, DIL_HEAD_DIM), lambda h, i: (0, COL_DV + h)),
            pl.BlockSpec((len(DIL_DILATIONS), None, DIL_TQ, DIL_TK), lambda h, i: (0, h, 0, 0)),
            pl.BlockSpec((None, 1, DIL_HEAD_DIM), lambda h, i: (layer, 0, h)),
            wo_in, wd_in,
        ],
        out_specs=[pl.BlockSpec((tb, DIL_HEAD_DIM), lambda h, i: (i, h)), wo_out, wd_out],
        out_shape=[jax.ShapeDtypeStruct((s, DIL_WIDTH), BF16), wo_sds, wd_sds],
        scratch_shapes=[tile_f32] * 2 + [guarded] * 6 + [tile_bf16] * 2 + [tile_f32] * 6,
        compiler_params=_params(2, BIG_VMEM_LIMIT),
        name="dil_attn",
    )(proj, proj, proj, bias, gain, w_out, w_down)


def _t5_bucket(rel):
    half = REL_BUCKETS // 2
    max_exact = half // 2
    ret = jnp.where(rel > 0, half, 0)
    n = jnp.abs(rel)
    nf = jnp.maximum(n, 1).astype(F32)
    large = max_exact + (jnp.log(nf / max_exact) / math.log(REL_MAX_DISTANCE / max_exact)
                         * (half - max_exact)).astype(jnp.int32)
    large = jnp.minimum(large, half - 1)
    return ret + jnp.where(n < max_exact, n, large)


def _dil_bias_tiles(rel_bias):
    rel = jnp.arange(DIL_TK)[None, :] - DIL_BAND - jnp.arange(DIL_TQ)[:, None]
    band = jnp.abs(rel) <= DIL_BAND
    tiles = []
    for d in DIL_DILATIONS:
        onehot = (_t5_bucket(rel * d)[..., None] == jnp.arange(REL_BUCKETS)).astype(F32)
        bias = jnp.einsum("qkb,bh->hqk", onehot, rel_bias.astype(F32), precision=lax.Precision.HIGHEST)
        tiles.append(jnp.where(band[None], bias, NEG_INF))
    return jnp.stack(tiles, axis=0)


def _mem_kv_kernel(mem_ref, g_ref, w_ref, o_ref):
    h = _rms(mem_ref[...], g_ref[...]).astype(BF16)
    o_ref[...] = _dot(h, w_ref[...].astype(BF16)).astype(o_ref.dtype)


def _mem_kv(mem, gain, w):
    m, d = mem.shape
    n = w.shape[-1]
    return pl.pallas_call(
        _mem_kv_kernel,
        grid=(DEPTH,),
        in_specs=[
            pl.BlockSpec((m, d), lambda l: (0, 0)),
            pl.BlockSpec((None, 1, d), lambda l: (l, 0, 0)),
            pl.BlockSpec((None, d, n), lambda l: (l, 0, 0)),
        ],
        out_specs=pl.BlockSpec((None, m, n), lambda l: (l, 0, 0)),
        out_shape=jax.ShapeDtypeStruct((DEPTH, m, n), BF16),
        compiler_params=_params(1),
        name="mem_kv",
    )(mem, gain, w)


def _head_rms(o, gain):
    return o * lax.rsqrt(jnp.mean(o * o, axis=-1, keepdims=True) + EPS) * gain


def _mix_kernel(of_ref, ob_ref, gr_ref, dil_ref, mq_ref, km_ref, vm_ref, ggla_ref, gmem_ref, w_ref, x_ref,
                *rest, prep_next):
    if prep_next:
        w_in_ref, out_ref, w_in_out, gla_ref, mem_ref = rest
        _w_in_prep_kernel(w_in_ref, w_in_out)
    else:
        out_ref, gla_ref, mem_ref = rest

    dil0 = GLA_V_WIDTH
    mem0 = GLA_V_WIDTH + DIL_WIDTH

    def mem_head(h):
        sl = slice(h * MEM_HEAD_DIM, (h + 1) * MEM_HEAD_DIM)
        sc = _dot_nt(mq_ref[:, sl], km_ref[:, sl])
        yield
        m = jnp.max(sc, axis=-1, keepdims=True)
        p = jnp.exp(sc - m)
        den = jnp.sum(p, axis=-1, keepdims=True)
        pv = _dot(p.astype(BF16), vm_ref[:, sl])
        yield
        mem_ref[:, sl] = _head_rms(pv / den, gmem_ref[:, sl]).astype(mem_ref.dtype)

    def gla_heads():
        for h in range(GLA_HEADS):
            sl = slice(h * GLA_DV, (h + 1) * GLA_DV)
            o = of_ref[:, sl].astype(F32) + ob_ref[:, sl].astype(F32)
            r = gr_ref[:, sl].astype(F32)
            gate = r * (1.0 / (1.0 + jnp.exp(-r)))
            gla_ref[:, sl] = (_head_rms(o, ggla_ref[:, sl]) * gate).astype(gla_ref.dtype)
        yield
        out_ref[...] += _dot(gla_ref[...], w_ref[0:dil0, :])

    def dil_group():
        out_ref[...] = x_ref[...] + _dot(dil_ref[...], w_ref[dil0:mem0, :])
        yield

    _run_interleaved([mem_head(h) for h in range(MEM_HEADS)] + [dil_group(), gla_heads()])
    out_ref[...] += _dot(mem_ref[...], w_ref[mem0:, :])


def _mix_out(x, proj, o_f, o_b, dil, kv, g_gla, g_mem, w_out, w_in_t, layer, t=512):
    s, d = x.shape
    n_steps = s // t
    prep_next = layer + 1 < w_in_t.shape[0]
    rows = lambda w, c=0: pl.BlockSpec((t, w), lambda i: (i, c))
    per_layer = lambda shape, c=0: pl.BlockSpec((None,) + shape, lambda i: (layer, 0, c),
                                                pipeline_mode=pl.Buffered(1))
    in_specs = [
        rows(GLA_V_WIDTH), rows(GLA_V_WIDTH),
        rows(GLA_V_WIDTH, COL_GR * LANES // GLA_V_WIDTH),
        rows(DIL_WIDTH),
        rows(MEM_WIDTH, COL_MQ * LANES // MEM_WIDTH),
        per_layer((MEM_LEN, MEM_WIDTH), 0), per_layer((MEM_LEN, MEM_WIDTH), 1),
        per_layer((1, GLA_V_WIDTH)), per_layer((1, MEM_WIDTH)),
        pl.BlockSpec((MIX_WIDTH, d), lambda i: (0, 0), pipeline_mode=pl.Buffered(1)),
        rows(d),
    ]
    operands = [o_f, o_b, proj, dil, proj, kv, kv, g_gla, g_mem, w_out, x]
    out_specs = [rows(d)]
    out_shape = [jax.ShapeDtypeStruct((s, d), F32)]
    if prep_next:
        n_in = w_in_t.shape[1]
        in_specs.append(pl.BlockSpec((None, n_in, d // n_steps), lambda i: (layer + 1, 0, i)))
        operands.append(w_in_t)
        out_specs.append(pl.BlockSpec((PROJ_WIDTH, d // n_steps), lambda i: (0, i)))
        out_shape.append(jax.ShapeDtypeStruct((PROJ_WIDTH, d), BF16))
    outs = pl.pallas_call(
        functools.partial(_mix_kernel, prep_next=prep_next),
        grid=(n_steps,),
        in_specs=in_specs,
        out_specs=out_specs,
        out_shape=out_shape,
        scratch_shapes=[pltpu.VMEM((t, GLA_V_WIDTH), BF16), pltpu.VMEM((t, MEM_WIDTH), BF16)],
        compiler_params=_params(1),
        name="mix_out_proj",
    )(*operands)
    return (outs[0], outs[1]) if prep_next else (outs[0], None)


def _mlp_kernel(x_ref, g_ref, wu_ref, wd_ref, gf_ref, o_ref, h_ref, *, final_norm, n_sub):
    f = pl.program_id(1)
    sub = x_ref.shape[0] // n_sub

    def hidden(h):
        return jnp.square(jnp.maximum(_dot(h, wu_ref[...]), 0.0)).astype(BF16)

    @pl.when(f == 0)
    def _():
        for c in range(n_sub):
            rows = slice(c * sub, (c + 1) * sub)
            x = x_ref[rows, :]
            h = _rms(x, g_ref[...]).astype(h_ref.dtype)
            h_ref[rows, :] = h
            o_ref[rows, :] = x + _dot(hidden(h), wd_ref[...])

    last = pl.num_programs(1) - 1
    middle = (f != 0) & (f != last) if final_norm else f != 0

    @pl.when(middle)
    def _():
        for c in range(n_sub):
            rows = slice(c * sub, (c + 1) * sub)
            o_ref[rows, :] += _dot(hidden(h_ref[rows, :]), wd_ref[...])

    if final_norm:
        @pl.when(f == last)
        def _():
            for c in range(n_sub):
                rows = slice(c * sub, (c + 1) * sub)
                acc = o_ref[rows, :] + _dot(hidden(h_ref[rows, :]), wd_ref[...])
                o_ref[rows, :] = _rms(acc, gf_ref[...])


def _mlp(x, gain, w_up, w_down, gain_final, layer, final_norm, tm=1024, tf=1024, n_sub=4):
    s, d = x.shape
    ff = w_up.shape[-1]
    return pl.pallas_call(
        functools.partial(_mlp_kernel, final_norm=final_norm, n_sub=n_sub),
        grid=(s // tm, ff // tf),
        in_specs=[
            pl.BlockSpec((tm, d), lambda i, f: (i, 0)),
            pl.BlockSpec((None, 1, d), lambda i, f: (layer, 0, 0)),
            pl.BlockSpec((d, tf), lambda i, f: (0, f)),
            pl.BlockSpec((tf, d), lambda i, f: (f, 0)),
            pl.BlockSpec((1, d), lambda i, f: (0, 0)),
        ],
        out_specs=pl.BlockSpec((tm, d), lambda i, f: (i, 0)),
        out_shape=jax.ShapeDtypeStruct((s, d), F32),
        scratch_shapes=[pltpu.VMEM((tm, d), BF16)],
        compiler_params=_params(2, BIG_VMEM_LIMIT),
        name="mlp",
    )(x, gain, w_up, w_down, gain_final)


def _w_in_prep_kernel(w_ref, o_ref):
    lr0 = 2 * GLA_QK_WIDTH + 2 * GLA_V_WIDTH
    lr_w = 2 * GLA_GATE_RANK
    shift = lr_w

    def copy(src0, rows, dst0, scale=None):
        block = w_ref[src0:src0 + rows, :]
        if scale is not None:
            block = block * scale
        o_ref[dst0:dst0 + rows, :] = block.astype(o_ref.dtype)

    copy(0, GLA_QK_WIDTH, 0, GLA_DK ** -0.5)
    copy(GLA_QK_WIDTH, lr0 - GLA_QK_WIDTH, GLA_QK_WIDTH)
    dq0 = lr0 + lr_w
    copy(dq0, DIL_WIDTH, dq0 - shift, DIL_HEAD_DIM ** -0.5)
    copy(dq0 + DIL_WIDTH, 2 * DIL_WIDTH, dq0 + DIL_WIDTH - shift)
    mq0 = dq0 + 3 * DIL_WIDTH
    copy(mq0, MEM_WIDTH, mq0 - shift, MEM_HEAD_DIM ** -0.5)
    end = mq0 + MEM_WIDTH
    assert end == w_ref.shape[0] and COL_LR * LANES == end - shift
    copy(lr0, lr_w, end - shift)
    o_ref[end:, :] = jnp.zeros((o_ref.shape[0] - end, o_ref.shape[1]), o_ref.dtype)


def _reorder_w_in(w_in_t, layer, cols=256):
    _, n, d = w_in_t.shape
    assert COL_LR * LANES == n - 2 * GLA_GATE_RANK
    return pl.pallas_call(
        _w_in_prep_kernel,
        grid=(d // cols,),
        in_specs=[pl.BlockSpec((None, n, cols), lambda i: (layer, 0, i))],
        out_specs=pl.BlockSpec((PROJ_WIDTH, cols), lambda i: (0, i)),
        out_shape=jax.ShapeDtypeStruct((PROJ_WIDTH, d), BF16),
        compiler_params=_params(1),
        name="w_in_prep",
    )(w_in_t)


def _pad_gate_up(up, row0):
    out = jnp.zeros((up.shape[0], LANES, up.shape[-1]), BF16)
    return out.at[:, row0:row0 + GLA_GATE_RANK, :].set(up.astype(BF16))


def kernel(x, mem, norm_mix, w_in, gla_gate_up_fwd, gla_gate_bias_fwd, gla_gate_up_bwd, gla_gate_bias_bwd, gla_norm, rel_bias, dil_norm, mem_norm, w_mem_kv, mem_out_norm, w_out, norm_mlp, w_up, w_down, norm_final):
    batch, seq, d = x.shape
    assert batch == 1 and seq == SEQ and d == D_MODEL
    xs = x.reshape(seq, d)
    row = lambda g: g.reshape(g.shape[0], 1, g.shape[-1])

    w_in_t = jnp.swapaxes(w_in, 1, 2)
    w_in_l = _reorder_w_in(w_in_t, 0)
    up_f = _pad_gate_up(gla_gate_up_fwd, 0)
    up_b = _pad_gate_up(gla_gate_up_bwd, GLA_GATE_RANK)
    dil_bias = _dil_bias_tiles(rel_bias)
    kv = _mem_kv(mem.reshape(MEM_LEN, d), row(mem_norm), w_mem_kv)

    for layer in range(DEPTH):
        proj = _in_proj(xs, row(norm_mix), w_in_l, layer)
        o_f, o_b, w_up_b = _gla(proj, up_f, up_b, row(gla_gate_bias_fwd), row(gla_gate_bias_bwd), w_up, layer)
        dil, w_out_b, w_down_b = _dil_attention(proj, dil_bias, row(dil_norm), w_out, w_down, layer)
        xs, w_in_l = _mix_out(xs, proj, o_f, o_b, dil, kv, row(gla_norm), row(mem_out_norm), w_out_b,
                              w_in_t, layer)
        xs = _mlp(xs, row(norm_mlp), w_up_b, w_down_b, norm_final.reshape(1, d), layer,
                  final_norm=(layer == DEPTH - 1))
    return xs.reshape(batch, seq, d)
```

```python
import functools
import math

import jax
import jax.numpy as jnp
from jax import lax
from jax.experimental import pallas as pl
from jax.experimental.pallas import tpu as pltpu

F32 = jnp.float32
BF16 = jnp.bfloat16

D_MODEL = 2048
SEQ = 8192
DEPTH = 4
MEM_LEN = 256
GLA_HEADS = 4
GLA_DK = 128
GLA_DV = 256
GLA_GATE_RANK = 16
GLA_GATE_NORMALIZER = 16.0
GLA_CHUNK = 64
DIL_HEADS = 4
DIL_HEAD_DIM = 128
DIL_CONFIGS = ((128, 1), (512, 4), (2048, 16))
MEM_HEADS = 4
MEM_HEAD_DIM = 128
REL_BUCKETS = 32
REL_MAX_DISTANCE = 1024
D_FF = 4 * D_MODEL
EPS = 1e-6
NEG_INF = -1e30

GLA_QK_WIDTH = GLA_HEADS * GLA_DK
GLA_V_WIDTH = GLA_HEADS * GLA_DV
DIL_WIDTH = DIL_HEADS * DIL_HEAD_DIM
MEM_WIDTH = MEM_HEADS * MEM_HEAD_DIM
MIX_WIDTH = GLA_V_WIDTH + DIL_WIDTH + MEM_WIDTH

LANES = 128

COL_GQ = 0
COL_GK = COL_GQ + GLA_QK_WIDTH // LANES
COL_GV = COL_GK + GLA_QK_WIDTH // LANES
COL_GR = COL_GV + GLA_V_WIDTH // LANES
COL_DQ = COL_GR + GLA_V_WIDTH // LANES
COL_DK = COL_DQ + DIL_WIDTH // LANES
COL_DV = COL_DK + DIL_WIDTH // LANES
COL_MQ = COL_DV + DIL_WIDTH // LANES
COL_LR = COL_MQ + MEM_WIDTH // LANES
PROJ_COLS = COL_LR + 2
PROJ_WIDTH = PROJ_COLS * LANES

GLA_BLOCK = 256

DIL_BAND = 64
DIL_TQ = 128
DIL_TK = DIL_TQ + 2 * DIL_BAND
DIL_DILATIONS = tuple(d for _, d in DIL_CONFIGS)
DIL_TB = DIL_TQ * max(DIL_DILATIONS)
DIL_INTERLEAVE = 8

VMEM_LIMIT = 48 * 1024 * 1024
BIG_VMEM_LIMIT = 60 * 1024 * 1024


def _params(n_axes, vmem=VMEM_LIMIT):
    return pltpu.CompilerParams(dimension_semantics=("arbitrary",) * n_axes, vmem_limit_bytes=vmem)


def _dot(a, b):
    return jnp.dot(a, b, preferred_element_type=F32)


def _dot_nt(a, b):
    return lax.dot_general(a, b, (((1,), (1,)), ((), ())), preferred_element_type=F32)


def _dot_tn(a, b):
    return lax.dot_general(a, b, (((0,), (0,)), ((), ())), preferred_element_type=F32)


def _rms(x, gain):
    return x * lax.rsqrt(jnp.mean(x * x, axis=-1, keepdims=True) + EPS) * gain


def _run_interleaved(stage_generators):
    live = list(stage_generators)
    while live:
        live = [gen for gen in live if next(gen, StopIteration) is not StopIteration]


def _in_proj_kernel(x_ref, g_ref, w_ref, o_ref, h_ref, *, n_sub):
    j = pl.program_id(1)
    sub = x_ref.shape[0] // n_sub

    @pl.when(j == 0)
    def _():
        for c in range(n_sub):
            rows = slice(c * sub, (c + 1) * sub)
            h = _rms(x_ref[rows, :], g_ref[...]).astype(h_ref.dtype)
            h_ref[rows, :] = h
            o_ref[rows, :] = _dot_nt(h, w_ref[...]).astype(o_ref.dtype)

    @pl.when(j != 0)
    def _():
        for c in range(n_sub):
            rows = slice(c * sub, (c + 1) * sub)
            o_ref[rows, :] = _dot_nt(h_ref[rows, :], w_ref[...]).astype(o_ref.dtype)


def _in_proj(x, gain, w, layer, tm=1024, tn=1792, n_sub=4):
    s, d = x.shape
    n = w.shape[0]
    return pl.pallas_call(
        functools.partial(_in_proj_kernel, n_sub=n_sub),
        grid=(s // tm, n // tn),
        in_specs=[
            pl.BlockSpec((tm, d), lambda i, j: (i, 0)),
            pl.BlockSpec((None, 1, d), lambda i, j: (layer, 0, 0)),
            pl.BlockSpec((tn, d), lambda i, j: (j, 0)),
        ],
        out_specs=pl.BlockSpec((tm, tn), lambda i, j: (i, j)),
        out_shape=jax.ShapeDtypeStruct((s, n), BF16),
        scratch_shapes=[pltpu.VMEM((tm, d), BF16)],
        compiler_params=_params(2),
        name="in_proj",
    )(x, gain, w)


def _log_sigmoid(x):
    return jnp.minimum(x, 0.0) - jnp.log(1.0 + jnp.exp(-jnp.abs(x)))


GLA_LEVELS = (GLA_BLOCK // GLA_CHUNK).bit_length() - 1


def _gla_tables(tri_ref, lvl_ref, reverse):
    t = GLA_BLOCK
    row = lax.broadcasted_iota(jnp.int32, (t, t), 0)
    col = lax.broadcasted_iota(jnp.int32, (t, t), 1)
    valid = (col >= row) if reverse else (col <= row)
    tri_ref[...] = jnp.where(valid, 1.0, 0.0).astype(tri_ref.dtype)
    shift = GLA_CHUNK.bit_length() - 1
    lvl_ref[...] = jnp.where(jnp.right_shift(row, shift) == jnp.right_shift(col, shift),
                             jnp.where(valid, 0, -1), -1)


def _gla_boundary(cum, group, reverse):
    t = cum.shape[0]
    pieces = []
    for a in range(0, t, group):
        idx = a + group if reverse else a - 1
        if idx < 0 or idx >= t:
            ref_row = jnp.zeros((1, cum.shape[1]), cum.dtype)
        else:
            ref_row = cum[idx:idx + 1]
        pieces.append(jnp.broadcast_to(ref_row, (group, cum.shape[1])))
    return jnp.concatenate(pieces, axis=0)


def _place_rows(part, row0, total_rows):
    pieces = []
    if row0 > 0:
        pieces.append(jnp.zeros((row0, part.shape[1]), part.dtype))
    pieces.append(part)
    rest = total_rows - row0 - part.shape[0]
    if rest > 0:
        pieces.append(jnp.zeros((rest, part.shape[1]), part.dtype))
    return jnp.concatenate(pieces, axis=0)


def _gla_block(q_ref, k_ref, v_ref, lr_ref, up_ref, bias_ref, o_ref, s_ref, tri_ref, lvl_ref, r0, reverse):
    t = GLA_BLOCK
    rows = slice(r0, r0 + t)
    logits = _dot(lr_ref[rows, :], up_ref[...]) + bias_ref[...]
    yield
    g = _log_sigmoid(logits) * (1.0 / GLA_GATE_NORMALIZER)
    g_hi = g.astype(BF16)
    g_lo = (g - g_hi.astype(F32)).astype(BF16)
    cum2 = _dot(tri_ref[...], jnp.concatenate([g_hi, g_lo], axis=1))
    yield
    cum = cum2[:, :GLA_DK] + cum2[:, GLA_DK:]

    q = q_ref[rows, :].astype(F32)
    k = k_ref[rows, :].astype(F32)
    ref0 = _gla_boundary(cum, GLA_CHUNK, reverse)
    q_dec = (q * jnp.exp(cum - ref0)).astype(BF16)
    k_inv = (k * jnp.exp(ref0 - cum)).astype(BF16)
    q_segs, k_segs = [], []
    for level in range(1, GLA_LEVELS + 1):
        group = GLA_CHUNK << level
        for a0 in range(0, t, group):
            mid = a0 + group // 2
            att = slice(a0, mid) if reverse else slice(mid, a0 + group)
            src = slice(mid, a0 + group) if reverse else slice(a0, mid)
            ref_row = cum[mid:mid + 1] if reverse else cum[mid - 1:mid]
            q_part = (q[att] * jnp.exp(cum[att] - ref_row)).astype(BF16)
            k_part = (k[src] * jnp.exp(ref_row - cum[src])).astype(BF16)
            q_segs.append(_place_rows(q_part, att.start, t))
            k_segs.append(_place_rows(k_part, src.start, t))
    cross = _dot_nt(jnp.concatenate(q_segs, axis=1), jnp.concatenate(k_segs, axis=1))
    diag = _dot_nt(q_dec, k_inv)
    total = cum[0:1] if reverse else cum[t - 1:t]
    q_in = (q * jnp.exp(cum)).astype(BF16)
    k_out = (k * jnp.exp(total - cum)).astype(BF16)
    carry_in = _dot_tn(v_ref[rows, :], k_out)
    yield
    a = jnp.where(lvl_ref[...] == 0, diag, cross)
    o_local = _dot(a.astype(BF16), v_ref[rows, :])
    yield
    state = s_ref[...]
    o = o_local + _dot_nt(q_in, state.astype(BF16))
    o_ref[rows, :] = o.astype(o_ref.dtype)
    s_ref[...] = state * jnp.exp(total) + carry_in


def _cast_slab(weights, layer, n_steps, step_of):
    rows, cols = weights.shape[1:]
    slab = rows // n_steps
    return (pl.BlockSpec((None, slab, cols), lambda *g: (layer, step_of(*g), 0)),
            pl.BlockSpec((slab, cols), lambda *g: (step_of(*g), 0)),
            jax.ShapeDtypeStruct((rows, cols), BF16))


def _gla_kernel(qf, kf, vf, lrf, qb, kb, vb, lrb, upf, upb, bsf, bsb, wu_ref, of, ob, wu_out,
                sf, sb, tri_f, tri_b, lvl_f, lvl_b, *, interleave):
    @pl.when(pl.program_id(1) == 0)
    def _():
        sf[...] = jnp.zeros_like(sf)
        sb[...] = jnp.zeros_like(sb)
        _gla_tables(tri_f, lvl_f, reverse=False)
        _gla_tables(tri_b, lvl_b, reverse=True)

    starts = list(range(0, qf.shape[0], GLA_BLOCK))
    n_groups = len(starts) // interleave

    def cast_piece(piece):
        n = wu_ref.shape[0] // n_groups
        wu_out[piece * n:(piece + 1) * n, :] = wu_ref[piece * n:(piece + 1) * n, :].astype(wu_out.dtype)

    for gi in range(n_groups):
        blocks = []
        for r_fwd, r_bwd in list(zip(starts, reversed(starts)))[gi * interleave:(gi + 1) * interleave]:
            blocks.append(_gla_block(qf, kf, vf, lrf, upf, bsf, of, sf, tri_f, lvl_f, r_fwd, reverse=False))
            blocks.append(_gla_block(qb, kb, vb, lrb, upb, bsb, ob, sb, tri_b, lvl_b, r_bwd, reverse=True))
        _run_interleaved(blocks)
        cast_piece(gi)


def _gla(proj, up_f, up_b, bias_f, bias_b, w_up, layer, t=2048, interleave=8):
    s = proj.shape[0]
    nb = s // t
    vcol = COL_GV * LANES // GLA_DV
    wu_in, wu_out, wu_sds = _cast_slab(w_up, layer, GLA_HEADS * nb, lambda h, i: h * nb + i)

    def row_specs(rmap):
        return [
            pl.BlockSpec((t, GLA_DK), lambda h, i: (rmap(i), COL_GQ + h)),
            pl.BlockSpec((t, GLA_DK), lambda h, i: (rmap(i), COL_GK + h)),
            pl.BlockSpec((t, GLA_DV), lambda h, i: (rmap(i), vcol + h)),
            pl.BlockSpec((t, LANES), lambda h, i: (rmap(i), COL_LR)),
        ]

    fwd = lambda i: i
    bwd = lambda i: nb - 1 - i
    up_spec = pl.BlockSpec((None, LANES, GLA_DK), lambda h, i: (layer, 0, h))
    bias_spec = pl.BlockSpec((None, 1, GLA_DK), lambda h, i: (layer, 0, h))
    out_sds = jax.ShapeDtypeStruct((s, GLA_V_WIDTH), BF16)
    state = pltpu.VMEM((GLA_DV, GLA_DK), F32)
    tri = pltpu.VMEM((GLA_BLOCK, GLA_BLOCK), BF16)
    lvl = pltpu.VMEM((GLA_BLOCK, GLA_BLOCK), jnp.int32)
    return pl.pallas_call(
        functools.partial(_gla_kernel, interleave=interleave),
        grid=(GLA_HEADS, nb),
        in_specs=row_specs(fwd) + row_specs(bwd) + [up_spec, up_spec, bias_spec, bias_spec, wu_in],
        out_specs=[
            pl.BlockSpec((t, GLA_DV), lambda h, i: (fwd(i), h)),
            pl.BlockSpec((t, GLA_DV), lambda h, i: (bwd(i), h)),
            wu_out,
        ],
        out_shape=[out_sds, out_sds, wu_sds],
        scratch_shapes=[state, state, tri, tri, lvl, lvl],
        compiler_params=_params(2),
        name="gla_scan",
    )(proj, proj, proj, proj, proj, proj, proj, proj, up_f, up_b, bias_f, bias_b, w_up)


def _dil_tile(load_q, load_k, load_v, load_bias, o_dst, l_dst, rows):
    raw = _dot_nt(load_q(), load_k())
    yield
    sc = raw + load_bias()
    m = jnp.max(sc, axis=-1, keepdims=True)
    p = jnp.exp(sc - m).astype(BF16)
    v = load_v()
    pv = _dot(p, jnp.concatenate([v, jnp.ones_like(v)], axis=1))
    yield
    den = pv[:, DIL_HEAD_DIM:]
    o_dst[rows, :] = pv[:, :DIL_HEAD_DIM] / den
    l_dst[rows, :] = m + jnp.log(den)


def _dil_kernel(q_ref, k_ref, v_ref, bias_ref, gain_ref, wo_ref, wd_ref, out_ref, wo_out, wd_out,
                stage, stage_mid, kg1, vg1, kg4, vg4, kg16, vg16, qp4, qp16, o1, o4, o16, l1, l4, l16):
    i = pl.program_id(1)
    s = k_ref.shape[0]
    tb = q_ref.shape[0]
    k_bufs = dict(zip(DIL_DILATIONS, (kg1, kg4, kg16)))
    v_bufs = dict(zip(DIL_DILATIONS, (vg1, vg4, vg16)))
    q_perm = dict(zip(DIL_DILATIONS, (None, qp4, qp16)))
    o_bufs = dict(zip(DIL_DILATIONS, (o1, o4, o16)))
    l_bufs = dict(zip(DIL_DILATIONS, (l1, l4, l16)))
    _, d_mid, d_max = DIL_DILATIONS
    assert d_max == d_mid * d_mid

    def deinterleave(dst_mid, dst_max, base_mid, base_max, residue_rows):
        per_mid, per_max = tb // d_mid, tb // d_max

        def at(base, offset):
            start = base + offset
            return start if isinstance(start, int) else pl.multiple_of(start, DIL_BAND)

        for r in range(d_mid):
            part = stage[pl.ds(r, per_mid, stride=d_mid), :]
            stage_mid[r * per_mid:(r + 1) * per_mid, :] = part
            dst_mid[pl.ds(at(base_mid, r * (residue_rows // d_mid)), per_mid), :] = part.astype(dst_mid.dtype)
        for r in range(d_max):
            r_mid, m = r % d_mid, r // d_mid
            part = stage_mid[pl.ds(r_mid * per_mid + m, per_max, stride=d_mid), :]
            dst_max[pl.ds(at(base_max, r * (residue_rows // d_max)), per_max), :] = part.astype(dst_max.dtype)

    @pl.when(i == 0)
    def _():
        zeros = jnp.zeros((DIL_BAND, DIL_HEAD_DIM), BF16)
        for buf in (kg1, vg1, kg4, vg4, kg16, vg16):
            buf[0:DIL_BAND, :] = zeros
            buf[DIL_BAND + s:, :] = zeros
        kg1[DIL_BAND:DIL_BAND + s, :] = k_ref[...]
        vg1[DIL_BAND:DIL_BAND + s, :] = v_ref[...]

        def body(c, carry):
            base = pl.multiple_of(c * tb, tb)
            for src, bufs in ((k_ref, k_bufs), (v_ref, v_bufs)):
                stage[...] = src[pl.ds(base, tb), :].astype(F32)
                deinterleave(bufs[d_mid], bufs[d_max], DIL_BAND + c * (tb // d_mid),
                             DIL_BAND + c * (tb // d_max), s)
            return carry

        lax.fori_loop(0, s // tb, body, 0)

    stage[...] = q_ref[...].astype(F32)
    deinterleave(q_perm[d_mid], q_perm[d_max], 0, 0, tb)

    tiles = []
    for b, d in enumerate(DIL_DILATIONS):
        sub_len = s // d
        per = tb // d
        n_tiles = per // DIL_TQ
        for r in range(d):
            for tj in range(n_tiles):
                tiles.append((b, d, sub_len, per, n_tiles, r, tj))

    def tile_stages(b, d, sub_len, per, n_tiles, r, tj):
        u0 = i * per + tj * DIL_TQ
        q_src = q_ref if d == 1 else q_perm[d]
        k_start = pl.multiple_of(r * sub_len + u0, DIL_TQ)
        variant = DIL_INTERIOR
        if tj == 0 or tj == n_tiles - 1:
            assert sub_len >= 2 * DIL_TQ
            variant = jnp.where(u0 == 0, DIL_FIRST, jnp.where(u0 + DIL_TQ == sub_len, DIL_LAST, DIL_INTERIOR))
        if d == 1:
            rows = pl.ds(tj * DIL_TQ, DIL_TQ)
        else:
            rows = pl.ds(r + d * tj * DIL_TQ, DIL_TQ, stride=d)
        return _dil_tile(lambda: q_src[r * per + tj * DIL_TQ:r * per + (tj + 1) * DIL_TQ, :],
                         lambda: k_bufs[d][pl.ds(k_start, DIL_TK), :],
                         lambda: v_bufs[d][pl.ds(k_start, DIL_TK), :],
                         lambda: bias_ref[b, variant], o_bufs[d], l_bufs[d], rows)

    for g0 in range(0, len(tiles), DIL_INTERLEAVE):
        _run_interleaved([tile_stages(*tile) for tile in tiles[g0:g0 + DIL_INTERLEAVE]])

    lses = [l_bufs[d][...] for d in DIL_DILATIONS]
    m = functools.reduce(jnp.maximum, lses)
    es = [jnp.exp(l - m) for l in lses]
    num = sum(e * o_bufs[d][...] for e, d in zip(es, DIL_DILATIONS))
    mixed = num / sum(es)
    out_ref[...] = _head_rms(mixed, gain_ref[...]).astype(out_ref.dtype)

    wo_out[...] = wo_ref[...].astype(wo_out.dtype)
    wd_out[...] = wd_ref[...].astype(wd_out.dtype)


def _dil_attention(proj, bias, gain, w_out, w_down, layer):
    s = proj.shape[0]
    tb = DIL_TB
    nq = s // tb
    step_of = lambda h, i: h * nq + i
    wo_in, wo_out, wo_sds = _cast_slab(w_out, layer, DIL_HEADS * nq, step_of)
    wd_in, wd_out, wd_sds = _cast_slab(w_down, layer, DIL_HEADS * nq, step_of)
    guarded = pltpu.VMEM((s + 2 * DIL_BAND, DIL_HEAD_DIM), BF16)
    tile_f32 = pltpu.VMEM((tb, DIL_HEAD_DIM), F32)
    tile_bf16 = pltpu.VMEM((tb, DIL_HEAD_DIM), BF16)
    return pl.pallas_call(
        _dil_kernel,
        grid=(DIL_HEADS, nq),
        in_specs=[
            pl.BlockSpec((tb, DIL_HEAD_DIM), lambda h, i: (i, COL_DQ + h)),
            pl.BlockSpec((s, DIL_HEAD_DIM), lambda h, i: (0, COL_DK + h)),
            pl.BlockSpec((s, DIL_HEAD_DIM), lambda h, i: (0, COL_DV + h)),
            pl.BlockSpec(bias.shape[:2] + (None, DIL_TQ, DIL_TK), lambda h, i: (0, 0, h, 0, 0)),
            pl.BlockSpec((None, 1, DIL_HEAD_DIM), lambda h, i: (layer, 0, h)),
            wo_in, wd_in,
        ],
        out_specs=[pl.BlockSpec((tb, DIL_HEAD_DIM), lambda h, i: (i, h)), wo_out, wd_out],
        out_shape=[jax.ShapeDtypeStruct((s, DIL_WIDTH), BF16), wo_sds, wd_sds],
        scratch_shapes=[tile_f32] * 2 + [guarded] * 6 + [tile_bf16] * 2 + [tile_f32] * 6,
        compiler_params=_params(2, BIG_VMEM_LIMIT),
        name="dil_attn",
    )(proj, proj, proj, bias, gain, w_out, w_down)


def _t5_bucket(rel):
    half = REL_BUCKETS // 2
    max_exact = half // 2
    ret = jnp.where(rel > 0, half, 0)
    n = jnp.abs(rel)
    nf = jnp.maximum(n, 1).astype(F32)
    large = max_exact + (jnp.log(nf / max_exact) / math.log(REL_MAX_DISTANCE / max_exact)
                         * (half - max_exact)).astype(jnp.int32)
    large = jnp.minimum(large, half - 1)
    return ret + jnp.where(n < max_exact, n, large)


DIL_INTERIOR, DIL_FIRST, DIL_LAST = 0, 1, 2


def _dil_bias_tiles(rel_bias):
    rel = jnp.arange(DIL_TK)[None, :] - DIL_BAND - jnp.arange(DIL_TQ)[:, None]
    band = jnp.abs(rel) <= DIL_BAND
    col = jnp.arange(DIL_TK)[None, :]
    inside = {DIL_INTERIOR: band, DIL_FIRST: band & (col >= DIL_BAND), DIL_LAST: band & (col < DIL_TK - DIL_BAND)}
    tiles = []
    for d in DIL_DILATIONS:
        onehot = (_t5_bucket(rel * d)[..., None] == jnp.arange(REL_BUCKETS)).astype(F32)
        bias = jnp.einsum("qkb,bh->hqk", onehot, rel_bias.astype(F32), precision=lax.Precision.HIGHEST)
        tiles.append(jnp.stack([jnp.where(inside[variant][None], bias, NEG_INF)
                                for variant in (DIL_INTERIOR, DIL_FIRST, DIL_LAST)], axis=0))
    return jnp.stack(tiles, axis=0)


def _mem_kv_kernel(mem_ref, g_ref, w_ref, o_ref):
    h = _rms(mem_ref[...], g_ref[...]).astype(BF16)
    o_ref[...] = _dot(h, w_ref[...].astype(BF16)).astype(o_ref.dtype)


def _mem_kv(mem, gain, w):
    m, d = mem.shape
    n = w.shape[-1]
    return pl.pallas_call(
        _mem_kv_kernel,
        grid=(DEPTH,),
        in_specs=[
            pl.BlockSpec((m, d), lambda l: (0, 0)),
            pl.BlockSpec((None, 1, d), lambda l: (l, 0, 0)),
            pl.BlockSpec((None, d, n), lambda l: (l, 0, 0)),
        ],
        out_specs=pl.BlockSpec((None, m, n), lambda l: (l, 0, 0)),
        out_shape=jax.ShapeDtypeStruct((DEPTH, m, n), BF16),
        compiler_params=_params(1),
        name="mem_kv",
    )(mem, gain, w)


def _head_rms(o, gain):
    return o * lax.rsqrt(jnp.mean(o * o, axis=-1, keepdims=True) + EPS) * gain


def _mix_kernel(of_ref, ob_ref, gr_ref, dil_ref, mq_ref, km_ref, vm_ref, ggla_ref, gmem_ref, w_ref, x_ref,
                *rest, prep_next):
    if prep_next:
        w_in_ref, out_ref, w_in_out, gla_ref, mem_ref = rest
        _w_in_prep_kernel(w_in_ref, w_in_out)
    else:
        out_ref, gla_ref, mem_ref = rest

    dil0 = GLA_V_WIDTH
    mem0 = GLA_V_WIDTH + DIL_WIDTH

    def mem_head(h):
        sl = slice(h * MEM_HEAD_DIM, (h + 1) * MEM_HEAD_DIM)
        sc = _dot_nt(mq_ref[:, sl], km_ref[:, sl])
        yield
        m = jnp.max(sc, axis=-1, keepdims=True)
        p = jnp.exp(sc - m)
        den = jnp.sum(p, axis=-1, keepdims=True)
        pv = _dot(p.astype(BF16), vm_ref[:, sl])
        yield
        mem_ref[:, sl] = _head_rms(pv / den, gmem_ref[:, sl]).astype(mem_ref.dtype)

    def gla_heads():
        for h in range(GLA_HEADS):
            sl = slice(h * GLA_DV, (h + 1) * GLA_DV)
            o = of_ref[:, sl].astype(F32) + ob_ref[:, sl].astype(F32)
            r = gr_ref[:, sl].astype(F32)
            gate = r * (1.0 / (1.0 + jnp.exp(-r)))
            gla_ref[:, sl] = (_head_rms(o, ggla_ref[:, sl]) * gate).astype(gla_ref.dtype)
        yield
        out_ref[...] += _dot(gla_ref[...], w_ref[0:dil0, :])

    def dil_group():
        out_ref[...] = x_ref[...] + _dot(dil_ref[...], w_ref[dil0:mem0, :])
        yield

    _run_interleaved([mem_head(h) for h in range(MEM_HEADS)] + [dil_group(), gla_heads()])
    out_ref[...] += _dot(mem_ref[...], w_ref[mem0:, :])


def _mix_out(x, proj, o_f, o_b, dil, kv, g_gla, g_mem, w_out, w_in_t, layer, t=512):
    s, d = x.shape
    n_steps = s // t
    prep_next = layer + 1 < w_in_t.shape[0]
    rows = lambda w, c=0: pl.BlockSpec((t, w), lambda i: (i, c))
    per_layer = lambda shape, c=0: pl.BlockSpec((None,) + shape, lambda i: (layer, 0, c),
                                                pipeline_mode=pl.Buffered(1))
    in_specs = [
        rows(GLA_V_WIDTH), rows(GLA_V_WIDTH),
        rows(GLA_V_WIDTH, COL_GR * LANES // GLA_V_WIDTH),
        rows(DIL_WIDTH),
        rows(MEM_WIDTH, COL_MQ * LANES // MEM_WIDTH),
        per_layer((MEM_LEN, MEM_WIDTH), 0), per_layer((MEM_LEN, MEM_WIDTH), 1),
        per_layer((1, GLA_V_WIDTH)), per_layer((1, MEM_WIDTH)),
        pl.BlockSpec((MIX_WIDTH, d), lambda i: (0, 0), pipeline_mode=pl.Buffered(1)),
        rows(d),
    ]
    operands = [o_f, o_b, proj, dil, proj, kv, kv, g_gla, g_mem, w_out, x]
    out_specs = [rows(d)]
    out_shape = [jax.ShapeDtypeStruct((s, d), F32)]
    if prep_next:
        n_in = w_in_t.shape[1]
        in_specs.append(pl.BlockSpec((None, n_in, d // n_steps), lambda i: (layer + 1, 0, i)))
        operands.append(w_in_t)
        out_specs.append(pl.BlockSpec((PROJ_WIDTH, d // n_steps), lambda i: (0, i)))
        out_shape.append(jax.ShapeDtypeStruct((PROJ_WIDTH, d), BF16))
    outs = pl.pallas_call(
        functools.partial(_mix_kernel, prep_next=prep_next),
        grid=(n_steps,),
        in_specs=in_specs,
        out_specs=out_specs,
        out_shape=out_shape,
        scratch_shapes=[pltpu.VMEM((t, GLA_V_WIDTH), BF16), pltpu.VMEM((t, MEM_WIDTH), BF16)],
        compiler_params=_params(1),
        name="mix_out_proj",
    )(*operands)
    return (outs[0], outs[1]) if prep_next else (outs[0], None)


def _mlp_kernel(x_ref, g_ref, wu_ref, wd_ref, gf_ref, o_ref, h_ref, *, final_norm, n_sub):
    f = pl.program_id(1)
    sub = x_ref.shape[0] // n_sub

    def hidden(h):
        return jnp.square(jnp.maximum(_dot(h, wu_ref[...]), 0.0)).astype(BF16)

    @pl.when(f == 0)
    def _():
        for c in range(n_sub):
            rows = slice(c * sub, (c + 1) * sub)
            x = x_ref[rows, :]
            h = _rms(x, g_ref[...]).astype(h_ref.dtype)
            h_ref[rows, :] = h
            o_ref[rows, :] = x + _dot(hidden(h), wd_ref[...])

    last = pl.num_programs(1) - 1
    middle = (f != 0) & (f != last) if final_norm else f != 0

    @pl.when(middle)
    def _():
        for c in range(n_sub):
            rows = slice(c * sub, (c + 1) * sub)
            o_ref[rows, :] += _dot(hidden(h_ref[rows, :]), wd_ref[...])

    if final_norm:
        @pl.when(f == last)
        def _():
            for c in range(n_sub):
                rows = slice(c * sub, (c + 1) * sub)
                acc = o_ref[rows, :] + _dot(hidden(h_ref[rows, :]), wd_ref[...])
                o_ref[rows, :] = _rms(acc, gf_ref[...])


def _mlp(x, gain, w_up, w_down, gain_final, layer, final_norm, tm=1024, tf=1024, n_sub=4):
    s, d = x.shape
    ff = w_up.shape[-1]
    return pl.pallas_call(
        functools.partial(_mlp_kernel, final_norm=final_norm, n_sub=n_sub),
        grid=(s // tm, ff // tf),
        in_specs=[
            pl.BlockSpec((tm, d), lambda i, f: (i, 0)),
            pl.BlockSpec((None, 1, d), lambda i, f: (layer, 0, 0)),
            pl.BlockSpec((d, tf), lambda i, f: (0, f)),
            pl.BlockSpec((tf, d), lambda i, f: (f, 0)),
            pl.BlockSpec((1, d), lambda i, f: (0, 0)),
        ],
        out_specs=pl.BlockSpec((tm, d), lambda i, f: (i, 0)),
        out_shape=jax.ShapeDtypeStruct((s, d), F32),
        scratch_shapes=[pltpu.VMEM((tm, d), BF16)],
        compiler_params=_params(2, BIG_VMEM_LIMIT),
        name="mlp",
    )(x, gain, w_up, w_down, gain_final)


def _w_in_prep_kernel(w_ref, o_ref):
    lr0 = 2 * GLA_QK_WIDTH + 2 * GLA_V_WIDTH
    lr_w = 2 * GLA_GATE_RANK
    shift = lr_w

    def copy(src0, rows, dst0, scale=None):
        block = w_ref[src0:src0 + rows, :]
        if scale is not None:
            block = block * scale
        o_ref[dst0:dst0 + rows, :] = block.astype(o_ref.dtype)

    copy(0, GLA_QK_WIDTH, 0, GLA_DK ** -0.5)
    copy(GLA_QK_WIDTH, lr0 - GLA_QK_WIDTH, GLA_QK_WIDTH)
    dq0 = lr0 + lr_w
    copy(dq0, DIL_WIDTH, dq0 - shift, DIL_HEAD_DIM ** -0.5)
    copy(dq0 + DIL_WIDTH, 2 * DIL_WIDTH, dq0 + DIL_WIDTH - shift)
    mq0 = dq0 + 3 * DIL_WIDTH
    copy(mq0, MEM_WIDTH, mq0 - shift, MEM_HEAD_DIM ** -0.5)
    end = mq0 + MEM_WIDTH
    assert end == w_ref.shape[0] and COL_LR * LANES == end - shift
    copy(lr0, lr_w, end - shift)
    o_ref[end:, :] = jnp.zeros((o_ref.shape[0] - end, o_ref.shape[1]), o_ref.dtype)


def _reorder_w_in(w_in_t, layer, cols=256):
    _, n, d = w_in_t.shape
    assert COL_LR * LANES == n - 2 * GLA_GATE_RANK
    return pl.pallas_call(
        _w_in_prep_kernel,
        grid=(d // cols,),
        in_specs=[pl.BlockSpec((None, n, cols), lambda i: (layer, 0, i))],
        out_specs=pl.BlockSpec((PROJ_WIDTH, cols), lambda i: (0, i)),
        out_shape=jax.ShapeDtypeStruct((PROJ_WIDTH, d), BF16),
        compiler_params=_params(1),
        name="w_in_prep",
    )(w_in_t)


def _pad_gate_up(up, row0):
    out = jnp.zeros((up.shape[0], LANES, up.shape[-1]), BF16)
    return out.at[:, row0:row0 + GLA_GATE_RANK, :].set(up.astype(BF16))


def kernel(x, mem, norm_mix, w_in, gla_gate_up_fwd, gla_gate_bias_fwd, gla_gate_up_bwd, gla_gate_bias_bwd, gla_norm, rel_bias, dil_norm, mem_norm, w_mem_kv, mem_out_norm, w_out, norm_mlp, w_up, w_down, norm_final):
    batch, seq, d = x.shape
    assert batch == 1 and seq == SEQ and d == D_MODEL
    xs = x.reshape(seq, d)
    row = lambda g: g.reshape(g.shape[0], 1, g.shape[-1])

    w_in_t = jnp.swapaxes(w_in, 1, 2)
    w_in_l = _reorder_w_in(w_in_t, 0)
    up_f = _pad_gate_up(gla_gate_up_fwd, 0)
    up_b = _pad_gate_up(gla_gate_up_bwd, GLA_GATE_RANK)
    dil_bias = _dil_bias_tiles(rel_bias)
    kv = _mem_kv(mem.reshape(MEM_LEN, d), row(mem_norm), w_mem_kv)

    for layer in range(DEPTH):
        proj = _in_proj(xs, row(norm_mix), w_in_l, layer)
        o_f, o_b, w_up_b = _gla(proj, up_f, up_b, row(gla_gate_bias_fwd), row(gla_gate_bias_bwd), w_up, layer)
        dil, w_out_b, w_down_b = _dil_attention(proj, dil_bias, row(dil_norm), w_out, w_down, layer)
        xs, w_in_l = _mix_out(xs, proj, o_f, o_b, dil, kv, row(gla_norm), row(mem_out_norm), w_out_b,
                              w_in_t, layer)
        xs = _mlp(xs, row(norm_mlp), w_up_b, w_down_b, norm_final.reshape(1, d), layer,
                  final_norm=(layer == DEPTH - 1))
    return xs.reshape(batch, seq, d)
```

```python
import functools
import math

import jax
import jax.numpy as jnp
from jax import lax
from jax.experimental import pallas as pl
from jax.experimental.pallas import tpu as pltpu

F32 = jnp.float32
BF16 = jnp.bfloat16

D_MODEL = 2048
SEQ = 8192
DEPTH = 4
MEM_LEN = 256
GLA_HEADS = 4
GLA_DK = 128
GLA_DV = 256
GLA_GATE_RANK = 16
GLA_GATE_NORMALIZER = 16.0
GLA_CHUNK = 64
DIL_HEADS = 4
DIL_HEAD_DIM = 128
DIL_CONFIGS = ((128, 1), (512, 4), (2048, 16))
MEM_HEADS = 4
MEM_HEAD_DIM = 128
REL_BUCKETS = 32
REL_MAX_DISTANCE = 1024
D_FF = 4 * D_MODEL
EPS = 1e-6
NEG_INF = -1e30

GLA_QK_WIDTH = GLA_HEADS * GLA_DK
GLA_V_WIDTH = GLA_HEADS * GLA_DV
DIL_WIDTH = DIL_HEADS * DIL_HEAD_DIM
MEM_WIDTH = MEM_HEADS * MEM_HEAD_DIM
MIX_WIDTH = GLA_V_WIDTH + DIL_WIDTH + MEM_WIDTH

LANES = 128

COL_GQ = 0
COL_GK = COL_GQ + GLA_QK_WIDTH // LANES
COL_GV = COL_GK + GLA_QK_WIDTH // LANES
COL_GR = COL_GV + GLA_V_WIDTH // LANES
COL_DQ = COL_GR + GLA_V_WIDTH // LANES
COL_DK = COL_DQ + DIL_WIDTH // LANES
COL_DV = COL_DK + DIL_WIDTH // LANES
COL_MQ = COL_DV + DIL_WIDTH // LANES
COL_LR = COL_MQ + MEM_WIDTH // LANES
PROJ_COLS = COL_LR + 2
PROJ_WIDTH = PROJ_COLS * LANES

GLA_BLOCK = 256

DIL_BAND = 64
DIL_TQ = 128
DIL_TK = DIL_TQ + 2 * DIL_BAND
DIL_DILATIONS = tuple(d for _, d in DIL_CONFIGS)
DIL_TB = DIL_TQ * max(DIL_DILATIONS)
DIL_INTERLEAVE = 8

VMEM_LIMIT = 48 * 1024 * 1024
BIG_VMEM_LIMIT = 60 * 1024 * 1024


def _params(n_axes, vmem=VMEM_LIMIT):
    return pltpu.CompilerParams(dimension_semantics=("arbitrary",) * n_axes, vmem_limit_bytes=vmem)


def _dot(a, b):
    return jnp.dot(a, b, preferred_element_type=F32)


def _dot_nt(a, b):
    return lax.dot_general(a, b, (((1,), (1,)), ((), ())), preferred_element_type=F32)


def _dot_tn(a, b):
    return lax.dot_general(a, b, (((0,), (0,)), ((), ())), preferred_element_type=F32)


def _rms(x, gain):
    return x * lax.rsqrt(jnp.mean(x * x, axis=-1, keepdims=True) + EPS) * gain


def _run_interleaved(stage_generators):
    live = list(stage_generators)
    while live:
        live = [gen for gen in live if next(gen, StopIteration) is not StopIteration]


def _in_proj_kernel(x_ref, g_ref, w_ref, o_ref, h_ref, *, n_sub):
    j = pl.program_id(1)
    sub = x_ref.shape[0] // n_sub

    @pl.when(j == 0)
    def _():
        for c in range(n_sub):
            rows = slice(c * sub, (c + 1) * sub)
            h = _rms(x_ref[rows, :], g_ref[...]).astype(h_ref.dtype)
            h_ref[rows, :] = h
            o_ref[rows, :] = _dot_nt(h, w_ref[...]).astype(o_ref.dtype)

    @pl.when(j != 0)
    def _():
        for c in range(n_sub):
            rows = slice(c * sub, (c + 1) * sub)
            o_ref[rows, :] = _dot_nt(h_ref[rows, :], w_ref[...]).astype(o_ref.dtype)


def _in_proj(x, gain, w, layer, tm=1024, tn=1792, n_sub=4):
    s, d = x.shape
    n = w.shape[0]
    return pl.pallas_call(
        functools.partial(_in_proj_kernel, n_sub=n_sub),
        grid=(s // tm, n // tn),
        in_specs=[
            pl.BlockSpec((tm, d), lambda i, j: (i, 0)),
            pl.BlockSpec((None, 1, d), lambda i, j: (layer, 0, 0)),
            pl.BlockSpec((tn, d), lambda i, j: (j, 0)),
        ],
        out_specs=pl.BlockSpec((tm, tn), lambda i, j: (i, j)),
        out_shape=jax.ShapeDtypeStruct((s, n), BF16),
        scratch_shapes=[pltpu.VMEM((tm, d), BF16)],
        compiler_params=_params(2),
        name="in_proj",
    )(x, gain, w)


def _log_sigmoid(x):
    return jnp.minimum(x, 0.0) - jnp.log(1.0 + jnp.exp(-jnp.abs(x)))


GLA_LEVELS = (GLA_BLOCK // GLA_CHUNK).bit_length() - 1


def _gla_tables(tri_ref, lvl_ref, reverse):
    t = GLA_BLOCK
    row = lax.broadcasted_iota(jnp.int32, (t, t), 0)
    col = lax.broadcasted_iota(jnp.int32, (t, t), 1)
    valid = (col >= row) if reverse else (col <= row)
    tri_ref[...] = jnp.where(valid, 1.0, 0.0).astype(tri_ref.dtype)
    shift = GLA_CHUNK.bit_length() - 1
    lvl_ref[...] = jnp.where(jnp.right_shift(row, shift) == jnp.right_shift(col, shift),
                             jnp.where(valid, 0, -1), -1)


def _gla_boundary(cum, group, reverse):
    t = cum.shape[0]
    pieces = []
    for a in range(0, t, group):
        idx = a + group if reverse else a - 1
        if idx < 0 or idx >= t:
            ref_row = jnp.zeros((1, cum.shape[1]), cum.dtype)
        else:
            ref_row = cum[idx:idx + 1]
        pieces.append(jnp.broadcast_to(ref_row, (group, cum.shape[1])))
    return jnp.concatenate(pieces, axis=0)


def _place_rows(part, row0, total_rows):
    pieces = []
    if row0 > 0:
        pieces.append(jnp.zeros((row0, part.shape[1]), part.dtype))
    pieces.append(part)
    rest = total_rows - row0 - part.shape[0]
    if rest > 0:
        pieces.append(jnp.zeros((rest, part.shape[1]), part.dtype))
    return jnp.concatenate(pieces, axis=0)


def _gla_block(q_ref, k_ref, v_ref, lr_ref, up_ref, bias_ref, o_ref, s_ref, tri_ref, lvl_ref, r0, reverse):
    t = GLA_BLOCK
    rows = slice(r0, r0 + t)
    logits = _dot(lr_ref[rows, :], up_ref[...]) + bias_ref[...]
    yield
    g = _log_sigmoid(logits) * (1.0 / GLA_GATE_NORMALIZER)
    g_hi = g.astype(BF16)
    g_lo = (g - g_hi.astype(F32)).astype(BF16)
    cum2 = _dot(tri_ref[...], jnp.concatenate([g_hi, g_lo], axis=1))
    yield
    cum = cum2[:, :GLA_DK] + cum2[:, GLA_DK:]

    q = q_ref[rows, :].astype(F32)
    k = k_ref[rows, :].astype(F32)
    ref0 = _gla_boundary(cum, GLA_CHUNK, reverse)
    q_dec = (q * jnp.exp(cum - ref0)).astype(BF16)
    k_inv = (k * jnp.exp(ref0 - cum)).astype(BF16)
    q_segs, k_segs = [], []
    for level in range(1, GLA_LEVELS + 1):
        group = GLA_CHUNK << level
        for a0 in range(0, t, group):
            mid = a0 + group // 2
            att = slice(a0, mid) if reverse else slice(mid, a0 + group)
            src = slice(mid, a0 + group) if reverse else slice(a0, mid)
            ref_row = cum[mid:mid + 1] if reverse else cum[mid - 1:mid]
            q_part = (q[att] * jnp.exp(cum[att] - ref_row)).astype(BF16)
            k_part = (k[src] * jnp.exp(ref_row - cum[src])).astype(BF16)
            q_segs.append(_place_rows(q_part, att.start, t))
            k_segs.append(_place_rows(k_part, src.start, t))
    cross = _dot_nt(jnp.concatenate(q_segs, axis=1), jnp.concatenate(k_segs, axis=1))
    diag = _dot_nt(q_dec, k_inv)
    total = cum[0:1] if reverse else cum[t - 1:t]
    q_in = (q * jnp.exp(cum)).astype(BF16)
    k_out = (k * jnp.exp(total - cum)).astype(BF16)
    carry_in = _dot_tn(v_ref[rows, :], k_out)
    yield
    a = jnp.where(lvl_ref[...] == 0, diag, cross)
    o_local = _dot(a.astype(BF16), v_ref[rows, :])
    yield
    state = s_ref[...]
    o = o_local + _dot_nt(q_in, state.astype(BF16))
    o_ref[rows, :] = o.astype(o_ref.dtype)
    s_ref[...] = state * jnp.exp(total) + carry_in


def _cast_slab(weights, layer, n_steps, step_of):
    rows, cols = weights.shape[1:]
    slab = rows // n_steps
    return (pl.BlockSpec((None, slab, cols), lambda *g: (layer, step_of(*g), 0)),
            pl.BlockSpec((slab, cols), lambda *g: (step_of(*g), 0)),
            jax.ShapeDtypeStruct((rows, cols), BF16))


def _gla_kernel(qf, kf, vf, lrf, qb, kb, vb, lrb, upf, upb, bsf, bsb, wu_ref, of, ob, wu_out,
                sf, sb, tri_f, tri_b, lvl_f, lvl_b, *, interleave):
    @pl.when(pl.program_id(1) == 0)
    def _():
        sf[...] = jnp.zeros_like(sf)
        sb[...] = jnp.zeros_like(sb)
        _gla_tables(tri_f, lvl_f, reverse=False)
        _gla_tables(tri_b, lvl_b, reverse=True)

    starts = list(range(0, qf.shape[0], GLA_BLOCK))
    n_groups = len(starts) // interleave

    def cast_piece(piece):
        n = wu_ref.shape[0] // n_groups
        wu_out[piece * n:(piece + 1) * n, :] = wu_ref[piece * n:(piece + 1) * n, :].astype(wu_out.dtype)

    for gi in range(n_groups):
        blocks = []
        for r_fwd, r_bwd in list(zip(starts, reversed(starts)))[gi * interleave:(gi + 1) * interleave]:
            blocks.append(_gla_block(qf, kf, vf, lrf, upf, bsf, of, sf, tri_f, lvl_f, r_fwd, reverse=False))
            blocks.append(_gla_block(qb, kb, vb, lrb, upb, bsb, ob, sb, tri_b, lvl_b, r_bwd, reverse=True))
        _run_interleaved(blocks)
        cast_piece(gi)


def _gla(proj, up_f, up_b, bias_f, bias_b, w_up, layer, t=2048, interleave=8):
    s = proj.shape[0]
    nb = s // t
    vcol = COL_GV * LANES // GLA_DV
    wu_in, wu_out, wu_sds = _cast_slab(w_up, layer, GLA_HEADS * nb, lambda h, i: h * nb + i)

    def row_specs(rmap):
        return [
            pl.BlockSpec((t, GLA_DK), lambda h, i: (rmap(i), COL_GQ + h)),
            pl.BlockSpec((t, GLA_DK), lambda h, i: (rmap(i), COL_GK + h)),
            pl.BlockSpec((t, GLA_DV), lambda h, i: (rmap(i), vcol + h)),
            pl.BlockSpec((t, LANES), lambda h, i: (rmap(i), COL_LR)),
        ]

    fwd = lambda i: i
    bwd = lambda i: nb - 1 - i
    up_spec = pl.BlockSpec((None, LANES, GLA_DK), lambda h, i: (layer, 0, h))
    bias_spec = pl.BlockSpec((None, 1, GLA_DK), lambda h, i: (layer, 0, h))
    out_sds = jax.ShapeDtypeStruct((s, GLA_V_WIDTH), BF16)
    state = pltpu.VMEM((GLA_DV, GLA_DK), F32)
    tri = pltpu.VMEM((GLA_BLOCK, GLA_BLOCK), BF16)
    lvl = pltpu.VMEM((GLA_BLOCK, GLA_BLOCK), jnp.int32)
    return pl.pallas_call(
        functools.partial(_gla_kernel, interleave=interleave),
        grid=(GLA_HEADS, nb),
        in_specs=row_specs(fwd) + row_specs(bwd) + [up_spec, up_spec, bias_spec, bias_spec, wu_in],
        out_specs=[
            pl.BlockSpec((t, GLA_DV), lambda h, i: (fwd(i), h)),
            pl.BlockSpec((t, GLA_DV), lambda h, i: (bwd(i), h)),
            wu_out,
        ],
        out_shape=[out_sds, out_sds, wu_sds],
        scratch_shapes=[state, state, tri, tri, lvl, lvl],
        compiler_params=_params(2),
        name="gla_scan",
    )(proj, proj, proj, proj, proj, proj, proj, proj, up_f, up_b, bias_f, bias_b, w_up)


def _dil_tile(load_q, load_k, load_v, load_bias, o_dst, l_dst, rows):
    raw = _dot_nt(load_q(), load_k())
    yield
    sc = raw + load_bias()
    m = jnp.max(sc, axis=-1, keepdims=True)
    p = jnp.exp(sc - m).astype(BF16)
    v = load_v()
    pv = _dot(p, jnp.concatenate([v, jnp.ones_like(v)], axis=1))
    yield
    den = pv[:, DIL_HEAD_DIM:]
    o_dst[rows, :] = pv[:, :DIL_HEAD_DIM] / den
    l_dst[rows, :] = m + jnp.log(den)


def _dil_kernel(q_ref, k_ref, v_ref, bias_ref, gain_ref, wo_ref, wd_ref, out_ref, wo_out, wd_out,
                stage, stage_mid, kg1, vg1, kg4, vg4, kg16, vg16, qp4, qp16, o1, o4, o16, l1, l4, l16):
    i = pl.program_id(1)
    s = k_ref.shape[0]
    tb = q_ref.shape[0]
    k_bufs = dict(zip(DIL_DILATIONS, (kg1, kg4, kg16)))
    v_bufs = dict(zip(DIL_DILATIONS, (vg1, vg4, vg16)))
    q_perm = dict(zip(DIL_DILATIONS, (None, qp4, qp16)))
    o_bufs = dict(zip(DIL_DILATIONS, (o1, o4, o16)))
    l_bufs = dict(zip(DIL_DILATIONS, (l1, l4, l16)))
    _, d_mid, d_max = DIL_DILATIONS
    assert d_max == d_mid * d_mid

    def deinterleave(dst_mid, dst_max, base_mid, base_max, residue_rows):
        per_mid, per_max = tb // d_mid, tb // d_max

        def at(base, offset):
            start = base + offset
            return start if isinstance(start, int) else pl.multiple_of(start, DIL_BAND)

        for r in range(d_mid):
            part = stage[pl.ds(r, per_mid, stride=d_mid), :]
            stage_mid[r * per_mid:(r + 1) * per_mid, :] = part
            dst_mid[pl.ds(at(base_mid, r * (residue_rows // d_mid)), per_mid), :] = part.astype(dst_mid.dtype)
        for r in range(d_max):
            r_mid, m = r % d_mid, r // d_mid
            part = stage_mid[pl.ds(r_mid * per_mid + m, per_max, stride=d_mid), :]
            dst_max[pl.ds(at(base_max, r * (residue_rows // d_max)), per_max), :] = part.astype(dst_max.dtype)

    @pl.when(i == 0)
    def _():
        zeros = jnp.zeros((DIL_BAND, DIL_HEAD_DIM), BF16)
        for buf in (kg1, vg1, kg4, vg4, kg16, vg16):
            buf[0:DIL_BAND, :] = zeros
            buf[DIL_BAND + s:, :] = zeros
        kg1[DIL_BAND:DIL_BAND + s, :] = k_ref[...]
        vg1[DIL_BAND:DIL_BAND + s, :] = v_ref[...]

        def body(c, carry):
            base = pl.multiple_of(c * tb, tb)
            for src, bufs in ((k_ref, k_bufs), (v_ref, v_bufs)):
                stage[...] = src[pl.ds(base, tb), :].astype(F32)
                deinterleave(bufs[d_mid], bufs[d_max], DIL_BAND + c * (tb // d_mid),
                             DIL_BAND + c * (tb // d_max), s)
            return carry

        lax.fori_loop(0, s // tb, body, 0)

    stage[...] = q_ref[...].astype(F32)
    deinterleave(q_perm[d_mid], q_perm[d_max], 0, 0, tb)

    tiles = []
    for b, d in enumerate(DIL_DILATIONS):
        sub_len = s // d
        per = tb // d
        n_tiles = per // DIL_TQ
        for r in range(d):
            for tj in range(n_tiles):
                tiles.append((b, d, sub_len, per, n_tiles, r, tj))

    def tile_stages(b, d, sub_len, per, n_tiles, r, tj):
        u0 = i * per + tj * DIL_TQ
        q_src = q_ref if d == 1 else q_perm[d]
        k_start = pl.multiple_of(r * sub_len + u0, DIL_TQ)
        variant = DIL_INTERIOR
        if tj == 0 or tj == n_tiles - 1:
            assert sub_len >= 2 * DIL_TQ
            variant = jnp.where(u0 == 0, DIL_FIRST, jnp.where(u0 + DIL_TQ == sub_len, DIL_LAST, DIL_INTERIOR))
        if d == 1:
            rows = pl.ds(tj * DIL_TQ, DIL_TQ)
        else:
            rows = pl.ds(r + d * tj * DIL_TQ, DIL_TQ, stride=d)
        return _dil_tile(lambda: q_src[r * per + tj * DIL_TQ:r * per + (tj + 1) * DIL_TQ, :],
                         lambda: k_bufs[d][pl.ds(k_start, DIL_TK), :],
                         lambda: v_bufs[d][pl.ds(k_start, DIL_TK), :],
                         lambda: bias_ref[b, variant], o_bufs[d], l_bufs[d], rows)

    for g0 in range(0, len(tiles), DIL_INTERLEAVE):
        _run_interleaved([tile_stages(*tile) for tile in tiles[g0:g0 + DIL_INTERLEAVE]])

    lses = [l_bufs[d][...] for d in DIL_DILATIONS]
    m = functools.reduce(jnp.maximum, lses)
    es = [jnp.exp(l - m) for l in lses]
    num = sum(e * o_bufs[d][...] for e, d in zip(es, DIL_DILATIONS))
    mixed = num / sum(es)
    out_ref[...] = _head_rms(mixed, gain_ref[...]).astype(out_ref.dtype)

    wo_out[...] = wo_ref[...].astype(wo_out.dtype)
    wd_out[...] = wd_ref[...].astype(wd_out.dtype)


def _dil_attention(proj, bias, gain, w_out, w_down, layer):
    s = proj.shape[0]
    tb = DIL_TB
    nq = s // tb
    step_of = lambda h, i: h * nq + i
    wo_in, wo_out, wo_sds = _cast_slab(w_out, layer, DIL_HEADS * nq, step_of)
    wd_in, wd_out, wd_sds = _cast_slab(w_down, layer, DIL_HEADS * nq, step_of)
    guarded = pltpu.VMEM((s + 2 * DIL_BAND, DIL_HEAD_DIM), BF16)
    tile_f32 = pltpu.VMEM((tb, DIL_HEAD_DIM), F32)
    tile_bf16 = pltpu.VMEM((tb, DIL_HEAD_DIM), BF16)
    return pl.pallas_call(
        _dil_kernel,
        grid=(DIL_HEADS, nq),
        in_specs=[
            pl.BlockSpec((tb, DIL_HEAD_DIM), lambda h, i: (i, COL_DQ + h)),
            pl.BlockSpec((s, DIL_HEAD_DIM), lambda h, i: (0, COL_DK + h)),
            pl.BlockSpec((s, DIL_HEAD_DIM), lambda h, i: (0, COL_DV + h)),
            pl.BlockSpec(bias.shape[:2] + (None, DIL_TQ, DIL_TK), lambda h, i: (0, 0, h, 0, 0)),
            pl.BlockSpec((None, 1, DIL_HEAD_DIM), lambda h, i: (layer, 0, h)),
            wo_in, wd_in,
        ],
        out_specs=[pl.BlockSpec((tb, DIL_HEAD_DIM), lambda h, i: (i, h)), wo_out, wd_out],
        out_shape=[jax.ShapeDtypeStruct((s, DIL_WIDTH), BF16), wo_sds, wd_sds],
        scratch_shapes=[tile_f32] * 2 + [guarded] * 6 + [tile_bf16] * 2 + [tile_f32] * 6,
        compiler_params=_params(2, BIG_VMEM_LIMIT),
        name="dil_attn",
    )(proj, proj, proj, bias, gain, w_out, w_down)


def _t5_bucket(rel):
    half = REL_BUCKETS // 2
    max_exact = half // 2
    ret = jnp.where(rel > 0, half, 0)
    n = jnp.abs(rel)
    nf = jnp.maximum(n, 1).astype(F32)
    large = max_exact + (jnp.log(nf / max_exact) / math.log(REL_MAX_DISTANCE / max_exact)
                         * (half - max_exact)).astype(jnp.int32)
    large = jnp.minimum(large, half - 1)
    return ret + jnp.where(n < max_exact, n, large)


DIL_INTERIOR, DIL_FIRST, DIL_LAST = 0, 1, 2


def _dil_bias_tiles(rel_bias):
    rel = jnp.arange(DIL_TK)[None, :] - DIL_BAND - jnp.arange(DIL_TQ)[:, None]
    band = jnp.abs(rel) <= DIL_BAND
    col = jnp.arange(DIL_TK)[None, :]
    inside = {DIL_INTERIOR: band, DIL_FIRST: band & (col >= DIL_BAND), DIL_LAST: band & (col < DIL_TK - DIL_BAND)}
    tiles = []
    for d in DIL_DILATIONS:
        onehot = (_t5_bucket(rel * d)[..., None] == jnp.arange(REL_BUCKETS)).astype(F32)
        bias = jnp.einsum("qkb,bh->hqk", onehot, rel_bias.astype(F32), precision=lax.Precision.HIGHEST)
        tiles.append(jnp.stack([jnp.where(inside[variant][None], bias, NEG_INF)
                                for variant in (DIL_INTERIOR, DIL_FIRST, DIL_LAST)], axis=0))
    return jnp.stack(tiles, axis=0)


def _mem_kv_kernel(mem_ref, g_ref, w_ref, o_ref):
    h = _rms(mem_ref[...], g_ref[...]).astype(BF16)
    o_ref[...] = _dot(h, w_ref[...].astype(BF16)).astype(o_ref.dtype)


def _mem_kv(mem, gain, w):
    m, d = mem.shape
    n = w.shape[-1]
    return pl.pallas_call(
        _mem_kv_kernel,
        grid=(DEPTH,),
        in_specs=[
            pl.BlockSpec((m, d), lambda l: (0, 0)),
            pl.BlockSpec((None, 1, d), lambda l: (l, 0, 0)),
            pl.BlockSpec((None, d, n), lambda l: (l, 0, 0)),
        ],
        out_specs=pl.BlockSpec((None, m, n), lambda l: (l, 0, 0)),
        out_shape=jax.ShapeDtypeStruct((DEPTH, m, n), BF16),
        compiler_params=_params(1),
        name="mem_kv",
    )(mem, gain, w)


def _head_rms(o, gain):
    return o * lax.rsqrt(jnp.mean(o * o, axis=-1, keepdims=True) + EPS) * gain


def _mix_kernel(of_ref, ob_ref, gr_ref, dil_ref, mq_ref, km_ref, vm_ref, ggla_ref, gmem_ref, w_ref, x_ref,
                *rest, prep_next):
    if prep_next:
        w_in_ref, out_ref, w_in_out, gla_ref, mem_ref = rest
    else:
        out_ref, gla_ref, mem_ref = rest

    dil0 = GLA_V_WIDTH
    mem0 = GLA_V_WIDTH + DIL_WIDTH

    def mem_head(h):
        sl = slice(h * MEM_HEAD_DIM, (h + 1) * MEM_HEAD_DIM)
        sc = _dot_nt(mq_ref[:, sl], km_ref[:, sl])
        yield
        m = jnp.max(sc, axis=-1, keepdims=True)
        p = jnp.exp(sc - m)
        den = jnp.sum(p, axis=-1, keepdims=True)
        pv = _dot(p.astype(BF16), vm_ref[:, sl])
        yield
        mem_ref[:, sl] = _head_rms(pv / den, gmem_ref[:, sl]).astype(mem_ref.dtype)

    def gla_heads():
        for h in range(GLA_HEADS):
            sl = slice(h * GLA_DV, (h + 1) * GLA_DV)
            o = of_ref[:, sl].astype(F32) + ob_ref[:, sl].astype(F32)
            r = gr_ref[:, sl].astype(F32)
            gate = r * (1.0 / (1.0 + jnp.exp(-r)))
            gla_ref[:, sl] = (_head_rms(o, ggla_ref[:, sl]) * gate).astype(gla_ref.dtype)
        yield
        out_ref[...] += _dot(gla_ref[...], w_ref[0:dil0, :])

    def dil_group():
        out_ref[...] = x_ref[...] + _dot(dil_ref[...], w_ref[dil0:mem0, :])
        yield

    def prep_group():
        yield
        if prep_next:
            _w_in_prep_kernel(w_in_ref, w_in_out)

    _run_interleaved([mem_head(h) for h in range(MEM_HEADS)] + [dil_group(), gla_heads(), prep_group()])
    out_ref[...] += _dot(mem_ref[...], w_ref[mem0:, :])


def _mix_out(x, proj, o_f, o_b, dil, kv, g_gla, g_mem, w_out, w_in_t, layer, t=512):
    s, d = x.shape
    n_steps = s // t
    prep_next = layer + 1 < w_in_t.shape[0]
    rows = lambda w, c=0: pl.BlockSpec((t, w), lambda i: (i, c))
    per_layer = lambda shape, c=0: pl.BlockSpec((None,) + shape, lambda i: (layer, 0, c),
                                                pipeline_mode=pl.Buffered(1))
    in_specs = [
        rows(GLA_V_WIDTH), rows(GLA_V_WIDTH),
        rows(GLA_V_WIDTH, COL_GR * LANES // GLA_V_WIDTH),
        rows(DIL_WIDTH),
        rows(MEM_WIDTH, COL_MQ * LANES // MEM_WIDTH),
        per_layer((MEM_LEN, MEM_WIDTH), 0), per_layer((MEM_LEN, MEM_WIDTH), 1),
        per_layer((1, GLA_V_WIDTH)), per_layer((1, MEM_WIDTH)),
        pl.BlockSpec((MIX_WIDTH, d), lambda i: (0, 0), pipeline_mode=pl.Buffered(1)),
        rows(d),
    ]
    operands = [o_f, o_b, proj, dil, proj, kv, kv, g_gla, g_mem, w_out, x]
    out_specs = [rows(d)]
    out_shape = [jax.ShapeDtypeStruct((s, d), F32)]
    if prep_next:
        n_in = w_in_t.shape[1]
        in_specs.append(pl.BlockSpec((None, n_in, d // n_steps), lambda i: (layer + 1, 0, i)))
        operands.append(w_in_t)
        out_specs.append(pl.BlockSpec((PROJ_WIDTH, d // n_steps), lambda i: (0, i)))
        out_shape.append(jax.ShapeDtypeStruct((PROJ_WIDTH, d), BF16))
    outs = pl.pallas_call(
        functools.partial(_mix_kernel, prep_next=prep_next),
        grid=(n_steps,),
        in_specs=in_specs,
        out_specs=out_specs,
        out_shape=out_shape,
        scratch_shapes=[pltpu.VMEM((t, GLA_V_WIDTH), BF16), pltpu.VMEM((t, MEM_WIDTH), BF16)],
        compiler_params=_params(1),
        name="mix_out_proj",
    )(*operands)
    return (outs[0], outs[1]) if prep_next else (outs[0], None)


def _mlp_kernel(x_ref, g_ref, wu_ref, wd_ref, gf_ref, o_ref, h_ref, *, final_norm, n_sub):
    f = pl.program_id(1)
    sub = x_ref.shape[0] // n_sub

    def hidden(h):
        return jnp.square(jnp.maximum(_dot(h, wu_ref[...]), 0.0)).astype(BF16)

    @pl.when(f == 0)
    def _():
        for c in range(n_sub):
            rows = slice(c * sub, (c + 1) * sub)
            x = x_ref[rows, :]
            h = _rms(x, g_ref[...]).astype(h_ref.dtype)
            h_ref[rows, :] = h
            o_ref[rows, :] = x + _dot(hidden(h), wd_ref[...])

    last = pl.num_programs(1) - 1
    middle = (f != 0) & (f != last) if final_norm else f != 0

    @pl.when(middle)
    def _():
        for c in range(n_sub):
            rows = slice(c * sub, (c + 1) * sub)
            o_ref[rows, :] += _dot(hidden(h_ref[rows, :]), wd_ref[...])

    if final_norm:
        @pl.when(f == last)
        def _():
            for c in range(n_sub):
                rows = slice(c * sub, (c + 1) * sub)
                acc = o_ref[rows, :] + _dot(hidden(h_ref[rows, :]), wd_ref[...])
                o_ref[rows, :] = _rms(acc, gf_ref[...])


def _mlp(x, gain, w_up, w_down, gain_final, layer, final_norm, tm=1024, tf=1024, n_sub=4):
    s, d = x.shape
    ff = w_up.shape[-1]
    return pl.pallas_call(
        functools.partial(_mlp_kernel, final_norm=final_norm, n_sub=n_sub),
        grid=(s // tm, ff // tf),
        in_specs=[
            pl.BlockSpec((tm, d), lambda i, f: (i, 0)),
            pl.BlockSpec((None, 1, d), lambda i, f: (layer, 0, 0)),
            pl.BlockSpec((d, tf), lambda i, f: (0, f)),
            pl.BlockSpec((tf, d), lambda i, f: (f, 0)),
            pl.BlockSpec((1, d), lambda i, f: (0, 0)),
        ],
        out_specs=pl.BlockSpec((tm, d), lambda i, f: (i, 0)),
        out_shape=jax.ShapeDtypeStruct((s, d), F32),
        scratch_shapes=[pltpu.VMEM((tm, d), BF16)],
        compiler_params=_params(2, BIG_VMEM_LIMIT),
        name="mlp",
    )(x, gain, w_up, w_down, gain_final)


def _w_in_prep_kernel(w_ref, o_ref):
    lr0 = 2 * GLA_QK_WIDTH + 2 * GLA_V_WIDTH
    lr_w = 2 * GLA_GATE_RANK
    shift = lr_w

    def copy(src0, rows, dst0, scale=None):
        block = w_ref[src0:src0 + rows, :]
        if scale is not None:
            block = block * scale
        o_ref[dst0:dst0 + rows, :] = block.astype(o_ref.dtype)

    copy(0, GLA_QK_WIDTH, 0, GLA_DK ** -0.5)
    copy(GLA_QK_WIDTH, lr0 - GLA_QK_WIDTH, GLA_QK_WIDTH)
    dq0 = lr0 + lr_w
    copy(dq0, DIL_WIDTH, dq0 - shift, DIL_HEAD_DIM ** -0.5)
    copy(dq0 + DIL_WIDTH, 2 * DIL_WIDTH, dq0 + DIL_WIDTH - shift)
    mq0 = dq0 + 3 * DIL_WIDTH
    copy(mq0, MEM_WIDTH, mq0 - shift, MEM_HEAD_DIM ** -0.5)
    end = mq0 + MEM_WIDTH
    assert end == w_ref.shape[0] and COL_LR * LANES == end - shift
    copy(lr0, lr_w, end - shift)
    o_ref[end:, :] = jnp.zeros((o_ref.shape[0] - end, o_ref.shape[1]), o_ref.dtype)


def _reorder_w_in(w_in_t, layer, cols=256):
    _, n, d = w_in_t.shape
    assert COL_LR * LANES == n - 2 * GLA_GATE_RANK
    return pl.pallas_call(
        _w_in_prep_kernel,
        grid=(d // cols,),
        in_specs=[pl.BlockSpec((None, n, cols), lambda i: (layer, 0, i))],
        out_specs=pl.BlockSpec((PROJ_WIDTH, cols), lambda i: (0, i)),
        out_shape=jax.ShapeDtypeStruct((PROJ_WIDTH, d), BF16),
        compiler_params=_params(1),
        name="w_in_prep",
    )(w_in_t)


def _pad_gate_up(up, row0):
    out = jnp.zeros((up.shape[0], LANES, up.shape[-1]), BF16)
    return out.at[:, row0:row0 + GLA_GATE_RANK, :].set(up.astype(BF16))


def kernel(x, mem, norm_mix, w_in, gla_gate_up_fwd, gla_gate_bias_fwd, gla_gate_up_bwd, gla_gate_bias_bwd, gla_norm, rel_bias, dil_norm, mem_norm, w_mem_kv, mem_out_norm, w_out, norm_mlp, w_up, w_down, norm_final):
    batch, seq, d = x.shape
    assert batch == 1 and seq == SEQ and d == D_MODEL
    xs = x.reshape(seq, d)
    row = lambda g: g.reshape(g.shape[0], 1, g.shape[-1])

    w_in_t = jnp.swapaxes(w_in, 1, 2)
    w_in_l = _reorder_w_in(w_in_t, 0)
    up_f = _pad_gate_up(gla_gate_up_fwd, 0)
    up_b = _pad_gate_up(gla_gate_up_bwd, GLA_GATE_RANK)
    dil_bias = _dil_bias_tiles(rel_bias)
    kv = _mem_kv(mem.reshape(MEM_LEN, d), row(mem_norm), w_mem_kv)

    for layer in range(DEPTH):
        proj = _in_proj(xs, row(norm_mix), w_in_l, layer)
        o_f, o_b, w_up_b = _gla(proj, up_f, up_b, row(gla_gate_bias_fwd), row(gla_gate_bias_bwd), w_up, layer)
        dil, w_out_b, w_down_b = _dil_attention(proj, dil_bias, row(dil_norm), w_out, w_down, layer)
        xs, w_in_l = _mix_out(xs, proj, o_f, o_b, dil, kv, row(gla_norm), row(mem_out_norm), w_out_b,
                              w_in_t, layer)
        xs = _mlp(xs, row(norm_mlp), w_up_b, w_down_b, norm_final.reshape(1, d), layer,
                  final_norm=(layer == DEPTH - 1))
    return xs.reshape(batch, seq, d)
```
